```python
import jax, jax.numpy as jnp
from jax import lax
import numpy as np

D_MODEL = 1024
BATCH = 8
SEQ = 4096
DEPTH = 2

CTX_LEN = 256
GRID_W = 64
EPS = 1e-6
NEG_INF = -1e30

FOURIER_WIDTH = D_MODEL // 4
FOURIER_GROUPS = 4
FOURIER_GD = FOURIER_WIDTH // FOURIER_GROUPS
HGRN_WIDTH = D_MODEL // 4
HGRN_DK = 64
HGRN_HEADS = HGRN_WIDTH // HGRN_DK
HGRN_CHUNK = 32
NA_WIDTH = D_MODEL - FOURIER_WIDTH - HGRN_WIDTH
NA_HEAD_DIM = 64
NA_HEADS = NA_WIDTH // NA_HEAD_DIM
NA_KH = 8
NA_KW = 16
NA_QCB = 16
NA_KCB = NA_QCB + NA_KW
MIX_WIDTH = FOURIER_WIDTH + HGRN_WIDTH + NA_WIDTH
IN_WIDTH = FOURIER_WIDTH + 5 * HGRN_WIDTH + 3 * NA_WIDTH
D_FF = -(-8 * D_MODEL // (3 * 256)) * 256

kernel_name = "hybrid_fourier_hgrn2_natten_dit_block"


def rmsnorm(x, g):
    xf = x.astype(jnp.float32)
    y = xf * lax.rsqrt(jnp.mean(xf * xf, axis=-1, keepdims=True) + EPS)
    return (y * g.astype(jnp.float32)).astype(x.dtype)


def modulate(h, shift, scale):
    return h * (1 + scale) + shift


def swiglu(h, w_gate, w_up, w_down):
    return (jax.nn.silu(h @ w_gate) * (h @ w_up)) @ w_down


def fourier_mix(u, w_f):
    B, T, _ = u.shape
    ug = u.astype(jnp.float32).reshape(B, T, FOURIER_GROUPS, FOURIER_GD)
    spec = jnp.fft.fft2(ug, axes=(1, 3), norm="ortho").real.astype(u.dtype)
    out = jnp.einsum('btgc,gcd->btgd', spec, w_f)
    return out.reshape(B, T, FOURIER_WIDTH)


def gla_chunk_scan(q, k, v, log_f, s0):
    B, H, T, dk = q.shape
    C = HGRN_CHUNK
    n = T // C

    def blocks(a):
        return a.reshape(B, H, n, C, a.shape[-1]).transpose(2, 0, 1, 3, 4)

    qb, kb, vb, gb = blocks(q), blocks(k), blocks(v), blocks(log_f)
    a = jnp.cumsum(gb, axis=3)
    a_last = a[:, :, :, -1:, :]
    q_dec = qb * jnp.exp(a)
    k_dec = kb * jnp.exp(-a)
    k_state = kb * jnp.exp(a_last - a)
    incl = jnp.tril(jnp.ones((C, C), dtype=bool))
    scores = jnp.where(incl, jnp.einsum('nbhtd,nbhsd->nbhts', q_dec, k_dec), 0.0)
    o_intra = jnp.einsum('nbhts,nbhsv->nbhtv', scores, vb)

    def step(s, inp):
        qd, ks, vv, al = inp
        o_inter = jnp.einsum('bhtd,bhdv->bhtv', qd, s)
        s_new = jnp.exp(al[:, :, 0, :])[..., None] * s + jnp.einsum('bhsd,bhsv->bhdv', ks, vv)
        return s_new, o_inter

    s_final, o_inter = lax.scan(step, s0, (q_dec, k_state, vb, a_last))
    o = (o_intra + o_inter).transpose(1, 2, 0, 3, 4).reshape(B, H, T, vb.shape[-1])
    return o, s_final


def hgrn2_mix(u_lat, u_ctx, lb, norm_g, with_ctx):
    def heads(a):
        B, T, _ = a.shape
        return a.astype(jnp.float32).reshape(B, T, HGRN_HEADS, HGRN_DK).transpose(0, 2, 1, 3)

    def split(u):
        q, f_fw, f_bw, i, g = jnp.split(u, 5, axis=-1)
        return heads(q), heads(f_fw), heads(f_bw), heads(i), g

    def gate(z, lb_d):
        f = lb_d + (1.0 - lb_d) * jax.nn.sigmoid(z)
        return 1.0 - f, jnp.log(f)

    def rev(a):
        return jnp.flip(a, axis=2)

    q_l, ff_l, fb_l, i_l, g_l = split(u_lat)
    q_c, ff_c, fb_c, i_c, g_c = split(u_ctx)
    lb_f = lb[0].reshape(HGRN_HEADS, 1, HGRN_DK)
    lb_b = lb[1].reshape(HGRN_HEADS, 1, HGRN_DK)
    s0 = jnp.zeros((u_lat.shape[0], HGRN_HEADS, HGRN_DK, HGRN_DK), jnp.float32)

    k, lf = gate(ff_c, lb_f)
    o_cf, s_cf = gla_chunk_scan(q_c, k, i_c, lf, s0)
    k, lf = gate(ff_l, lb_f)
    o_lf, _ = gla_chunk_scan(q_l, k, i_l, lf, s_cf)
    k, lf = gate(rev(fb_c), lb_b)
    o_cb, s_cb = gla_chunk_scan(rev(q_c), k, rev(i_c), lf, s0)
    k, lf = gate(rev(fb_l), lb_b)
    o_lb, _ = gla_chunk_scan(rev(q_l), k, rev(i_l), lf, s_cb)

    def readout(o, g):
        B, H, T, dv = o.shape
        o = o * lax.rsqrt(jnp.mean(o * o, axis=-1, keepdims=True) + EPS) * norm_g.astype(jnp.float32)
        o = o.transpose(0, 2, 1, 3).reshape(B, T, H * dv)
        return (o * jax.nn.silu(g.astype(jnp.float32))).astype(g.dtype)

    out_lat = readout(o_lf + rev(o_lb), g_l)
    out_ctx = readout(o_cf + rev(o_cb), g_c) if with_ctx else None
    return out_lat, out_ctx


def na_heads(u):
    B, T, _ = u.shape
    q, k, v = jnp.split(u, 3, axis=-1)
    return (q.reshape(B, T, NA_HEADS, NA_HEAD_DIM), k.reshape(B, T, NA_HEADS, NA_HEAD_DIM),
            v.reshape(B, T, NA_HEADS, NA_HEAD_DIM))


def na_latent(q, k, v, k_c, v_c, rpb):
    B, S, H, hd = q.shape
    rows = S // GRID_W
    kh = min(NA_KH, rows)
    ncb = GRID_W // NA_QCB
    scale = hd ** -0.5
    qg = q.reshape(B, rows, GRID_W, H, hd)
    kg = k.reshape(B, rows, GRID_W, H, hd)
    vg = v.reshape(B, rows, GRID_W, H, hd)

    band_start = np.clip(np.arange(ncb) * NA_QCB - NA_KW // 2, 0, GRID_W - NA_KCB)
    key_cols = band_start[:, None] + np.arange(NA_KCB)
    q_cols = np.arange(GRID_W).reshape(ncb, NA_QCB)
    win_col = np.clip(q_cols - NA_KW // 2, 0, GRID_W - NA_KW)[..., None]
    kc = key_cols[:, None, :]
    col_valid = (kc >= win_col) & (kc < win_col + NA_KW)
    col_idx = np.clip(kc - q_cols[..., None] + NA_KW - 1, 0, 2 * NA_KW - 2)
    col_valid = col_valid[:, :, None, :]
    col_idx = col_idx[:, :, None, :]
    row_start = jnp.clip(jnp.arange(rows) - kh // 2, 0, rows - kh)
    rpb32 = rpb.astype(jnp.float32)

    def one_row(args):
        r, rs = args
        qr = lax.dynamic_index_in_dim(qg, r, axis=1, keepdims=False)
        kr = lax.dynamic_slice_in_dim(kg, rs, kh, axis=1)
        vr = lax.dynamic_slice_in_dim(vg, rs, kh, axis=1)
        kb = kr[:, :, key_cols]
        vb = vr[:, :, key_cols]
        qb = qr.reshape(B, ncb, NA_QCB, H, hd)
        s_win = jnp.einsum('bjqhd,bkjchd->bhjqkc', qb, kb).astype(jnp.float32) * scale
        row_idx = (rs + jnp.arange(kh) - r + NA_KH - 1)[None, None, :, None]
        bias = rpb32[:, row_idx, col_idx]
        s_win = jnp.where(col_valid, s_win + bias[None], NEG_INF)
        s_ctx = jnp.einsum('bjqhd,blhd->bhjql', qb, k_c).astype(jnp.float32) * scale
        s_all = jnp.concatenate([s_win.reshape(B, H, ncb, NA_QCB, kh * NA_KCB), s_ctx], axis=-1)
        p = jax.nn.softmax(s_all, axis=-1).astype(v.dtype)
        p_win = p[..., :kh * NA_KCB].reshape(B, H, ncb, NA_QCB, kh, NA_KCB)
        p_ctx = p[..., kh * NA_KCB:]
        o = (jnp.einsum('bhjqkc,bkjchd->bjqhd', p_win, vb)
             + jnp.einsum('bhjql,blhd->bjqhd', p_ctx, v_c))
        return o.reshape(B, GRID_W, H * hd)

    out = lax.map(one_row, (jnp.arange(rows), row_start))
    return out.transpose(1, 0, 2, 3).reshape(B, S, H * hd)


def na_context(q_c, k_c, v_c):
    B, L, H, hd = q_c.shape
    s = jnp.einsum('blhd,bmhd->bhlm', q_c, k_c).astype(jnp.float32) * hd ** -0.5
    p = jax.nn.softmax(s, axis=-1).astype(v_c.dtype)
    return jnp.einsum('bhlm,bmhd->blhd', p, v_c).reshape(B, L, H * hd)


def mix_heads(a_lat, a_ctx, w_in, fourier_w, lb, hgrn_norm_g, rpb, w_out, with_ctx):
    u_lat = a_lat @ w_in
    u_ctx = a_ctx @ w_in
    f_end = FOURIER_WIDTH
    h_end = f_end + 5 * HGRN_WIDTH
    hg_lat, hg_ctx = hgrn2_mix(u_lat[..., f_end:h_end], u_ctx[..., f_end:h_end], lb, hgrn_norm_g, with_ctx)
    q_l, k_l, v_l = na_heads(u_lat[..., h_end:])
    q_c, k_c, v_c = na_heads(u_ctx[..., h_end:])
    o_lat = jnp.concatenate([fourier_mix(u_lat[..., :f_end], fourier_w), hg_lat,
                             na_latent(q_l, k_l, v_l, k_c, v_c, rpb)], axis=-1) @ w_out
    if not with_ctx:
        return o_lat, None
    o_ctx = jnp.concatenate([fourier_mix(u_ctx[..., :f_end], fourier_w), hg_ctx,
                             na_context(q_c, k_c, v_c)], axis=-1) @ w_out
    return o_lat, o_ctx


def setup_inputs(seed: int = 0) -> dict:
    key = jax.random.key(seed)
    ks = jax.random.split(key, 18)
    D = D_MODEL

    def nrm(k, shape, s):
        return jax.random.normal(k, shape, jnp.float32) * s

    return {
        "x": nrm(ks[0], (BATCH, SEQ, D), 1.0),
        "c": nrm(ks[1], (BATCH, D), 1.0),
        "ctx": nrm(ks[2], (BATCH, CTX_LEN, D), 1.0),
        "c_ctx": nrm(ks[3], (D,), 1.0),
        "w_mod": nrm(ks[4], (DEPTH, D, 6 * D), 0.5 * D ** -0.5),
        "b_mod": nrm(ks[5], (DEPTH, 6 * D), 0.02),
        "norm1_g": 1.0 + nrm(ks[6], (DEPTH, D), 0.02),
        "w_in": nrm(ks[7], (DEPTH, D, IN_WIDTH), D ** -0.5),
        "fourier_w": nrm(ks[8], (DEPTH, FOURIER_GROUPS, FOURIER_GD, FOURIER_GD), FOURIER_GD ** -0.5),
        "hgrn_lb": nrm(ks[9], (DEPTH, 2, HGRN_WIDTH), 0.5),
        "hgrn_norm_g": 1.0 + nrm(ks[10], (DEPTH, HGRN_DK), 0.02),
        "na_rpb": nrm(ks[11], (DEPTH, NA_HEADS, 2 * NA_KH - 1, 2 * NA_KW - 1), 0.1),
        "w_out": nrm(ks[12], (DEPTH, MIX_WIDTH, D), MIX_WIDTH ** -0.5),
        "norm2_g": 1.0 + nrm(ks[13], (DEPTH, D), 0.02),
        "w_ffn_gate": nrm(ks[14], (DEPTH, D, D_FF), D ** -0.5),
        "w_ffn_up": nrm(ks[15], (DEPTH, D, D_FF), D ** -0.5),
        "w_ffn_down": nrm(ks[16], (DEPTH, D_FF, D), D_FF ** -0.5),
        "final_norm_g": 1.0 + nrm(ks[17], (D,), 0.02),
    }


def reference(x, c, ctx, c_ctx, w_mod, b_mod, norm1_g, w_in, fourier_w, hgrn_lb, hgrn_norm_g,
              na_rpb, w_out, norm2_g, w_ffn_gate, w_ffn_up, w_ffn_down, final_norm_g):
    lbp = jax.nn.softmax(hgrn_lb.astype(jnp.float32), axis=0)
    lower_bounds = jnp.cumsum(lbp, axis=0) - lbp[0:1]
    h_ctx = ctx
    for l in range(DEPTH):
        with_ctx = l < DEPTH - 1
        mod = jax.nn.silu(c) @ w_mod[l] + b_mod[l]
        mod_c = jax.nn.silu(c_ctx) @ w_mod[l] + b_mod[l]
        sh1, sc1, g1, sh2, sc2, g2 = jnp.split(mod[:, None, :], 6, axis=-1)
        csh1, csc1, cg1, csh2, csc2, cg2 = jnp.split(mod_c, 6, axis=-1)

        a_lat = modulate(rmsnorm(x, norm1_g[l]), sh1, sc1)
        a_ctx = modulate(rmsnorm(h_ctx, norm1_g[l]), csh1, csc1)
        y_lat, y_ctx = mix_heads(a_lat, a_ctx, w_in[l], fourier_w[l], lower_bounds[l],
                                 hgrn_norm_g[l], na_rpb[l], w_out[l], with_ctx)
        x = x + g1 * y_lat
        x = x + g2 * swiglu(modulate(rmsnorm(x, norm2_g[l]), sh2, sc2),
                            w_ffn_gate[l], w_ffn_up[l], w_ffn_down[l])
        if with_ctx:
            h_ctx = h_ctx + cg1 * y_ctx
            h_ctx = h_ctx + cg2 * swiglu(modulate(rmsnorm(h_ctx, norm2_g[l]), csh2, csc2),
                                         w_ffn_gate[l], w_ffn_up[l], w_ffn_down[l])
    return rmsnorm(x, final_norm_g)
```

```python
import functools

import jax
import jax.numpy as jnp
import numpy as np
from jax import lax
from jax.experimental import pallas as pl
from jax.experimental.pallas import tpu as pltpu

F32 = jnp.float32
BF16 = jnp.bfloat16

EPS = 1e-6
NEG_INF = -1e30
GRID_W = 64
FOURIER_GROUPS = 4
FOURIER_GD = 64
HGRN_DK = 64
HGRN_HEADS = 4
HGRN_CHUNK = 32
NA_HEAD_DIM = 64
NA_HEADS = 8
NA_KH = 8
NA_KW = 16
MIX = 256
NA_WIDTH = NA_HEADS * NA_HEAD_DIM
MOD_ROWS = 16
FF_CHUNK = 256
VMEM_LIMIT = 56 * 1024 * 1024


def _cparams(n_axes):
    return pltpu.CompilerParams(dimension_semantics=("arbitrary",) * n_axes,
                                vmem_limit_bytes=VMEM_LIMIT)


def _split2(a):
    hi = a.astype(BF16)
    lo = (a - hi.astype(F32)).astype(BF16)
    return hi, lo


def _dot(a, b):
    return jnp.dot(a, b, preferred_element_type=F32)


def _dot_nt(a, b):
    return lax.dot_general(a, b, (((1,), (1,)), ((), ())), preferred_element_type=F32)


def _dot_tn(a, b):
    return lax.dot_general(a, b, (((0,), (0,)), ((), ())), preferred_element_type=F32)


def _dot_hilo(a, b):
    ah, al = _split2(a)
    bh, bl = _split2(b)
    return _dot(ah, bh) + _dot(al, bh) + _dot(ah, bl)


def _sigmoid(z):
    return 1.0 / (1.0 + jnp.exp(-z))


def _const_spec(shape):
    nd = len(shape)
    return pl.BlockSpec(shape, lambda *_: (0,) * nd, pipeline_mode=pl.Buffered(1))


def _mod_kernel(c_ref, w_ref, b_ref, o_ref):
    c = c_ref[...]
    s = c * _sigmoid(c)
    o_ref[0] = _dot_hilo(s, w_ref[0]) + b_ref[0]


def _modulation(cc, w_mod, b_mod):
    depth, d, n = w_mod.shape
    tn = 1024
    return pl.pallas_call(
        _mod_kernel,
        grid=(depth, n // tn),
        in_specs=[pl.BlockSpec((MOD_ROWS, d), lambda l, j: (0, 0)),
                  pl.BlockSpec((1, d, tn), lambda l, j: (l, 0, j)),
                  pl.BlockSpec((1, 1, tn), lambda l, j: (l, 0, j))],
        out_specs=pl.BlockSpec((1, MOD_ROWS, tn), lambda l, j: (l, 0, j)),
        out_shape=jax.ShapeDtypeStruct((depth, MOD_ROWS, n), F32),
        compiler_params=_cparams(2),
        name="modulation",
    )(cc, w_mod, b_mod.reshape(depth, 1, n))


def _inproj_kernel(x_ref, g_ref, sh_ref, sc_ref, w_ref,
                   uf_ref, hq_ref, zf_ref, zb_ref, hi_ref, hg_ref, q_ref, k_ref, v_ref):
    x = x_ref[...]
    y = x * lax.rsqrt(jnp.mean(x * x, axis=-1, keepdims=True) + EPS) * g_ref[...]
    a = (y * (1.0 + sc_ref[0]) + sh_ref[0]).astype(BF16)
    col = 0
    for ref in (uf_ref, hq_ref, zf_ref, zb_ref, hi_ref, hg_ref):
        ref[...] = _dot(a, w_ref[:, col:col + MIX]).astype(ref.dtype)
        col += MIX
    scale = NA_HEAD_DIM ** -0.5
    q_ref[...] = (_dot(a, w_ref[:, col:col + NA_WIDTH]) * scale).astype(BF16)
    col += NA_WIDTH
    k_ref[...] = _dot(a, w_ref[:, col:col + NA_WIDTH]).astype(BF16)
    col += NA_WIDTH
    v_ref[...] = _dot(a, w_ref[:, col:col + NA_WIDTH]).astype(BF16)


def _inproj(x2d, norm_g, modv, mod_base, rows_per_mod, w_in_b, tm):
    n, d = x2d.shape
    in_w = w_in_b.shape[1]

    def mod_idx(j):
        if rows_per_mod is None:
            return lambda i: (mod_base * 6 + j, 0, 0)
        tpm = rows_per_mod // tm
        return lambda i: ((mod_base + i // tpm) * 6 + j, 0, 0)

    def tok(w):
        return pl.BlockSpec((tm, w), lambda i: (i, 0))

    outs = [(MIX, BF16), (MIX, BF16), (MIX, F32), (MIX, F32), (MIX, BF16), (MIX, BF16),
            (NA_WIDTH, BF16), (NA_WIDTH, BF16), (NA_WIDTH, BF16)]
    return pl.pallas_call(
        _inproj_kernel,
        grid=(n // tm,),
        in_specs=[tok(d),
                  pl.BlockSpec((1, d), lambda i: (0, 0)),
                  pl.BlockSpec((1, 1, d), mod_idx(0)),
                  pl.BlockSpec((1, 1, d), mod_idx(1)),
                  _const_spec((d, in_w))],
        out_specs=[tok(w) for w, _ in outs],
        out_shape=[jax.ShapeDtypeStruct((n, w), dt) for w, dt in outs],
        compiler_params=_cparams(1),
        name="inproj",
    )(x2d, norm_g.reshape(1, d), modv, modv, w_in_b)


def _dft_table(t_len):
    kb = min(64, t_len)
    assert t_len % kb == 0
    t = jnp.arange(t_len, dtype=jnp.int32)[None, :]
    k_hi = jnp.arange(t_len // kb, dtype=jnp.int32)[:, None] * kb
    k_lo = jnp.arange(kb, dtype=jnp.int32)[:, None]
    w = 2.0 * np.pi / t_len
    ang_hi = ((k_hi * t) % t_len).astype(F32) * w
    ang_lo = ((k_lo * t) % t_len).astype(F32) * w
    ch, sh = jnp.cos(ang_hi)[:, None, :], jnp.sin(ang_hi)[:, None, :]
    cl, sl = jnp.cos(ang_lo)[None, :, :], jnp.sin(ang_lo)[None, :, :]
    cos = (ch * cl - sh * sl).reshape(t_len, t_len)
    sin = (sh * cl + ch * sl).reshape(t_len, t_len)
    return jnp.concatenate([cos, sin], axis=1).astype(BF16)


def _channel_dft_consts():
    cd = np.outer(np.arange(FOURIER_GD), np.arange(FOURIER_GD)) % FOURIER_GD
    ang = 2.0 * np.pi * cd / FOURIER_GD
    eye = np.eye(FOURIER_GROUPS)
    return (jnp.asarray(np.kron(eye, np.cos(ang)), F32),
            jnp.asarray(np.kron(eye, np.sin(ang)), F32))


def _fourier_kernel(u_ref, tab_ref, c64_ref, s64_ref, w_ref, o_ref, pq_ref, *, t_len):
    @pl.when(pl.program_id(1) == 0)
    def _():
        norm = (FOURIER_GD * t_len) ** -0.5
        w = w_ref[...]
        a = (_dot_hilo(c64_ref[...], w) * norm).astype(BF16)
        b = (_dot_hilo(s64_ref[...], w) * (-norm)).astype(BF16)
        u = u_ref[0]
        pq_ref[0:t_len, :] = _dot(u, a).astype(BF16)
        pq_ref[t_len:2 * t_len, :] = _dot(u, b).astype(BF16)

    o_ref[0] = _dot(tab_ref[...], pq_ref[...]).astype(BF16)


def _fourier(u, tab, c64, s64, w_bd):
    b, t_len, _ = u.shape
    tm = min(512, t_len)
    return pl.pallas_call(
        functools.partial(_fourier_kernel, t_len=t_len),
        grid=(b, t_len // tm),
        in_specs=[pl.BlockSpec((1, t_len, MIX), lambda i, j: (i, 0, 0)),
                  pl.BlockSpec((tm, 2 * t_len), lambda i, j: (j, 0)),
                  _const_spec((MIX, MIX)), _const_spec((MIX, MIX)), _const_spec((MIX, MIX))],
        out_specs=pl.BlockSpec((1, tm, MIX), lambda i, j: (i, j, 0)),
        out_shape=jax.ShapeDtypeStruct((b, t_len, MIX), BF16),
        scratch_shapes=[pltpu.VMEM((2 * t_len, MIX), BF16)],
        compiler_params=_cparams(2),
        name="fourier",
    )(u, tab, c64, s64, w_bd)


def _hgrn_kernel(ql_ref, zfl_ref, zbl_ref, il_ref, gl_ref,
                 qc_ref, zfc_ref, zbc_ref, ic_ref, gc_ref, lb_ref, ng_ref,
                 *rest, n_lat, n_ctx, with_ctx):
    if with_ctx:
        ol_ref, oc_ref, st_ref, osl_ref, osc_ref = rest
    else:
        ol_ref, st_ref, osl_ref = rest
        oc_ref = osc_ref = None
    C = HGRN_CHUNK
    W = MIX
    r_i = lax.broadcasted_iota(jnp.int32, (C, C), 0)
    c_i = lax.broadcasted_iota(jnp.int32, (C, C), 1)
    tri = (c_i <= r_i, c_i >= r_i)
    tri_b16 = tuple(jnp.where(m, 1.0, 0.0).astype(BF16) for m in tri)
    r4 = lax.broadcasted_iota(jnp.int32, (HGRN_HEADS * C, C), 0) % C
    c4 = lax.broadcasted_iota(jnp.int32, (HGRN_HEADS * C, C), 1)
    tri4 = (c4 <= r4, c4 >= r4)
    lane_head = lax.broadcasted_iota(jnp.int32, (1, W), 1) // HGRN_DK
    head_mask = [lane_head == h for h in range(HGRN_HEADS)]
    bd = (lax.broadcasted_iota(jnp.int32, (W, W), 0) // HGRN_DK
          == lax.broadcasted_iota(jnp.int32, (W, W), 1) // HGRN_DK)
    bd_ones = jnp.where(bd, 1.0, 0.0).astype(BF16)

    def step(q_ref, z_ref, i_ref, n, d, need_o):
        rows = pl.ds(pl.multiple_of(n * C, C), C)
        z = z_ref[0, rows, :]
        v = i_ref[0, rows, :]
        lb = lb_ref[d:d + 1, :]
        f = lb + (1.0 - lb) * _sigmoid(z)
        kf = 1.0 - f
        lf = jnp.log(f)
        h1 = lf.astype(BF16)
        r1 = lf - h1.astype(F32)
        h2 = r1.astype(BF16)
        h3 = (r1 - h2.astype(F32)).astype(BF16)
        a = _dot(tri_b16[d], h1) + _dot(tri_b16[d], h2) + _dot(tri_b16[d], h3)
        kd = kf * jnp.exp(-a)
        eal = jnp.exp(a[C - 1:C, :] if d == 0 else a[0:1, :])
        ksb = (kd * eal).astype(BF16)
        st = st_ref[d]
        o = None
        if need_o:
            q = q_ref[0, rows, :].astype(F32)
            qdb = (q * jnp.exp(a)).astype(BF16)
            kdb = kd.astype(BF16)
            qs = jnp.concatenate([jnp.where(m, qdb, jnp.zeros_like(qdb)) for m in head_mask], axis=0)
            sc = jnp.where(tri4[d], _dot_nt(qs, kdb), 0.0).astype(BF16)
            oi = _dot(sc, v)
            o = _dot_nt(qdb, st.astype(BF16))
            for h in range(HGRN_HEADS):
                o = o + jnp.where(head_mask[h], oi[h * C:(h + 1) * C, :], 0.0)
        st_ref[d] = st * eal + jnp.where(bd, _dot_tn(v, ksb), 0.0)
        return o

    def readout(o, g_ref, n):
        rows = pl.ds(pl.multiple_of(n * C, C), C)
        hi, lo = _split2(o * o)
        ms = (_dot(hi, bd_ones) + _dot(lo, bd_ones)) * (1.0 / HGRN_DK)
        y = o * lax.rsqrt(ms + EPS) * ng_ref[...]
        g = g_ref[0, rows, :].astype(F32)
        return (y * (g * _sigmoid(g))).astype(BF16)

    def scan(q_ref, zf_ref, zb_ref, i_ref, g_ref, os_ref, o_ref, n_chunks):
        half = n_chunks // 2
        emit = o_ref is not None

        def first(n, carry):
            nb = n_chunks - 1 - n
            of = step(q_ref, zf_ref, i_ref, n, 0, emit)
            ob = step(q_ref, zb_ref, i_ref, nb, 1, emit)
            if emit:
                os_ref[pl.ds(pl.multiple_of(n * C, C), C), :] = of
                os_ref[pl.ds(pl.multiple_of(nb * C, C), C), :] = ob
            return carry

        def second(n, carry):
            nb = n_chunks - 1 - n
            of = step(q_ref, zf_ref, i_ref, n, 0, emit)
            ob = step(q_ref, zb_ref, i_ref, nb, 1, emit)
            if emit:
                rf = pl.ds(pl.multiple_of(n * C, C), C)
                rb = pl.ds(pl.multiple_of(nb * C, C), C)
                o_ref[0, rf, :] = readout(os_ref[rf, :] + of, g_ref, n)
                o_ref[0, rb, :] = readout(os_ref[rb, :] + ob, g_ref, nb)
            return carry

        lax.fori_loop(0, half, first, 0)
        lax.fori_loop(half, n_chunks, second, 0)

    st_ref[...] = jnp.zeros_like(st_ref)
    scan(qc_ref, zfc_ref, zbc_ref, ic_ref, gc_ref, osc_ref, oc_ref, n_ctx)
    scan(ql_ref, zfl_ref, zbl_ref, il_ref, gl_ref, osl_ref, ol_ref, n_lat)


def _hgrn(lat, ctx, lb, norm_g4, with_ctx):
    b, s, _ = lat[0].shape
    l = ctx[0].shape[1]
    assert s % (2 * HGRN_CHUNK) == 0 and l % (2 * HGRN_CHUNK) == 0

    def seq(t):
        return pl.BlockSpec((1, t, MIX), lambda i: (i, 0, 0), pipeline_mode=pl.Buffered(1))

    out_specs = [pl.BlockSpec((1, s, MIX), lambda i: (i, 0, 0))]
    out_shape = [jax.ShapeDtypeStruct((b, s, MIX), BF16)]
    scratch = [pltpu.VMEM((2, MIX, MIX), F32), pltpu.VMEM((s, MIX), F32)]
    if with_ctx:
        out_specs.append(pl.BlockSpec((1, l, MIX), lambda i: (i, 0, 0)))
        out_shape.append(jax.ShapeDtypeStruct((b, l, MIX), BF16))
        scratch.append(pltpu.VMEM((l, MIX), F32))
    res = pl.pallas_call(
        functools.partial(_hgrn_kernel, n_lat=s // HGRN_CHUNK, n_ctx=l // HGRN_CHUNK,
                          with_ctx=with_ctx),
        grid=(b,),
        in_specs=[seq(s)] * 5 + [seq(l)] * 5 + [_const_spec((2, MIX)), _const_spec((1, MIX))],
        out_specs=out_specs,
        out_shape=out_shape,
        scratch_shapes=scratch,
        compiler_params=_cparams(1),
        name="hgrn",
    )(*lat, *ctx, lb, norm_g4)
    return (res[0], res[1]) if with_ctx else (res[0], None)


def _na_bias_table(rpb):
    di = np.arange(NA_KH)[:, None, None, None]
    qc = np.arange(GRID_W)[None, :, None, None]
    kr = np.arange(NA_KH)[None, None, :, None]
    kc = np.arange(GRID_W)[None, None, None, :]
    win = np.clip(qc - NA_KW // 2, 0, GRID_W - NA_KW)
    valid = np.broadcast_to((kc >= win) & (kc < win + NA_KW), (NA_KH, GRID_W, NA_KH, GRID_W))
    ri = np.broadcast_to(kr - di + NA_KH - 1, valid.shape)
    ci = np.broadcast_to(np.clip(kc - qc + NA_KW - 1, 0, 2 * NA_KW - 2), valid.shape)
    tab = jnp.where(valid[None], rpb.astype(F32)[:, ri, ci], NEG_INF)
    return tab.reshape(rpb.shape[0], NA_KH, GRID_W, NA_KH * GRID_W)


def _softmax_pv(s_list, v_list):
    m = functools.reduce(jnp.maximum, [jnp.max(s, axis=-1, keepdims=True) for s in s_list])
    p_list = [jnp.exp(s - m) for s in s_list]
    l = functools.reduce(jnp.add, [jnp.sum(p, axis=-1, keepdims=True) for p in p_list])
    o = functools.reduce(jnp.add, [_dot(p.astype(BF16), v) for p, v in zip(p_list, v_list)])
    return o * (1.0 / l)


def _na_kernel(q_ref, k_ref, v_ref, kc_ref, vc_ref, bias_ref, o_ref, *, rows):
    lane = lax.broadcasted_iota(jnp.int32, (1, 2 * NA_HEAD_DIM), 1)
    first = lane < NA_HEAD_DIM
    kc = kc_ref[0]
    vc = vc_ref[0]
    win = NA_KH * GRID_W

    def body(r, carry):
        rs = jnp.clip(r - NA_KH // 2, 0, rows - NA_KH)
        di = r - rs
        qrows = pl.ds(pl.multiple_of(r * GRID_W, GRID_W), GRID_W)
        krows = pl.ds(pl.multiple_of(rs * GRID_W, GRID_W), win)
        q = q_ref[0, qrows, :]
        kw = k_ref[0, krows, :]
        vw = v_ref[0, krows, :]
        outs = []
        for h in range(2):
            qh = jnp.where(first if h == 0 else ~first, q, jnp.zeros_like(q))
            sw = _dot_nt(qh, kw) + bias_ref[h, di]
            sx = _dot_nt(qh, kc)
            outs.append(_softmax_pv([sw, sx], [vw, vc]))
        o_ref[0, qrows, :] = jnp.where(first, outs[0], outs[1]).astype(BF16)
        return carry

    lax.fori_loop(0, rows, body, 0)


def _na_latent(q, k, v, kc, vc, bias):
    b, s, _ = q.shape
    l = kc.shape[1]
    rows = s // GRID_W
    assert s % GRID_W == 0 and rows >= NA_KH
    hp = 2 * NA_HEAD_DIM

    def seq(t):
        return pl.BlockSpec((1, t, hp), lambda i, j: (i, 0, j))

    return pl.pallas_call(
        functools.partial(_na_kernel, rows=rows),
        grid=(b, NA_HEADS // 2),
        in_specs=[seq(s), seq(s), seq(s), seq(l), seq(l),
                  pl.BlockSpec((2, NA_KH, GRID_W, NA_KH * GRID_W), lambda i, j: (j, 0, 0, 0))],
        out_specs=seq(s),
        out_shape=jax.ShapeDtypeStruct((b, s, NA_WIDTH), BF16),
        compiler_params=_cparams(2),
        name="na_latent",
    )(q, k, v, kc, vc, bias)


def _na_ctx_kernel(q_ref, k_ref, v_ref, o_ref):
    lane = lax.broadcasted_iota(jnp.int32, (1, 2 * NA_HEAD_DIM), 1)
    first = lane < NA_HEAD_DIM
    q, k, v = q_ref[0], k_ref[0], v_ref[0]
    outs = []
    for h in range(2):
        qh = jnp.where(first if h == 0 else ~first, q, jnp.zeros_like(q))
        outs.append(_softmax_pv([_dot_nt(qh, k)], [v]))
    o_ref[0] = jnp.where(first, outs[0], outs[1]).astype(BF16)


def _na_context(q, k, v):
    b, l, _ = q.shape
    spec = pl.BlockSpec((1, l, 2 * NA_HEAD_DIM), lambda i, j: (i, 0, j))
    return pl.pallas_call(
        _na_ctx_kernel,
        grid=(b, NA_HEADS // 2),
        in_specs=[spec, spec, spec],
        out_specs=spec,
        out_shape=jax.ShapeDtypeStruct((b, l, NA_WIDTH), BF16),
        compiler_params=_cparams(2),
        name="na_context",
    )(q, k, v)


def _ffn_kernel(x_ref, f_ref, h_ref, n_ref, g1_ref, sh_ref, sc_ref, g2_ref, ng_ref, fg_ref,
                wo_ref, wg_ref, wu_ref, wd_ref, o_ref, *, final):
    y = (_dot(f_ref[...], wo_ref[0:MIX, :]) + _dot(h_ref[...], wo_ref[MIX:2 * MIX, :])
         + _dot(n_ref[...], wo_ref[2 * MIX:, :]))
    x1 = x_ref[...] + g1_ref[0] * y
    hn = x1 * lax.rsqrt(jnp.mean(x1 * x1, axis=-1, keepdims=True) + EPS) * ng_ref[...]
    hm = (hn * (1.0 + sc_ref[0]) + sh_ref[0]).astype(BF16)
    d_ff = wg_ref.shape[1]
    acc = jnp.zeros(x1.shape, F32)
    for j in range(d_ff // FF_CHUNK):
        cols = slice(j * FF_CHUNK, (j + 1) * FF_CHUNK)
        g = _dot(hm, wg_ref[:, cols])
        u = _dot(hm, wu_ref[:, cols])
        act = (g * _sigmoid(g) * u).astype(BF16)
        acc = acc + _dot(act, wd_ref[cols, :])
    x2 = x1 + g2_ref[0] * acc
    if final:
        x2 = x2 * lax.rsqrt(jnp.mean(x2 * x2, axis=-1, keepdims=True) + EPS) * fg_ref[...]
    o_ref[...] = x2


def _ffn(x2d, f, h, n, norm_g, final_g, modv, mod_base, rows_per_mod, wo, wg, wu, wd, tm, final):
    nt, d = x2d.shape
    d_ff = wg.shape[1]
    assert d_ff % FF_CHUNK == 0

    def mod_idx(j):
        if rows_per_mod is None:
            return lambda i: (mod_base * 6 + j, 0, 0)
        tpm = rows_per_mod // tm
        return lambda i: ((mod_base + i // tpm) * 6 + j, 0, 0)

    def tok(w):
        return pl.BlockSpec((tm, w), lambda i: (i, 0))

    def vec():
        return pl.BlockSpec((1, d), lambda i: (0, 0))

    return pl.pallas_call(
        functools.partial(_ffn_kernel, final=final),
        grid=(nt // tm,),
        in_specs=[tok(d), tok(MIX), tok(MIX), tok(NA_WIDTH),
                  pl.BlockSpec((1, 1, d), mod_idx(2)), pl.BlockSpec((1, 1, d), mod_idx(3)),
                  pl.BlockSpec((1, 1, d), mod_idx(4)), pl.BlockSpec((1, 1, d), mod_idx(5)),
                  vec(), vec(),
                  _const_spec(wo.shape), _const_spec(wg.shape), _const_spec(wu.shape),
                  _const_spec(wd.shape)],
        out_specs=tok(d),
        out_shape=jax.ShapeDtypeStruct((nt, d), F32),
        compiler_params=_cparams(1),
        name="outproj_ffn",
    )(x2d, f, h, n, modv, modv, modv, modv, norm_g.reshape(1, d), final_g.reshape(1, d),
      wo, wg, wu, wd)


def _token_tile(n):
    for tm in (512, 256, 128, 64, 32, 16, 8):
        if n % tm == 0:
            return tm
    raise ValueError(f"token count {n} is not a multiple of 8")


def kernel(x, c, ctx, c_ctx, w_mod, b_mod, norm1_g, w_in, fourier_w, hgrn_lb, hgrn_norm_g, na_rpb,
           w_out, norm2_g, w_ffn_gate, w_ffn_up, w_ffn_down, final_norm_g):
    b, s, d = x.shape
    l = ctx.shape[1]
    depth = w_mod.shape[0]
    assert b < MOD_ROWS and d == 2 * MIX + NA_WIDTH

    cc = jnp.zeros((MOD_ROWS, d), F32).at[:b].set(c).at[b].set(c_ctx)
    mod = _modulation(cc, w_mod, b_mod)
    modv = mod.reshape(depth * MOD_ROWS * 6, 1, d)

    lbp = jax.nn.softmax(hgrn_lb.astype(F32), axis=0)
    lower = jnp.cumsum(lbp, axis=0) - lbp[0:1]
    norm_g4 = jnp.tile(hgrn_norm_g.astype(F32), (1, HGRN_HEADS))[:, None, :]

    tab_s, tab_l = _dft_table(s), _dft_table(l)
    c64, s64 = _channel_dft_consts()
    eye_g = jnp.eye(FOURIER_GROUPS, dtype=F32)

    tm_s = _token_tile(s)
    tm_c = _token_tile(b * l)
    xl = x.reshape(b * s, d)
    xc = ctx.reshape(b * l, d)
    for li in range(depth):
        with_ctx = li < depth - 1
        final = li == depth - 1
        base = li * MOD_ROWS
        w_in_b = w_in[li].astype(BF16)
        ul = _inproj(xl, norm1_g[li], modv, base, s, w_in_b, tm_s)
        uc = _inproj(xc, norm1_g[li], modv, base + b, None, w_in_b, tm_c)
        ul = [a.reshape(b, s, -1) for a in ul]
        uc = [a.reshape(b, l, -1) for a in uc]

        w_bd = (eye_g[:, None, :, None] * fourier_w[li][:, :, None, :]).reshape(MIX, MIX)
        f_lat = _fourier(ul[0], tab_s, c64, s64, w_bd)
        h_lat, h_ctx = _hgrn(ul[1:6], uc[1:6], lower[li], norm_g4[li], with_ctx)
        n_lat = _na_latent(ul[6], ul[7], ul[8], uc[7], uc[8], _na_bias_table(na_rpb[li]))

        wo, wg = w_out[li].astype(BF16), w_ffn_gate[li].astype(BF16)
        wu, wd = w_ffn_up[li].astype(BF16), w_ffn_down[li].astype(BF16)
        if with_ctx:
            f_ctx = _fourier(uc[0], tab_l, c64, s64, w_bd)
            n_ctx = _na_context(uc[6], uc[7], uc[8])
            xc = _ffn(xc, f_ctx.reshape(b * l, -1), h_ctx.reshape(b * l, -1),
                      n_ctx.reshape(b * l, -1), norm2_g[li], final_norm_g, modv, base + b, None,
                      wo, wg, wu, wd, tm_c, False)
        xl = _ffn(xl, f_lat.reshape(b * s, -1), h_lat.reshape(b * s, -1), n_lat.reshape(b * s, -1),
                  norm2_g[li], final_norm_g, modv, base, s, wo, wg, wu, wd, tm_s, final)
    return xl.reshape(b, s, d)
```

```python
import functools

import jax
import jax.numpy as jnp
import numpy as np
from jax import lax
from jax.experimental import pallas as pl
from jax.experimental.pallas import tpu as pltpu

F32 = jnp.float32
BF16 = jnp.bfloat16

EPS = 1e-6
NEG_INF = -1e30
GRID_W = 64
FOURIER_GROUPS = 4
FOURIER_GD = 64
HGRN_DK = 64
HGRN_HEADS = 4
HGRN_CHUNK = 32
NA_HEAD_DIM = 64
NA_HEADS = 8
NA_KH = 8
NA_KW = 16
MIX = 256
NA_WIDTH = NA_HEADS * NA_HEAD_DIM
MOD_ROWS = 16
FF_CHUNK = 256
VMEM_LIMIT = 56 * 1024 * 1024


def _cparams(n_axes):
    return pltpu.CompilerParams(dimension_semantics=("arbitrary",) * n_axes,
                                vmem_limit_bytes=VMEM_LIMIT)


def _split2(a):
    hi = a.astype(BF16)
    lo = (a - hi.astype(F32)).astype(BF16)
    return hi, lo


def _dot(a, b):
    return jnp.dot(a, b, preferred_element_type=F32)


def _dot_nt(a, b):
    return lax.dot_general(a, b, (((1,), (1,)), ((), ())), preferred_element_type=F32)


def _dot_tn(a, b):
    return lax.dot_general(a, b, (((0,), (0,)), ((), ())), preferred_element_type=F32)


def _dot_hilo(a, b):
    ah, al = _split2(a)
    bh, bl = _split2(b)
    return _dot(ah, bh) + _dot(al, bh) + _dot(ah, bl)


def _sigmoid(z):
    return 1.0 / (1.0 + jnp.exp(-z))


def _const_spec(shape):
    nd = len(shape)
    return pl.BlockSpec(shape, lambda *_: (0,) * nd, pipeline_mode=pl.Buffered(1))


def _mod_kernel(c_ref, w_ref, b_ref, o_ref):
    c = c_ref[...]
    s = c * _sigmoid(c)
    o_ref[0] = _dot_hilo(s, w_ref[0]) + b_ref[0]


def _modulation(cc, w_mod, b_mod):
    depth, d, n = w_mod.shape
    tn = 1024
    return pl.pallas_call(
        _mod_kernel,
        grid=(depth, n // tn),
        in_specs=[pl.BlockSpec((MOD_ROWS, d), lambda l, j: (0, 0)),
                  pl.BlockSpec((1, d, tn), lambda l, j: (l, 0, j)),
                  pl.BlockSpec((1, 1, tn), lambda l, j: (l, 0, j))],
        out_specs=pl.BlockSpec((1, MOD_ROWS, tn), lambda l, j: (l, 0, j)),
        out_shape=jax.ShapeDtypeStruct((depth, MOD_ROWS, n), F32),
        compiler_params=_cparams(2),
        name="modulation",
    )(cc, w_mod, b_mod.reshape(depth, 1, n))


def _inproj_kernel(x_ref, g_ref, sh_ref, sc_ref, w_ref,
                   uf_ref, hq_ref, zf_ref, zb_ref, hi_ref, hg_ref, q_ref, k_ref, v_ref):
    x = x_ref[...]
    y = x * lax.rsqrt(jnp.mean(x * x, axis=-1, keepdims=True) + EPS) * g_ref[...]
    a = (y * (1.0 + sc_ref[0]) + sh_ref[0]).astype(BF16)
    col = 0
    for ref in (uf_ref, hq_ref, zf_ref, zb_ref, hi_ref, hg_ref):
        ref[...] = _dot(a, w_ref[:, col:col + MIX]).astype(ref.dtype)
        col += MIX
    scale = NA_HEAD_DIM ** -0.5
    q_ref[...] = (_dot(a, w_ref[:, col:col + NA_WIDTH]) * scale).astype(BF16)
    col += NA_WIDTH
    k_ref[...] = _dot(a, w_ref[:, col:col + NA_WIDTH]).astype(BF16)
    col += NA_WIDTH
    v_ref[...] = _dot(a, w_ref[:, col:col + NA_WIDTH]).astype(BF16)


def _inproj(x2d, norm_g, modv, mod_base, rows_per_mod, w_in_b, tm):
    n, d = x2d.shape
    in_w = w_in_b.shape[1]

    def mod_idx(j):
        if rows_per_mod is None:
            return lambda i: (mod_base * 6 + j, 0, 0)
        tpm = rows_per_mod // tm
        return lambda i: ((mod_base + i // tpm) * 6 + j, 0, 0)

    def tok(w):
        return pl.BlockSpec((tm, w), lambda i: (i, 0))

    outs = [(MIX, BF16), (MIX, BF16), (MIX, F32), (MIX, F32), (MIX, BF16), (MIX, BF16),
            (NA_WIDTH, BF16), (NA_WIDTH, BF16), (NA_WIDTH, BF16)]
    return pl.pallas_call(
        _inproj_kernel,
        grid=(n // tm,),
        in_specs=[tok(d),
                  pl.BlockSpec((1, d), lambda i: (0, 0)),
                  pl.BlockSpec((1, 1, d), mod_idx(0)),
                  pl.BlockSpec((1, 1, d), mod_idx(1)),
                  _const_spec((d, in_w))],
        out_specs=[tok(w) for w, _ in outs],
        out_shape=[jax.ShapeDtypeStruct((n, w), dt) for w, dt in outs],
        compiler_params=_cparams(1),
        name="inproj",
    )(x2d, norm_g.reshape(1, d), modv, modv, w_in_b)


def _dft_table(t_len):
    kb = min(64, t_len)
    assert t_len % kb == 0
    t = jnp.arange(t_len, dtype=jnp.int32)[None, :]
    k_hi = jnp.arange(t_len // kb, dtype=jnp.int32)[:, None] * kb
    k_lo = jnp.arange(kb, dtype=jnp.int32)[:, None]
    w = 2.0 * np.pi / t_len
    ang_hi = ((k_hi * t) % t_len).astype(F32) * w
    ang_lo = ((k_lo * t) % t_len).astype(F32) * w
    ch, sh = jnp.cos(ang_hi)[:, None, :], jnp.sin(ang_hi)[:, None, :]
    cl, sl = jnp.cos(ang_lo)[None, :, :], jnp.sin(ang_lo)[None, :, :]
    cos = (ch * cl - sh * sl).reshape(t_len, t_len)
    sin = (sh * cl + ch * sl).reshape(t_len, t_len)
    return jnp.concatenate([cos, sin], axis=1).astype(BF16)


def _channel_dft_consts():
    cd = np.outer(np.arange(FOURIER_GD), np.arange(FOURIER_GD)) % FOURIER_GD
    ang = 2.0 * np.pi * cd / FOURIER_GD
    eye = np.eye(FOURIER_GROUPS)
    return (jnp.asarray(np.kron(eye, np.cos(ang)), F32),
            jnp.asarray(np.kron(eye, np.sin(ang)), F32))


def _fourier_kernel(u_ref, tab_ref, c64_ref, s64_ref, w_ref, o_ref, pq_ref, *, t_len):
    @pl.when(pl.program_id(1) == 0)
    def _():
        norm = (FOURIER_GD * t_len) ** -0.5
        w = w_ref[...]
        a = (_dot_hilo(c64_ref[...], w) * norm).astype(BF16)
        b = (_dot_hilo(s64_ref[...], w) * (-norm)).astype(BF16)
        u = u_ref[0]
        pq_ref[0:t_len, :] = _dot(u, a).astype(BF16)
        pq_ref[t_len:2 * t_len, :] = _dot(u, b).astype(BF16)

    o_ref[0] = _dot(tab_ref[...], pq_ref[...]).astype(BF16)


def _fourier(u, tab, c64, s64, w_bd):
    b, t_len, _ = u.shape
    tm = min(512, t_len)
    return pl.pallas_call(
        functools.partial(_fourier_kernel, t_len=t_len),
        grid=(b, t_len // tm),
        in_specs=[pl.BlockSpec((1, t_len, MIX), lambda i, j: (i, 0, 0)),
                  pl.BlockSpec((tm, 2 * t_len), lambda i, j: (j, 0)),
                  _const_spec((MIX, MIX)), _const_spec((MIX, MIX)), _const_spec((MIX, MIX))],
        out_specs=pl.BlockSpec((1, tm, MIX), lambda i, j: (i, j, 0)),
        out_shape=jax.ShapeDtypeStruct((b, t_len, MIX), BF16),
        scratch_shapes=[pltpu.VMEM((2 * t_len, MIX), BF16)],
        compiler_params=_cparams(2),
        name="fourier",
    )(u, tab, c64, s64, w_bd)


def _hgrn_kernel(ql_ref, zfl_ref, zbl_ref, il_ref, gl_ref,
                 qc_ref, zfc_ref, zbc_ref, ic_ref, gc_ref, lb_ref, ng_ref,
                 *rest, n_lat, n_ctx, with_ctx):
    if with_ctx:
        ol_ref, oc_ref, st_ref, osl_ref, osc_ref = rest
    else:
        ol_ref, st_ref, osl_ref = rest
        oc_ref = osc_ref = None
    C = HGRN_CHUNK
    W = MIX
    r_i = lax.broadcasted_iota(jnp.int32, (C, C), 0)
    c_i = lax.broadcasted_iota(jnp.int32, (C, C), 1)
    tri = (c_i <= r_i, c_i >= r_i)
    tri_b16 = tuple(jnp.where(m, 1.0, 0.0).astype(BF16) for m in tri)
    r4 = lax.broadcasted_iota(jnp.int32, (HGRN_HEADS * C, C), 0) % C
    c4 = lax.broadcasted_iota(jnp.int32, (HGRN_HEADS * C, C), 1)
    tri4 = (c4 <= r4, c4 >= r4)
    lane_head = lax.broadcasted_iota(jnp.int32, (1, W), 1) // HGRN_DK
    head_mask = [lane_head == h for h in range(HGRN_HEADS)]
    bd = (lax.broadcasted_iota(jnp.int32, (W, W), 0) // HGRN_DK
          == lax.broadcasted_iota(jnp.int32, (W, W), 1) // HGRN_DK)
    bd_ones = jnp.where(bd, 1.0, 0.0).astype(BF16)

    def step(q_ref, z_ref, i_ref, n, d, need_o):
        rows = pl.ds(pl.multiple_of(n * C, C), C)
        z = z_ref[0, rows, :]
        v = i_ref[0, rows, :]
        lb = lb_ref[d:d + 1, :]
        f = lb + (1.0 - lb) * _sigmoid(z)
        kf = 1.0 - f
        lf = jnp.log(f)
        h1 = lf.astype(BF16)
        r1 = lf - h1.astype(F32)
        h2 = r1.astype(BF16)
        h3 = (r1 - h2.astype(F32)).astype(BF16)
        a = _dot(tri_b16[d], h1) + _dot(tri_b16[d], h2) + _dot(tri_b16[d], h3)
        kd = kf * jnp.exp(-a)
        eal = jnp.exp(a[C - 1:C, :] if d == 0 else a[0:1, :])
        ksb = (kd * eal).astype(BF16)
        st = st_ref[d]
        o = None
        if need_o:
            q = q_ref[0, rows, :].astype(F32)
            qdb = (q * jnp.exp(a)).astype(BF16)
            kdb = kd.astype(BF16)
            qs = jnp.concatenate([jnp.where(m, qdb, jnp.zeros_like(qdb)) for m in head_mask], axis=0)
            sc = jnp.where(tri4[d], _dot_nt(qs, kdb), 0.0).astype(BF16)
            oa = _dot(sc, v) + _dot_nt(qs, st.astype(BF16))
            o = jnp.where(head_mask[0], oa[0:C, :], 0.0)
            for h in range(1, HGRN_HEADS):
                o = o + jnp.where(head_mask[h], oa[h * C:(h + 1) * C, :], 0.0)
        st_ref[d] = st * eal + _dot_tn(v, ksb)
        return o

    def readout(o, g_ref, n):
        rows = pl.ds(pl.multiple_of(n * C, C), C)
        hi, lo = _split2(o * o)
        ms = (_dot(hi, bd_ones) + _dot(lo, bd_ones)) * (1.0 / HGRN_DK)
        y = o * lax.rsqrt(ms + EPS) * ng_ref[...]
        g = g_ref[0, rows, :].astype(F32)
        return (y * (g * _sigmoid(g))).astype(BF16)

    def scan(q_ref, zf_ref, zb_ref, i_ref, g_ref, os_ref, o_ref, n_chunks):
        half = n_chunks // 2
        emit = o_ref is not None

        def first(n, carry):
            nb = n_chunks - 1 - n
            of = step(q_ref, zf_ref, i_ref, n, 0, emit)
            ob = step(q_ref, zb_ref, i_ref, nb, 1, emit)
            if emit:
                os_ref[pl.ds(pl.multiple_of(n * C, C), C), :] = of
                os_ref[pl.ds(pl.multiple_of(nb * C, C), C), :] = ob
            return carry

        def second(n, carry):
            nb = n_chunks - 1 - n
            of = step(q_ref, zf_ref, i_ref, n, 0, emit)
            ob = step(q_ref, zb_ref, i_ref, nb, 1, emit)
            if emit:
                rf = pl.ds(pl.multiple_of(n * C, C), C)
                rb = pl.ds(pl.multiple_of(nb * C, C), C)
                o_ref[0, rf, :] = readout(os_ref[rf, :] + of, g_ref, n)
                o_ref[0, rb, :] = readout(os_ref[rb, :] + ob, g_ref, nb)
            return carry

        lax.fori_loop(0, half, first, 0, unroll=2)
        lax.fori_loop(half, n_chunks, second, 0, unroll=2)

    st_ref[...] = jnp.zeros_like(st_ref)
    scan(qc_ref, zfc_ref, zbc_ref, ic_ref, gc_ref, osc_ref, oc_ref, n_ctx)
    scan(ql_ref, zfl_ref, zbl_ref, il_ref, gl_ref, osl_ref, ol_ref, n_lat)


def _hgrn(lat, ctx, lb, norm_g4, with_ctx):
    b, s, _ = lat[0].shape
    l = ctx[0].shape[1]
    assert s % (2 * HGRN_CHUNK) == 0 and l % (2 * HGRN_CHUNK) == 0

    def seq(t):
        return pl.BlockSpec((1, t, MIX), lambda i: (i, 0, 0), pipeline_mode=pl.Buffered(1))

    out_specs = [pl.BlockSpec((1, s, MIX), lambda i: (i, 0, 0))]
    out_shape = [jax.ShapeDtypeStruct((b, s, MIX), BF16)]
    scratch = [pltpu.VMEM((2, MIX, MIX), F32), pltpu.VMEM((s, MIX), F32)]
    if with_ctx:
        out_specs.append(pl.BlockSpec((1, l, MIX), lambda i: (i, 0, 0)))
        out_shape.append(jax.ShapeDtypeStruct((b, l, MIX), BF16))
        scratch.append(pltpu.VMEM((l, MIX), F32))
    res = pl.pallas_call(
        functools.partial(_hgrn_kernel, n_lat=s // HGRN_CHUNK, n_ctx=l // HGRN_CHUNK,
                          with_ctx=with_ctx),
        grid=(b,),
        in_specs=[seq(s)] * 5 + [seq(l)] * 5 + [_const_spec((2, MIX)), _const_spec((1, MIX))],
        out_specs=out_specs,
        out_shape=out_shape,
        scratch_shapes=scratch,
        compiler_params=_cparams(1),
        name="hgrn",
    )(*lat, *ctx, lb, norm_g4)
    return (res[0], res[1]) if with_ctx else (res[0], None)


def _na_bias_table(rpb):
    n_h = rpb.shape[0]
    w, nr = GRID_W, 2 * NA_KH - 1
    side = (w - 1) - (NA_KW - 1)
    p = jnp.pad(rpb.astype(F32), ((0, 0), (0, 0), (side, side + 1)))
    skew = jnp.broadcast_to(p[:, :, None, :], (n_h, nr, w, 2 * w)).reshape(n_h, nr, 2 * w * w)
    skew = skew[:, :, :w * (2 * w - 1)].reshape(n_h, nr, w, 2 * w - 1)
    toep = skew[..., w - 1:]
    qc = np.arange(w)[:, None]
    kc = np.arange(w)[None, :]
    win = np.clip(qc - NA_KW // 2, 0, w - NA_KW)
    valid = (kc >= win) & (kc < win + NA_KW)
    toep = jnp.where(valid[None, None], toep, NEG_INF)
    tab = jnp.stack([toep[:, NA_KH - 1 - di:2 * NA_KH - 1 - di] for di in range(NA_KH)], axis=1)
    tab = tab.reshape(n_h // 2, 2, NA_KH, NA_KH, w, w)
    tab = tab.transpose(0, 2, 1, 4, 3, 5)
    return tab.reshape(n_h // 2, NA_KH, 2 * w, NA_KH * w)


def _softmax_pv(s_list, v_list):
    m = functools.reduce(jnp.maximum, [jnp.max(s, axis=-1, keepdims=True) for s in s_list])
    p_list = [jnp.exp(s - m) for s in s_list]
    l = functools.reduce(jnp.add, [jnp.sum(p, axis=-1, keepdims=True) for p in p_list])
    o = functools.reduce(jnp.add, [_dot(p.astype(BF16), v) for p, v in zip(p_list, v_list)])
    return o * (1.0 / l)


NA_ROWS_PER_STEP = 4


def _na_kernel(q_ref, k_ref, v_ref, kc_ref, vc_ref, bias_ref, o_ref, *, rows):
    lane = lax.broadcasted_iota(jnp.int32, (1, 2 * NA_HEAD_DIM), 1)
    first = lane < NA_HEAD_DIM
    kc = kc_ref[0]
    vc = vc_ref[0]
    win = NA_KH * GRID_W
    nq = 2 * GRID_W

    def body(g, carry):
        qs, kws, vws, dis, qrows = [], [], [], [], []
        for j in range(NA_ROWS_PER_STEP):
            r = g * NA_ROWS_PER_STEP + j
            rs = jnp.clip(r - NA_KH // 2, 0, rows - NA_KH)
            dis.append(r - rs)
            qrows.append(pl.ds(pl.multiple_of(r * GRID_W, GRID_W), GRID_W))
            krows = pl.ds(pl.multiple_of(rs * GRID_W, GRID_W), win)
            q = q_ref[0, qrows[-1], :]
            qs += [jnp.where(first, q, jnp.zeros_like(q)), jnp.where(first, jnp.zeros_like(q), q)]
            kws.append(k_ref[0, krows, :])
            vws.append(v_ref[0, krows, :])
        qs = jnp.concatenate(qs, axis=0)
        sx = _dot_nt(qs, kc)
        pws, pxs, inv_ls = [], [], []
        for j in range(NA_ROWS_PER_STEP):
            sw = _dot_nt(qs[j * nq:(j + 1) * nq], kws[j]) + bias_ref[0, dis[j]]
            sxj = sx[j * nq:(j + 1) * nq]
            m = jnp.maximum(jnp.max(sw, axis=-1, keepdims=True), jnp.max(sxj, axis=-1, keepdims=True))
            pw = jnp.exp(sw - m)
            px = jnp.exp(sxj - m)
            inv_ls.append(1.0 / (jnp.sum(pw, axis=-1, keepdims=True) + jnp.sum(px, axis=-1, keepdims=True)))
            pws.append(pw.astype(BF16))
            pxs.append(px.astype(BF16))
        oc = _dot(jnp.concatenate(pxs, axis=0), vc)
        for j in range(NA_ROWS_PER_STEP):
            o = (_dot(pws[j], vws[j]) + oc[j * nq:(j + 1) * nq]) * inv_ls[j]
            o_ref[0, qrows[j], :] = jnp.where(first, o[:GRID_W], o[GRID_W:]).astype(BF16)
        return carry

    lax.fori_loop(0, rows // NA_ROWS_PER_STEP, body, 0)


def _na_latent(q, k, v, kc, vc, bias):
    b, s, _ = q.shape
    l = kc.shape[1]
    rows = s // GRID_W
    assert s % GRID_W == 0 and rows >= NA_KH and rows % NA_ROWS_PER_STEP == 0
    hp = 2 * NA_HEAD_DIM

    def seq(t):
        return pl.BlockSpec((1, t, hp), lambda i, j: (i, 0, j))

    return pl.pallas_call(
        functools.partial(_na_kernel, rows=rows),
        grid=(b, NA_HEADS // 2),
        in_specs=[seq(s), seq(s), seq(s), seq(l), seq(l),
                  pl.BlockSpec((1, NA_KH, 2 * GRID_W, NA_KH * GRID_W), lambda i, j: (j, 0, 0, 0))],
        out_specs=seq(s),
        out_shape=jax.ShapeDtypeStruct((b, s, NA_WIDTH), BF16),
        compiler_params=_cparams(2),
        name="na_latent",
    )(q, k, v, kc, vc, bias)


def _na_ctx_kernel(q_ref, k_ref, v_ref, o_ref):
    lane = lax.broadcasted_iota(jnp.int32, (1, 2 * NA_HEAD_DIM), 1)
    first = lane < NA_HEAD_DIM
    q, k, v = q_ref[0], k_ref[0], v_ref[0]
    outs = []
    for h in range(2):
        qh = jnp.where(first if h == 0 else ~first, q, jnp.zeros_like(q))
        outs.append(_softmax_pv([_dot_nt(qh, k)], [v]))
    o_ref[0] = jnp.where(first, outs[0], outs[1]).astype(BF16)


def _na_context(q, k, v):
    b, l, _ = q.shape
    spec = pl.BlockSpec((1, l, 2 * NA_HEAD_DIM), lambda i, j: (i, 0, j))
    return pl.pallas_call(
        _na_ctx_kernel,
        grid=(b, NA_HEADS // 2),
        in_specs=[spec, spec, spec],
        out_specs=spec,
        out_shape=jax.ShapeDtypeStruct((b, l, NA_WIDTH), BF16),
        compiler_params=_cparams(2),
        name="na_context",
    )(q, k, v)


def _ffn_kernel(x_ref, f_ref, h_ref, n_ref, g1_ref, sh_ref, sc_ref, g2_ref, ng_ref, fg_ref,
                wo_ref, wg_ref, wu_ref, wd_ref, o_ref, *, final):
    y = (_dot(f_ref[...], wo_ref[0:MIX, :]) + _dot(h_ref[...], wo_ref[MIX:2 * MIX, :])
         + _dot(n_ref[...], wo_ref[2 * MIX:, :]))
    x1 = x_ref[...] + g1_ref[0] * y
    hn = x1 * lax.rsqrt(jnp.mean(x1 * x1, axis=-1, keepdims=True) + EPS) * ng_ref[...]
    hm = (hn * (1.0 + sc_ref[0]) + sh_ref[0]).astype(BF16)
    d_ff = wg_ref.shape[1]
    acc = jnp.zeros(x1.shape, F32)
    for j in range(d_ff // FF_CHUNK):
        cols = slice(j * FF_CHUNK, (j + 1) * FF_CHUNK)
        g = _dot(hm, wg_ref[:, cols])
        u = _dot(hm, wu_ref[:, cols])
        act = (g * _sigmoid(g) * u).astype(BF16)
        acc = acc + _dot(act, wd_ref[cols, :])
    x2 = x1 + g2_ref[0] * acc
    if final:
        x2 = x2 * lax.rsqrt(jnp.mean(x2 * x2, axis=-1, keepdims=True) + EPS) * fg_ref[...]
    o_ref[...] = x2


def _ffn(x2d, f, h, n, norm_g, final_g, modv, mod_base, rows_per_mod, wo, wg, wu, wd, tm, final):
    nt, d = x2d.shape
    d_ff = wg.shape[1]
    assert d_ff % FF_CHUNK == 0

    def mod_idx(j):
        if rows_per_mod is None:
            return lambda i: (mod_base * 6 + j, 0, 0)
        tpm = rows_per_mod // tm
        return lambda i: ((mod_base + i // tpm) * 6 + j, 0, 0)

    def tok(w):
        return pl.BlockSpec((tm, w), lambda i: (i, 0))

    def vec():
        return pl.BlockSpec((1, d), lambda i: (0, 0))

    return pl.pallas_call(
        functools.partial(_ffn_kernel, final=final),
        grid=(nt // tm,),
        in_specs=[tok(d), tok(MIX), tok(MIX), tok(NA_WIDTH),
                  pl.BlockSpec((1, 1, d), mod_idx(2)), pl.BlockSpec((1, 1, d), mod_idx(3)),
                  pl.BlockSpec((1, 1, d), mod_idx(4)), pl.BlockSpec((1, 1, d), mod_idx(5)),
                  vec(), vec(),
                  _const_spec(wo.shape), _const_spec(wg.shape), _const_spec(wu.shape),
                  _const_spec(wd.shape)],
        out_specs=tok(d),
        out_shape=jax.ShapeDtypeStruct((nt, d), F32),
        compiler_params=_cparams(1),
        name="outproj_ffn",
    )(x2d, f, h, n, modv, modv, modv, modv, norm_g.reshape(1, d), final_g.reshape(1, d),
      wo, wg, wu, wd)


def _token_tile(n):
    for tm in (512, 256, 128, 64, 32, 16, 8):
        if n % tm == 0:
            return tm
    raise ValueError(f"token count {n} is not a multiple of 8")


def kernel(x, c, ctx, c_ctx, w_mod, b_mod, norm1_g, w_in, fourier_w, hgrn_lb, hgrn_norm_g, na_rpb,
           w_out, norm2_g, w_ffn_gate, w_ffn_up, w_ffn_down, final_norm_g):
    b, s, d = x.shape
    l = ctx.shape[1]
    depth = w_mod.shape[0]
    assert b < MOD_ROWS and d == 2 * MIX + NA_WIDTH

    cc = jnp.zeros((MOD_ROWS, d), F32).at[:b].set(c).at[b].set(c_ctx)
    mod = _modulation(cc, w_mod, b_mod)
    modv = mod.reshape(depth * MOD_ROWS * 6, 1, d)

    lbp = jax.nn.softmax(hgrn_lb.astype(F32), axis=0)
    lower = jnp.cumsum(lbp, axis=0) - lbp[0:1]
    norm_g4 = jnp.tile(hgrn_norm_g.astype(F32), (1, HGRN_HEADS))[:, None, :]

    tab_s, tab_l = _dft_table(s), _dft_table(l)
    c64, s64 = _channel_dft_consts()
    eye_g = jnp.eye(FOURIER_GROUPS, dtype=F32)

    tm_s = _token_tile(s)
    tm_c = _token_tile(b * l)
    xl = x.reshape(b * s, d)
    xc = ctx.reshape(b * l, d)
    for li in range(depth):
        with_ctx = li < depth - 1
        final = li == depth - 1
        base = li * MOD_ROWS
        w_in_b = w_in[li].astype(BF16)
        ul = _inproj(xl, norm1_g[li], modv, base, s, w_in_b, tm_s)
        uc = _inproj(xc, norm1_g[li], modv, base + b, None, w_in_b, tm_c)
        ul = [a.reshape(b, s, -1) for a in ul]
        uc = [a.reshape(b, l, -1) for a in uc]

        w_bd = (eye_g[:, None, :, None] * fourier_w[li][:, :, None, :]).reshape(MIX, MIX)
        f_lat = _fourier(ul[0], tab_s, c64, s64, w_bd)
        h_lat, h_ctx = _hgrn(ul[1:6], uc[1:6], lower[li], norm_g4[li], with_ctx)
        n_lat = _na_latent(ul[6], ul[7], ul[8], uc[7], uc[8], _na_bias_table(na_rpb[li]))

        wo, wg = w_out[li].astype(BF16), w_ffn_gate[li].astype(BF16)
        wu, wd = w_ffn_up[li].astype(BF16), w_ffn_down[li].astype(BF16)
        if with_ctx:
            f_ctx = _fourier(uc[0], tab_l, c64, s64, w_bd)
            n_ctx = _na_context(uc[6], uc[7], uc[8])
            xc = _ffn(xc, f_ctx.reshape(b * l, -1), h_ctx.reshape(b * l, -1),
                      n_ctx.reshape(b * l, -1), norm2_g[li], final_norm_g, modv, base + b, None,
                      wo, wg, wu, wd, tm_c, False)
        xl = _ffn(xl, f_lat.reshape(b * s, -1), h_lat.reshape(b * s, -1), n_lat.reshape(b * s, -1),
                  norm2_g[li], final_norm_g, modv, base, s, wo, wg, wu, wd, tm_s, final)
    return xl.reshape(b, s, d)
```

```python
import functools

import jax
import jax.numpy as jnp
import numpy as np
from jax import lax
from jax.experimental import pallas as pl
from jax.experimental.pallas import tpu as pltpu

F32 = jnp.float32
BF16 = jnp.bfloat16

EPS = 1e-6
NEG_INF = -1e30
GRID_W = 64
FOURIER_GROUPS = 4
FOURIER_GD = 64
HGRN_DK = 64
HGRN_HEADS = 4
HGRN_CHUNK = 32
NA_HEAD_DIM = 64
NA_HEADS = 8
NA_KH = 8
NA_KW = 16
MIX = 256
NA_WIDTH = NA_HEADS * NA_HEAD_DIM
MOD_ROWS = 16
FF_CHUNK = 256
VMEM_LIMIT = 56 * 1024 * 1024


def _cparams(n_axes):
    return pltpu.CompilerParams(dimension_semantics=("arbitrary",) * n_axes,
                                vmem_limit_bytes=VMEM_LIMIT)


def _split2(a):
    hi = a.astype(BF16)
    lo = (a - hi.astype(F32)).astype(BF16)
    return hi, lo


def _dot(a, b):
    return jnp.dot(a, b, preferred_element_type=F32)


def _dot_nt(a, b):
    return lax.dot_general(a, b, (((1,), (1,)), ((), ())), preferred_element_type=F32)


def _dot_tn(a, b):
    return lax.dot_general(a, b, (((0,), (0,)), ((), ())), preferred_element_type=F32)


def _dot_hilo(a, b):
    ah, al = _split2(a)
    bh, bl = _split2(b)
    return _dot(ah, bh) + _dot(al, bh) + _dot(ah, bl)


def _sigmoid(z):
    return 1.0 / (1.0 + jnp.exp(-z))


def _const_spec(shape):
    nd = len(shape)
    return pl.BlockSpec(shape, lambda *_: (0,) * nd, pipeline_mode=pl.Buffered(1))


def _mod_kernel(c_ref, w_ref, b_ref, o_ref):
    c = c_ref[...]
    s = c * _sigmoid(c)
    o_ref[0] = _dot_hilo(s, w_ref[0]) + b_ref[0]


def _modulation(cc, w_mod, b_mod):
    depth, d, n = w_mod.shape
    tn = 1024
    return pl.pallas_call(
        _mod_kernel,
        grid=(depth, n // tn),
        in_specs=[pl.BlockSpec((MOD_ROWS, d), lambda l, j: (0, 0)),
                  pl.BlockSpec((1, d, tn), lambda l, j: (l, 0, j)),
                  pl.BlockSpec((1, 1, tn), lambda l, j: (l, 0, j))],
        out_specs=pl.BlockSpec((1, MOD_ROWS, tn), lambda l, j: (l, 0, j)),
        out_shape=jax.ShapeDtypeStruct((depth, MOD_ROWS, n), F32),
        compiler_params=_cparams(2),
        name="modulation",
    )(cc, w_mod, b_mod.reshape(depth, 1, n))


def _inproj_kernel(x_ref, g_ref, sh_ref, sc_ref, w_ref,
                   uf_ref, hq_ref, zf_ref, zb_ref, hi_ref, hg_ref, q_ref, k_ref, v_ref):
    x = x_ref[...]
    y = x * lax.rsqrt(jnp.mean(x * x, axis=-1, keepdims=True) + EPS) * g_ref[...]
    a = (y * (1.0 + sc_ref[0]) + sh_ref[0]).astype(BF16)
    col = 0
    for ref in (uf_ref, hq_ref, zf_ref, zb_ref, hi_ref, hg_ref):
        ref[...] = _dot(a, w_ref[:, col:col + MIX]).astype(ref.dtype)
        col += MIX
    scale = NA_HEAD_DIM ** -0.5
    q_ref[...] = (_dot(a, w_ref[:, col:col + NA_WIDTH]) * scale).astype(BF16)
    col += NA_WIDTH
    k_ref[...] = _dot(a, w_ref[:, col:col + NA_WIDTH]).astype(BF16)
    col += NA_WIDTH
    v_ref[...] = _dot(a, w_ref[:, col:col + NA_WIDTH]).astype(BF16)


def _inproj(x2d, norm_g, modv, mod_base, rows_per_mod, w_in_b, tm):
    n, d = x2d.shape
    in_w = w_in_b.shape[1]

    def mod_idx(j):
        if rows_per_mod is None:
            return lambda i: (mod_base * 6 + j, 0, 0)
        tpm = rows_per_mod // tm
        return lambda i: ((mod_base + i // tpm) * 6 + j, 0, 0)

    def tok(w):
        return pl.BlockSpec((tm, w), lambda i: (i, 0))

    outs = [(MIX, BF16), (MIX, BF16), (MIX, F32), (MIX, F32), (MIX, BF16), (MIX, BF16),
            (NA_WIDTH, BF16), (NA_WIDTH, BF16), (NA_WIDTH, BF16)]
    return pl.pallas_call(
        _inproj_kernel,
        grid=(n // tm,),
        in_specs=[tok(d),
                  pl.BlockSpec((1, d), lambda i: (0, 0)),
                  pl.BlockSpec((1, 1, d), mod_idx(0)),
                  pl.BlockSpec((1, 1, d), mod_idx(1)),
                  _const_spec((d, in_w))],
        out_specs=[tok(w) for w, _ in outs],
        out_shape=[jax.ShapeDtypeStruct((n, w), dt) for w, dt in outs],
        compiler_params=_cparams(1),
        name="inproj",
    )(x2d, norm_g.reshape(1, d), modv, modv, w_in_b)


def _dft_table(t_len):
    kb = min(64, t_len)
    assert t_len % kb == 0
    t = jnp.arange(t_len // 2, dtype=jnp.int32)[None, :]
    k_hi = jnp.arange(t_len // kb, dtype=jnp.int32)[:, None] * kb
    k_lo = jnp.arange(kb, dtype=jnp.int32)[:, None]
    w = 2.0 * np.pi / t_len
    ang_hi = ((k_hi * t) % t_len).astype(F32) * w
    ang_lo = ((k_lo * t) % t_len).astype(F32) * w
    ch, sh = jnp.cos(ang_hi)[:, None, :], jnp.sin(ang_hi)[:, None, :]
    cl, sl = jnp.cos(ang_lo)[None, :, :], jnp.sin(ang_lo)[None, :, :]
    cos = (ch * cl - sh * sl).reshape(t_len, t_len // 2)
    sin = (sh * cl + ch * sl).reshape(t_len, t_len // 2)
    return jnp.concatenate([cos, sin], axis=1).astype(BF16)


def _channel_dft_consts():
    cd = np.outer(np.arange(FOURIER_GD), np.arange(FOURIER_GD)) % FOURIER_GD
    ang = 2.0 * np.pi * cd / FOURIER_GD
    eye = np.eye(FOURIER_GROUPS)
    return (jnp.asarray(np.kron(eye, np.cos(ang)), F32),
            jnp.asarray(np.kron(eye, np.sin(ang)), F32))


def _fourier_kernel(u_ref, r_ref, um_ref, tab_ref, c64_ref, s64_ref, w_ref, o_ref, pq_ref, pm_ref,
                    *, t_len, tm):
    half = t_len // 2

    @pl.when(pl.program_id(1) == 0)
    def _():
        norm = (FOURIER_GD * t_len) ** -0.5
        w = w_ref[...]
        a = (_dot_hilo(c64_ref[...], w) * norm).astype(BF16)
        b = (_dot_hilo(s64_ref[...], w) * (-norm)).astype(BF16)
        u = u_ref[0].astype(F32)
        r = r_ref[0].astype(F32)
        row0 = lax.broadcasted_iota(jnp.int32, (half, 1), 0) == 0
        pq_ref[0:half, :] = _dot(jnp.where(row0, u, u + r).astype(BF16), a).astype(BF16)
        pq_ref[half:t_len, :] = _dot((u - r).astype(BF16), b).astype(BF16)
        pm_ref[...] = _dot(um_ref[0], a)

    k = pl.program_id(1) * tm + lax.broadcasted_iota(jnp.int32, (tm, 1), 0)
    sign = (1 - 2 * (k & 1)).astype(F32)
    o_ref[0] = (_dot(tab_ref[...], pq_ref[...]) + sign * pm_ref[0:1, :]).astype(BF16)


def _fourier(u, tab, c64, s64, w_bd):
    b, t_len, _ = u.shape
    half = t_len // 2
    assert t_len % 16 == 0
    tm = min(512, t_len)
    refl = jnp.concatenate([u[:, :1], jnp.flip(u[:, half + 1:], axis=1)], axis=1)
    mid = u[:, half:half + 8]
    return pl.pallas_call(
        functools.partial(_fourier_kernel, t_len=t_len, tm=tm),
        grid=(b, t_len // tm),
        in_specs=[pl.BlockSpec((1, half, MIX), lambda i, j: (i, 0, 0)),
                  pl.BlockSpec((1, half, MIX), lambda i, j: (i, 0, 0)),
                  pl.BlockSpec((1, 8, MIX), lambda i, j: (i, 0, 0)),
                  pl.BlockSpec((tm, t_len), lambda i, j: (j, 0)),
                  _const_spec((MIX, MIX)), _const_spec((MIX, MIX)), _const_spec((MIX, MIX))],
        out_specs=pl.BlockSpec((1, tm, MIX), lambda i, j: (i, j, 0)),
        out_shape=jax.ShapeDtypeStruct((b, t_len, MIX), BF16),
        scratch_shapes=[pltpu.VMEM((t_len, MIX), BF16), pltpu.VMEM((8, MIX), F32)],
        compiler_params=_cparams(2),
        name="fourier",
    )(u, refl, mid, tab, c64, s64, w_bd)


def _hgrn_kernel(ql_ref, zfl_ref, zbl_ref, il_ref, gl_ref,
                 qc_ref, zfc_ref, zbc_ref, ic_ref, gc_ref, lb_ref, ng_ref,
                 *rest, n_lat, n_ctx, with_ctx):
    if with_ctx:
        ol_ref, oc_ref, st_ref, qd_ref, ks_ref, eal_ref, os_ref = rest
    else:
        ol_ref, st_ref, qd_ref, ks_ref, eal_ref, os_ref = rest
        oc_ref = None
    C = HGRN_CHUNK
    W = MIX
    lane_head = lax.broadcasted_iota(jnp.int32, (1, W), 1) // HGRN_DK
    head_mask = [lane_head == h for h in range(HGRN_HEADS)]
    bd = (lax.broadcasted_iota(jnp.int32, (W, W), 0) // HGRN_DK
          == lax.broadcasted_iota(jnp.int32, (W, W), 1) // HGRN_DK)
    bd_ones = jnp.where(bd, 1.0, 0.0).astype(BF16)

    def chunk_rows(n):
        return pl.ds(pl.multiple_of(n * C, C), C)

    def stack_heads(x):
        return jnp.concatenate([jnp.where(m, x, jnp.zeros_like(x)) for m in head_mask], axis=0)

    def pick_heads(oa):
        t = oa.shape[0] // HGRN_HEADS
        o = jnp.where(head_mask[0], oa[0:t, :], 0.0)
        for h in range(1, HGRN_HEADS):
            o = o + jnp.where(head_mask[h], oa[h * t:(h + 1) * t, :], 0.0)
        return o

    def group_chunks(n_chunks):
        return 4 if n_chunks % 4 == 0 else 2

    def phase_a(q_ref, zf_ref, zb_ref, i_ref, n_chunks, need_o):
        gc = group_chunks(n_chunks)
        gt = gc * C
        t_i = lax.broadcasted_iota(jnp.int32, (gt, gt), 0)
        s_i = lax.broadcasted_iota(jnp.int32, (gt, gt), 1)
        same = (t_i // C) == (s_i // C)
        allow = (same & (s_i <= t_i), same & (s_i >= t_i))
        allow_b16 = tuple(jnp.where(m, 1.0, 0.0).astype(BF16) for m in allow)
        t4 = lax.broadcasted_iota(jnp.int32, (HGRN_HEADS * gt, gt), 0) % gt
        s4 = lax.broadcasted_iota(jnp.int32, (HGRN_HEADS * gt, gt), 1)
        same4 = (t4 // C) == (s4 // C)
        allow4 = (same4 & (s4 <= t4), same4 & (s4 >= t4))

        def body(g, carry):
            rows = pl.ds(pl.multiple_of(g * gt, gt), gt)
            v = i_ref[0, rows, :]
            fs, parts = [], []
            for d, z_ref in ((0, zf_ref), (1, zb_ref)):
                lb = lb_ref[d:d + 1, :]
                f = lb + (1.0 - lb) * _sigmoid(z_ref[0, rows, :])
                lf = jnp.log(f)
                h1 = lf.astype(BF16)
                r1 = lf - h1.astype(F32)
                h2 = r1.astype(BF16)
                h3 = (r1 - h2.astype(F32)).astype(BF16)
                fs.append(f)
                parts.append((h1, h2, h3))
            a_s = [_dot(allow_b16[d], parts[d][0]) + _dot(allow_b16[d], parts[d][1])
                   + _dot(allow_b16[d], parts[d][2]) for d in (0, 1)]
            scs = []
            for d in (0, 1):
                a = a_s[d]
                kd = (1.0 - fs[d]) * jnp.exp(-a)
                ends = [a[c * C + C - 1:c * C + C, :] if d == 0 else a[c * C:c * C + 1, :]
                        for c in range(gc)]
                eal = [jnp.exp(e) for e in ends]
                ks_ref[d, rows, :] = (kd * jnp.concatenate(
                    [jnp.broadcast_to(e, (C, W)) for e in eal], axis=0)).astype(BF16)
                eal_ref[d, pl.ds(pl.multiple_of(g * gc * 8, gc * 8), gc * 8), :] = jnp.concatenate(
                    [jnp.broadcast_to(e, (8, W)) for e in eal], axis=0)
                if need_o:
                    qdb = (q_ref[0, rows, :].astype(F32) * jnp.exp(a)).astype(BF16)
                    qd_ref[d, rows, :] = qdb
                    scs.append(jnp.where(allow4[d], _dot_nt(stack_heads(qdb), kd.astype(BF16)),
                                         0.0).astype(BF16))
            if need_o:
                os_ref[rows, :] = pick_heads(_dot(scs[0], v)) + pick_heads(_dot(scs[1], v))
            return carry

        lax.fori_loop(0, n_chunks // gc, body, 0)

    def phase_b(i_ref, n_chunks, need_o):
        gc = group_chunks(n_chunks)

        def body(g, carry):
            idx = [[g * gc + c, n_chunks - 1 - (g * gc + c)] for c in range(gc)]
            upd = [[_dot_tn(i_ref[0, chunk_rows(m), :], ks_ref[d, chunk_rows(m), :])
                    for d, m in enumerate(pair)] for pair in idx]
            for c in range(gc):
                for d in (0, 1):
                    m = idx[c][d]
                    rows = chunk_rows(m)
                    st = st_ref[d]
                    if need_o:
                        oa = _dot_nt(stack_heads(qd_ref[d, rows, :]), st.astype(BF16))
                        os_ref[rows, :] = os_ref[rows, :] + pick_heads(oa)
                    eal = eal_ref[d, pl.ds(pl.multiple_of(m * 8, 8), 8), :][0:1, :]
                    st_ref[d] = st * eal + upd[c][d]
            return carry

        lax.fori_loop(0, n_chunks // gc, body, 0)

    def phase_c(g_ref, o_ref, n_rows):
        rt = next(t for t in (256, 128, 64) if n_rows % t == 0)

        def body(t, carry):
            rows = pl.ds(pl.multiple_of(t * rt, rt), rt)
            o = os_ref[rows, :]
            hi, lo = _split2(o * o)
            ms = (_dot(hi, bd_ones) + _dot(lo, bd_ones)) * (1.0 / HGRN_DK)
            y = o * lax.rsqrt(ms + EPS) * ng_ref[...]
            g = g_ref[0, rows, :].astype(F32)
            o_ref[0, rows, :] = (y * (g * _sigmoid(g))).astype(BF16)
            return carry

        lax.fori_loop(0, n_rows // rt, body, 0)

    st_ref[...] = jnp.zeros_like(st_ref)
    phase_a(qc_ref, zfc_ref, zbc_ref, ic_ref, n_ctx, with_ctx)
    phase_b(ic_ref, n_ctx, with_ctx)
    if with_ctx:
        phase_c(gc_ref, oc_ref, n_ctx * C)
    phase_a(ql_ref, zfl_ref, zbl_ref, il_ref, n_lat, True)
    phase_b(il_ref, n_lat, True)
    phase_c(gl_ref, ol_ref, n_lat * C)


def _hgrn(lat, ctx, lb, norm_g4, with_ctx):
    b, s, _ = lat[0].shape
    l = ctx[0].shape[1]
    assert s % (2 * HGRN_CHUNK) == 0 and l % (2 * HGRN_CHUNK) == 0

    def seq(t):
        return pl.BlockSpec((1, t, MIX), lambda i: (i, 0, 0), pipeline_mode=pl.Buffered(1))

    out_specs = [pl.BlockSpec((1, s, MIX), lambda i: (i, 0, 0))]
    out_shape = [jax.ShapeDtypeStruct((b, s, MIX), BF16)]
    t_max = max(s, l)
    scratch = [pltpu.VMEM((2, MIX, MIX), F32),
               pltpu.VMEM((2, t_max, MIX), BF16),
               pltpu.VMEM((2, t_max, MIX), BF16),
               pltpu.VMEM((2, t_max // HGRN_CHUNK * 8, MIX), F32),
               pltpu.VMEM((t_max, MIX), F32)]
    if with_ctx:
        out_specs.append(pl.BlockSpec((1, l, MIX), lambda i: (i, 0, 0)))
        out_shape.append(jax.ShapeDtypeStruct((b, l, MIX), BF16))
    res = pl.pallas_call(
        functools.partial(_hgrn_kernel, n_lat=s // HGRN_CHUNK, n_ctx=l // HGRN_CHUNK,
                          with_ctx=with_ctx),
        grid=(b,),
        in_specs=[seq(s)] * 5 + [seq(l)] * 5 + [_const_spec((2, MIX)), _const_spec((1, MIX))],
        out_specs=out_specs,
        out_shape=out_shape,
        scratch_shapes=scratch,
        compiler_params=_cparams(1),
        name="hgrn",
    )(*lat, *ctx, lb, norm_g4)
    return (res[0], res[1]) if with_ctx else (res[0], None)


def _na_bias_table(rpb):
    n_h = rpb.shape[0]
    w, nr = GRID_W, 2 * NA_KH - 1
    side = (w - 1) - (NA_KW - 1)
    p = jnp.pad(rpb.astype(F32), ((0, 0), (0, 0), (side, side + 1)))
    skew = jnp.broadcast_to(p[:, :, None, :], (n_h, nr, w, 2 * w)).reshape(n_h, nr, 2 * w * w)
    skew = skew[:, :, :w * (2 * w - 1)].reshape(n_h, nr, w, 2 * w - 1)
    toep = skew[..., w - 1:]
    qc = np.arange(w)[:, None]
    kc = np.arange(w)[None, :]
    win = np.clip(qc - NA_KW // 2, 0, w - NA_KW)
    valid = (kc >= win) & (kc < win + NA_KW)
    toep = jnp.where(valid[None, None], toep, NEG_INF)
    tab = jnp.stack([toep[:, NA_KH - 1 - di:2 * NA_KH - 1 - di] for di in range(NA_KH)], axis=1)
    tab = tab.reshape(n_h // 2, 2, NA_KH, NA_KH, w, w)
    tab = tab.transpose(0, 2, 1, 4, 3, 5)
    return tab.reshape(n_h // 2, NA_KH, 2 * w, NA_KH * w)


def _softmax_pv(s_list, v_list):
    m = functools.reduce(jnp.maximum, [jnp.max(s, axis=-1, keepdims=True) for s in s_list])
    p_list = [jnp.exp(s - m) for s in s_list]
    l = functools.reduce(jnp.add, [jnp.sum(p, axis=-1, keepdims=True) for p in p_list])
    o = functools.reduce(jnp.add, [_dot(p.astype(BF16), v) for p, v in zip(p_list, v_list)])
    return o * (1.0 / l)


NA_ROWS_PER_STEP = 8


def _na_kernel(q_ref, k_ref, v_ref, kc_ref, vc_ref, bias_ref, o_ref, *, rows):
    lane = lax.broadcasted_iota(jnp.int32, (1, 2 * NA_HEAD_DIM), 1)
    first = lane < NA_HEAD_DIM
    kc = kc_ref[0]
    vc = vc_ref[0]
    win = NA_KH * GRID_W
    nq = 2 * GRID_W

    def body(g, carry):
        qs, kws, vws, dis, qrows = [], [], [], [], []
        for j in range(NA_ROWS_PER_STEP):
            r = g * NA_ROWS_PER_STEP + j
            rs = jnp.clip(r - NA_KH // 2, 0, rows - NA_KH)
            dis.append(r - rs)
            qrows.append(pl.ds(pl.multiple_of(r * GRID_W, GRID_W), GRID_W))
            krows = pl.ds(pl.multiple_of(rs * GRID_W, GRID_W), win)
            q = q_ref[0, qrows[-1], :]
            qs += [jnp.where(first, q, jnp.zeros_like(q)), jnp.where(first, jnp.zeros_like(q), q)]
            kws.append(k_ref[0, krows, :])
            vws.append(v_ref[0, krows, :])
        qs = jnp.concatenate(qs, axis=0)
        sx = _dot_nt(qs, kc)
        pws, pxs, inv_ls = [], [], []
        for j in range(NA_ROWS_PER_STEP):
            sw = _dot_nt(qs[j * nq:(j + 1) * nq], kws[j]) + bias_ref[0, dis[j]]
            sxj = sx[j * nq:(j + 1) * nq]
            m = jnp.maximum(jnp.max(sw, axis=-1, keepdims=True), jnp.max(sxj, axis=-1, keepdims=True))
            pw = jnp.exp(sw - m)
            px = jnp.exp(sxj - m)
            inv_ls.append(1.0 / (jnp.sum(pw, axis=-1, keepdims=True) + jnp.sum(px, axis=-1, keepdims=True)))
            pws.append(pw.astype(BF16))
            pxs.append(px.astype(BF16))
        oc = _dot(jnp.concatenate(pxs, axis=0), vc)
        for j in range(NA_ROWS_PER_STEP):
            o = (_dot(pws[j], vws[j]) + oc[j * nq:(j + 1) * nq]) * inv_ls[j]
            o_ref[0, qrows[j], :] = jnp.where(first, o[:GRID_W], o[GRID_W:]).astype(BF16)
        return carry

    lax.fori_loop(0, rows // NA_ROWS_PER_STEP, body, 0)


def _na_latent(q, k, v, kc, vc, bias):
    b, s, _ = q.shape
    l = kc.shape[1]
    rows = s // GRID_W
    assert s % GRID_W == 0 and rows >= NA_KH and rows % NA_ROWS_PER_STEP == 0
    hp = 2 * NA_HEAD_DIM

    def seq(t):
        return pl.BlockSpec((1, t, hp), lambda i, j: (i, 0, j))

    return pl.pallas_call(
        functools.partial(_na_kernel, rows=rows),
        grid=(b, NA_HEADS // 2),
        in_specs=[seq(s), seq(s), seq(s), seq(l), seq(l),
                  pl.BlockSpec((1, NA_KH, 2 * GRID_W, NA_KH * GRID_W), lambda i, j: (j, 0, 0, 0))],
        out_specs=seq(s),
        out_shape=jax.ShapeDtypeStruct((b, s, NA_WIDTH), BF16),
        compiler_params=_cparams(2),
        name="na_latent",
    )(q, k, v, kc, vc, bias)


def _na_ctx_kernel(q_ref, k_ref, v_ref, o_ref):
    lane = lax.broadcasted_iota(jnp.int32, (1, 2 * NA_HEAD_DIM), 1)
    first = lane < NA_HEAD_DIM
    q, k, v = q_ref[0], k_ref[0], v_ref[0]
    outs = []
    for h in range(2):
        qh = jnp.where(first if h == 0 else ~first, q, jnp.zeros_like(q))
        outs.append(_softmax_pv([_dot_nt(qh, k)], [v]))
    o_ref[0] = jnp.where(first, outs[0], outs[1]).astype(BF16)


def _na_context(q, k, v):
    b, l, _ = q.shape
    spec = pl.BlockSpec((1, l, 2 * NA_HEAD_DIM), lambda i, j: (i, 0, j))
    return pl.pallas_call(
        _na_ctx_kernel,
        grid=(b, NA_HEADS // 2),
        in_specs=[spec, spec, spec],
        out_specs=spec,
        out_shape=jax.ShapeDtypeStruct((b, l, NA_WIDTH), BF16),
        compiler_params=_cparams(2),
        name="na_context",
    )(q, k, v)


def _ffn_kernel(x_ref, f_ref, h_ref, n_ref, g1_ref, sh_ref, sc_ref, g2_ref, ng_ref, fg_ref,
                wo_ref, wg_ref, wu_ref, wd_ref, o_ref, *, final):
    y = (_dot(f_ref[...], wo_ref[0:MIX, :]) + _dot(h_ref[...], wo_ref[MIX:2 * MIX, :])
         + _dot(n_ref[...], wo_ref[2 * MIX:, :]))
    x1 = x_ref[...] + g1_ref[0] * y
    hn = x1 * lax.rsqrt(jnp.mean(x1 * x1, axis=-1, keepdims=True) + EPS) * ng_ref[...]
    hm = (hn * (1.0 + sc_ref[0]) + sh_ref[0]).astype(BF16)
    d_ff = wg_ref.shape[1]
    acc = jnp.zeros(x1.shape, F32)
    for j in range(d_ff // FF_CHUNK):
        cols = slice(j * FF_CHUNK, (j + 1) * FF_CHUNK)
        g = _dot(hm, wg_ref[:, cols])
        u = _dot(hm, wu_ref[:, cols])
        act = (g * _sigmoid(g) * u).astype(BF16)
        acc = acc + _dot(act, wd_ref[cols, :])
    x2 = x1 + g2_ref[0] * acc
    if final:
        x2 = x2 * lax.rsqrt(jnp.mean(x2 * x2, axis=-1, keepdims=True) + EPS) * fg_ref[...]
    o_ref[...] = x2


def _ffn(x2d, f, h, n, norm_g, final_g, modv, mod_base, rows_per_mod, wo, wg, wu, wd, tm, final):
    nt, d = x2d.shape
    d_ff = wg.shape[1]
    assert d_ff % FF_CHUNK == 0

    def mod_idx(j):
        if rows_per_mod is None:
            return lambda i: (mod_base * 6 + j, 0, 0)
        tpm = rows_per_mod // tm
        return lambda i: ((mod_base + i // tpm) * 6 + j, 0, 0)

    def tok(w):
        return pl.BlockSpec((tm, w), lambda i: (i, 0))

    def vec():
        return pl.BlockSpec((1, d), lambda i: (0, 0))

    return pl.pallas_call(
        functools.partial(_ffn_kernel, final=final),
        grid=(nt // tm,),
        in_specs=[tok(d), tok(MIX), tok(MIX), tok(NA_WIDTH),
                  pl.BlockSpec((1, 1, d), mod_idx(2)), pl.BlockSpec((1, 1, d), mod_idx(3)),
                  pl.BlockSpec((1, 1, d), mod_idx(4)), pl.BlockSpec((1, 1, d), mod_idx(5)),
                  vec(), vec(),
                  _const_spec(wo.shape), _const_spec(wg.shape), _const_spec(wu.shape),
                  _const_spec(wd.shape)],
        out_specs=tok(d),
        out_shape=jax.ShapeDtypeStruct((nt, d), F32),
        compiler_params=_cparams(1),
        name="outproj_ffn",
    )(x2d, f, h, n, modv, modv, modv, modv, norm_g.reshape(1, d), final_g.reshape(1, d),
      wo, wg, wu, wd)


def _token_tile(n):
    for tm in (512, 256, 128, 64, 32, 16, 8):
        if n % tm == 0:
            return tm
    raise ValueError(f"token count {n} is not a multiple of 8")


def kernel(x, c, ctx, c_ctx, w_mod, b_mod, norm1_g, w_in, fourier_w, hgrn_lb, hgrn_norm_g, na_rpb,
           w_out, norm2_g, w_ffn_gate, w_ffn_up, w_ffn_down, final_norm_g):
    b, s, d = x.shape
    l = ctx.shape[1]
    depth = w_mod.shape[0]
    assert b < MOD_ROWS and d == 2 * MIX + NA_WIDTH

    cc = jnp.zeros((MOD_ROWS, d), F32).at[:b].set(c).at[b].set(c_ctx)
    mod = _modulation(cc, w_mod, b_mod)
    modv = mod.reshape(depth * MOD_ROWS * 6, 1, d)

    lbp = jax.nn.softmax(hgrn_lb.astype(F32), axis=0)
    lower = jnp.cumsum(lbp, axis=0) - lbp[0:1]
    norm_g4 = jnp.tile(hgrn_norm_g.astype(F32), (1, HGRN_HEADS))[:, None, :]

    tab_s, tab_l = _dft_table(s), _dft_table(l)
    c64, s64 = _channel_dft_consts()
    eye_g = jnp.eye(FOURIER_GROUPS, dtype=F32)

    tm_s = _token_tile(s)
    tm_c = _token_tile(b * l)
    xl = x.reshape(b * s, d)
    xc = ctx.reshape(b * l, d)
    for li in range(depth):
        with_ctx = li < depth - 1
        final = li == depth - 1
        base = li * MOD_ROWS
        w_in_b = w_in[li].astype(BF16)
        ul = _inproj(xl, norm1_g[li], modv, base, s, w_in_b, tm_s)
        uc = _inproj(xc, norm1_g[li], modv, base + b, None, w_in_b, tm_c)
        ul = [a.reshape(b, s, -1) for a in ul]
        uc = [a.reshape(b, l, -1) for a in uc]

        w_bd = (eye_g[:, None, :, None] * fourier_w[li][:, :, None, :]).reshape(MIX, MIX)
        f_lat = _fourier(ul[0], tab_s, c64, s64, w_bd)
        h_lat, h_ctx = _hgrn(ul[1:6], uc[1:6], lower[li], norm_g4[li], with_ctx)
        n_lat = _na_latent(ul[6], ul[7], ul[8], uc[7], uc[8], _na_bias_table(na_rpb[li]))

        wo, wg = w_out[li].astype(BF16), w_ffn_gate[li].astype(BF16)
        wu, wd = w_ffn_up[li].astype(BF16), w_ffn_down[li].astype(BF16)
        if with_ctx:
            f_ctx = _fourier(uc[0], tab_l, c64, s64, w_bd)
            n_ctx = _na_context(uc[6], uc[7], uc[8])
            xc = _ffn(xc, f_ctx.reshape(b * l, -1), h_ctx.reshape(b * l, -1),
                      n_ctx.reshape(b * l, -1), norm2_g[li], final_norm_g, modv, base + b, None,
                      wo, wg, wu, wd, tm_c, False)
        xl = _ffn(xl, f_lat.reshape(b * s, -1), h_lat.reshape(b * s, -1), n_lat.reshape(b * s, -1),
                  norm2_g[li], final_norm_g, modv, base, s, wo, wg, wu, wd, tm_s, final)
    return xl.reshape(b, s, d)
```

```python
import functools

import jax
import jax.numpy as jnp
import numpy as np
from jax import lax
from jax.experimental import pallas as pl
from jax.experimental.pallas import tpu as pltpu

F32 = jnp.float32
BF16 = jnp.bfloat16

EPS = 1e-6
NEG_INF = -1e30
GRID_W = 64
FOURIER_GROUPS = 4
FOURIER_GD = 64
HGRN_DK = 64
HGRN_HEADS = 4
HGRN_CHUNK = 32
NA_HEAD_DIM = 64
NA_HEADS = 8
NA_KH = 8
NA_KW = 16
MIX = 256
NA_WIDTH = NA_HEADS * NA_HEAD_DIM
MOD_ROWS = 16
FF_CHUNK = 256
VMEM_LIMIT = 56 * 1024 * 1024


def _cparams(n_axes):
    return pltpu.CompilerParams(dimension_semantics=("arbitrary",) * n_axes,
                                vmem_limit_bytes=VMEM_LIMIT)


def _split2(a):
    hi = a.astype(BF16)
    lo = (a - hi.astype(F32)).astype(BF16)
    return hi, lo


def _dot(a, b):
    return jnp.dot(a, b, preferred_element_type=F32)


def _dot_nt(a, b):
    return lax.dot_general(a, b, (((1,), (1,)), ((), ())), preferred_element_type=F32)


def _dot_tn(a, b):
    return lax.dot_general(a, b, (((0,), (0,)), ((), ())), preferred_element_type=F32)


def _dot_hilo(a, b):
    ah, al = _split2(a)
    bh, bl = _split2(b)
    return _dot(ah, bh) + _dot(al, bh) + _dot(ah, bl)


def _sigmoid(z):
    return 1.0 / (1.0 + jnp.exp(-z))


def _const_spec(shape):
    nd = len(shape)
    return pl.BlockSpec(shape, lambda *_: (0,) * nd, pipeline_mode=pl.Buffered(1))


def _mod_kernel(c_ref, w_ref, b_ref, o_ref):
    c = c_ref[...]
    s = c * _sigmoid(c)
    o_ref[0] = _dot_hilo(s, w_ref[0]) + b_ref[0]


def _modulation(cc, w_mod, b_mod):
    depth, d, n = w_mod.shape
    tn = 1024
    return pl.pallas_call(
        _mod_kernel,
        grid=(depth, n // tn),
        in_specs=[pl.BlockSpec((MOD_ROWS, d), lambda l, j: (0, 0)),
                  pl.BlockSpec((1, d, tn), lambda l, j: (l, 0, j)),
                  pl.BlockSpec((1, 1, tn), lambda l, j: (l, 0, j))],
        out_specs=pl.BlockSpec((1, MOD_ROWS, tn), lambda l, j: (l, 0, j)),
        out_shape=jax.ShapeDtypeStruct((depth, MOD_ROWS, n), F32),
        compiler_params=_cparams(2),
        name="modulation",
    )(cc, w_mod, b_mod.reshape(depth, 1, n))


def _inproj_kernel(x_ref, g_ref, sh_ref, sc_ref, w_ref,
                   uf_ref, hq_ref, zf_ref, zb_ref, hi_ref, hg_ref, q_ref, k_ref, v_ref):
    x = x_ref[...]
    y = x * lax.rsqrt(jnp.mean(x * x, axis=-1, keepdims=True) + EPS) * g_ref[...]
    a = (y * (1.0 + sc_ref[0]) + sh_ref[0]).astype(BF16)
    col = 0
    for ref in (uf_ref, hq_ref, zf_ref, zb_ref, hi_ref, hg_ref):
        ref[...] = _dot(a, w_ref[:, col:col + MIX]).astype(ref.dtype)
        col += MIX
    scale = NA_HEAD_DIM ** -0.5
    q_ref[...] = (_dot(a, w_ref[:, col:col + NA_WIDTH]) * scale).astype(BF16)
    col += NA_WIDTH
    k_ref[...] = _dot(a, w_ref[:, col:col + NA_WIDTH]).astype(BF16)
    col += NA_WIDTH
    v_ref[...] = _dot(a, w_ref[:, col:col + NA_WIDTH]).astype(BF16)


def _inproj(x2d, norm_g, modv, mod_base, rows_per_mod, w_in_b, tm):
    n, d = x2d.shape
    in_w = w_in_b.shape[1]

    def mod_idx(j):
        if rows_per_mod is None:
            return lambda i: (mod_base * 6 + j, 0, 0)
        tpm = rows_per_mod // tm
        return lambda i: ((mod_base + i // tpm) * 6 + j, 0, 0)

    def tok(w):
        return pl.BlockSpec((tm, w), lambda i: (i, 0))

    outs = [(MIX, BF16), (MIX, BF16), (MIX, F32), (MIX, F32), (MIX, BF16), (MIX, BF16),
            (NA_WIDTH, BF16), (NA_WIDTH, BF16), (NA_WIDTH, BF16)]
    return pl.pallas_call(
        _inproj_kernel,
        grid=(n // tm,),
        in_specs=[tok(d),
                  pl.BlockSpec((1, d), lambda i: (0, 0)),
                  pl.BlockSpec((1, 1, d), mod_idx(0)),
                  pl.BlockSpec((1, 1, d), mod_idx(1)),
                  _const_spec((d, in_w))],
        out_specs=[tok(w) for w, _ in outs],
        out_shape=[jax.ShapeDtypeStruct((n, w), dt) for w, dt in outs],
        compiler_params=_cparams(1),
        name="inproj",
    )(x2d, norm_g.reshape(1, d), modv, modv, w_in_b)


def _dft_table(t_len):
    kb = min(64, t_len)
    assert t_len % kb == 0
    t = jnp.arange(t_len // 2, dtype=jnp.int32)[None, :]
    k_hi = jnp.arange(t_len // kb, dtype=jnp.int32)[:, None] * kb
    k_lo = jnp.arange(kb, dtype=jnp.int32)[:, None]
    w = 2.0 * np.pi / t_len
    ang_hi = ((k_hi * t) % t_len).astype(F32) * w
    ang_lo = ((k_lo * t) % t_len).astype(F32) * w
    ch, sh = jnp.cos(ang_hi)[:, None, :], jnp.sin(ang_hi)[:, None, :]
    cl, sl = jnp.cos(ang_lo)[None, :, :], jnp.sin(ang_lo)[None, :, :]
    cos = (ch * cl - sh * sl).reshape(t_len, t_len // 2)
    sin = (sh * cl + ch * sl).reshape(t_len, t_len // 2)
    return jnp.concatenate([cos, sin], axis=1).astype(BF16)


def _channel_dft_consts():
    cd = np.outer(np.arange(FOURIER_GD), np.arange(FOURIER_GD)) % FOURIER_GD
    ang = 2.0 * np.pi * cd / FOURIER_GD
    eye = np.eye(FOURIER_GROUPS)
    return (jnp.asarray(np.kron(eye, np.cos(ang)), F32),
            jnp.asarray(np.kron(eye, np.sin(ang)), F32))


def _fourier_kernel(u_ref, tab_ref, c64_ref, s64_ref, w_ref, o_ref, pq_ref, pm_ref, *, t_len, tm):
    half = t_len // 2
    bs = min(256, half)
    nb = half // bs

    @pl.when(pl.program_id(1) == 0)
    def _():
        norm = (FOURIER_GD * t_len) ** -0.5
        w = w_ref[...]
        a = (_dot_hilo(c64_ref[...], w) * norm).astype(BF16)
        b = (_dot_hilo(s64_ref[...], w) * (-norm)).astype(BF16)
        p_i = lax.broadcasted_iota(jnp.int32, (bs, bs), 0)
        q_i = lax.broadcasted_iota(jnp.int32, (bs, bs), 1)
        rev_shift = jnp.where(q_i == bs - p_i, 1.0, 0.0).astype(BF16)
        first_row = jnp.where(p_i + q_i == 0, 1.0, 0.0).astype(BF16)
        for i in range(nb):
            src = half + (nb - 1 - i) * bs
            r = _dot(rev_shift, u_ref[0, src:src + bs, :])
            if i > 0:
                r = r + _dot(first_row, u_ref[0, src + bs:src + 2 * bs, :])
            u = u_ref[0, i * bs:(i + 1) * bs, :].astype(F32)
            pq_ref[i * bs:(i + 1) * bs, :] = _dot((u + r).astype(BF16), a).astype(BF16)
            pq_ref[half + i * bs:half + (i + 1) * bs, :] = _dot((u - r).astype(BF16), b).astype(BF16)
        pm_ref[...] = _dot(u_ref[0, half:half + 8, :], a)

    k = pl.program_id(1) * tm + lax.broadcasted_iota(jnp.int32, (tm, 1), 0)
    sign = (1 - 2 * (k & 1)).astype(F32)
    o_ref[0] = (_dot(tab_ref[...], pq_ref[...]) + sign * pm_ref[0:1, :]).astype(BF16)


def _fourier(u, tab, c64, s64, w_bd):
    b, t_len, _ = u.shape
    assert t_len % 64 == 0
    tm = min(512, t_len)
    return pl.pallas_call(
        functools.partial(_fourier_kernel, t_len=t_len, tm=tm),
        grid=(b, t_len // tm),
        in_specs=[pl.BlockSpec((1, t_len, MIX), lambda i, j: (i, 0, 0)),
                  pl.BlockSpec((tm, t_len), lambda i, j: (j, 0)),
                  _const_spec((MIX, MIX)), _const_spec((MIX, MIX)), _const_spec((MIX, MIX))],
        out_specs=pl.BlockSpec((1, tm, MIX), lambda i, j: (i, j, 0)),
        out_shape=jax.ShapeDtypeStruct((b, t_len, MIX), BF16),
        scratch_shapes=[pltpu.VMEM((t_len, MIX), BF16), pltpu.VMEM((8, MIX), F32)],
        compiler_params=_cparams(2),
        name="fourier",
    )(u, tab, c64, s64, w_bd)


def _hgrn_kernel(ql_ref, zfl_ref, zbl_ref, il_ref, gl_ref,
                 qc_ref, zfc_ref, zbc_ref, ic_ref, gc_ref, lb_ref, ng_ref,
                 *rest, n_lat, n_ctx, with_ctx):
    if with_ctx:
        ol_ref, oc_ref, st_ref, qd_ref, ks_ref, eal_ref, os_ref = rest
    else:
        ol_ref, st_ref, qd_ref, ks_ref, eal_ref, os_ref = rest
        oc_ref = None
    C = HGRN_CHUNK
    W = MIX
    lane_head = lax.broadcasted_iota(jnp.int32, (1, W), 1) // HGRN_DK
    head_mask = [lane_head == h for h in range(HGRN_HEADS)]
    bd = (lax.broadcasted_iota(jnp.int32, (W, W), 0) // HGRN_DK
          == lax.broadcasted_iota(jnp.int32, (W, W), 1) // HGRN_DK)
    bd_ones = jnp.where(bd, 1.0, 0.0).astype(BF16)

    def chunk_rows(n):
        return pl.ds(pl.multiple_of(n * C, C), C)

    def stack_heads(x):
        return jnp.concatenate([jnp.where(m, x, jnp.zeros_like(x)) for m in head_mask], axis=0)

    def group_chunks(n_chunks):
        return 4 if n_chunks % 4 == 0 else 2

    def phase_a(q_ref, zf_ref, zb_ref, i_ref, n_chunks, need_o):
        gc = group_chunks(n_chunks)
        gt = gc * C
        t_i = lax.broadcasted_iota(jnp.int32, (gt, gt), 0)
        s_i = lax.broadcasted_iota(jnp.int32, (gt, gt), 1)
        same = (t_i // C) == (s_i // C)
        allow = (same & (s_i <= t_i), same & (s_i >= t_i))
        allow_b16 = tuple(jnp.where(m, 1.0, 0.0).astype(BF16) for m in allow)
        t4 = lax.broadcasted_iota(jnp.int32, (gt, HGRN_HEADS * gt), 0)
        s4 = lax.broadcasted_iota(jnp.int32, (gt, HGRN_HEADS * gt), 1) % gt
        same4 = (t4 // C) == (s4 // C)
        allow4 = (same4 & (s4 <= t4), same4 & (s4 >= t4))

        def body(g, carry):
            rows = pl.ds(pl.multiple_of(g * gt, gt), gt)
            vs = stack_heads(i_ref[0, rows, :]) if need_o else None
            fs, parts = [], []
            for d, z_ref in ((0, zf_ref), (1, zb_ref)):
                lb = lb_ref[d:d + 1, :]
                f = lb + (1.0 - lb) * _sigmoid(z_ref[0, rows, :])
                lf = jnp.log(f)
                h1 = lf.astype(BF16)
                r1 = lf - h1.astype(F32)
                h2 = r1.astype(BF16)
                h3 = (r1 - h2.astype(F32)).astype(BF16)
                fs.append(f)
                parts.append((h1, h2, h3))
            a_s = [_dot(allow_b16[d], parts[d][0]) + _dot(allow_b16[d], parts[d][1])
                   + _dot(allow_b16[d], parts[d][2]) for d in (0, 1)]
            scs = []
            for d in (0, 1):
                a = a_s[d]
                kd = (1.0 - fs[d]) * jnp.exp(-a)
                ends = [a[c * C + C - 1:c * C + C, :] if d == 0 else a[c * C:c * C + 1, :]
                        for c in range(gc)]
                eal = [jnp.exp(e) for e in ends]
                ks_ref[d, rows, :] = (kd * jnp.concatenate(
                    [jnp.broadcast_to(e, (C, W)) for e in eal], axis=0)).astype(BF16)
                eal_ref[d, pl.ds(pl.multiple_of(g * gc * 8, gc * 8), gc * 8), :] = jnp.concatenate(
                    [jnp.broadcast_to(e, (8, W)) for e in eal], axis=0)
                if need_o:
                    qdb = (q_ref[0, rows, :].astype(F32) * jnp.exp(a)).astype(BF16)
                    qd_ref[d, rows, :] = qdb
                    scs.append(jnp.where(allow4[d], _dot_nt(qdb, stack_heads(kd.astype(BF16))),
                                         0.0).astype(BF16))
            if need_o:
                os_ref[rows, :] = _dot(scs[0], vs) + _dot(scs[1], vs)
            return carry

        lax.fori_loop(0, n_chunks // gc, body, 0, unroll=2)

    def phase_b(i_ref, n_chunks, need_o):
        gc = group_chunks(n_chunks)

        def body(g, carry):
            idx = [[g * gc + c, n_chunks - 1 - (g * gc + c)] for c in range(gc)]
            upd = [[_dot_tn(stack_heads(i_ref[0, chunk_rows(m), :]),
                            stack_heads(ks_ref[d, chunk_rows(m), :]))
                    for d, m in enumerate(pair)] for pair in idx]
            for c in range(gc):
                for d in (0, 1):
                    m = idx[c][d]
                    rows = chunk_rows(m)
                    st = st_ref[d]
                    if need_o:
                        os_ref[rows, :] = os_ref[rows, :] + _dot_nt(qd_ref[d, rows, :],
                                                                     st.astype(BF16))
                    eal = eal_ref[d, pl.ds(pl.multiple_of(m * 8, 8), 8), :][0:1, :]
                    st_ref[d] = st * eal + upd[c][d]
            return carry

        lax.fori_loop(0, n_chunks // gc, body, 0, unroll=2)

    def phase_c(g_ref, o_ref, n_rows):
        rt = next(t for t in (256, 128, 64) if n_rows % t == 0)

        def body(t, carry):
            rows = pl.ds(pl.multiple_of(t * rt, rt), rt)
            o = os_ref[rows, :]
            hi, lo = _split2(o * o)
            ms = (_dot(hi, bd_ones) + _dot(lo, bd_ones)) * (1.0 / HGRN_DK)
            y = o * lax.rsqrt(ms + EPS) * ng_ref[...]
            g = g_ref[0, rows, :].astype(F32)
            o_ref[0, rows, :] = (y * (g * _sigmoid(g))).astype(BF16)
            return carry

        lax.fori_loop(0, n_rows // rt, body, 0)

    st_ref[...] = jnp.zeros_like(st_ref)
    phase_a(qc_ref, zfc_ref, zbc_ref, ic_ref, n_ctx, with_ctx)
    phase_b(ic_ref, n_ctx, with_ctx)
    if with_ctx:
        phase_c(gc_ref, oc_ref, n_ctx * C)
    phase_a(ql_ref, zfl_ref, zbl_ref, il_ref, n_lat, True)
    phase_b(il_ref, n_lat, True)
    phase_c(gl_ref, ol_ref, n_lat * C)


def _hgrn(lat, ctx, lb, norm_g4, with_ctx):
    b, s, _ = lat[0].shape
    l = ctx[0].shape[1]
    assert s % (2 * HGRN_CHUNK) == 0 and l % (2 * HGRN_CHUNK) == 0

    def seq(t):
        return pl.BlockSpec((1, t, MIX), lambda i: (i, 0, 0), pipeline_mode=pl.Buffered(1))

    out_specs = [pl.BlockSpec((1, s, MIX), lambda i: (i, 0, 0))]
    out_shape = [jax.ShapeDtypeStruct((b, s, MIX), BF16)]
    t_max = max(s, l)
    scratch = [pltpu.VMEM((2, MIX, MIX), F32),
               pltpu.VMEM((2, t_max, MIX), BF16),
               pltpu.VMEM((2, t_max, MIX), BF16),
               pltpu.VMEM((2, t_max // HGRN_CHUNK * 8, MIX), F32),
               pltpu.VMEM((t_max, MIX), F32)]
    if with_ctx:
        out_specs.append(pl.BlockSpec((1, l, MIX), lambda i: (i, 0, 0)))
        out_shape.append(jax.ShapeDtypeStruct((b, l, MIX), BF16))
    res = pl.pallas_call(
        functools.partial(_hgrn_kernel, n_lat=s // HGRN_CHUNK, n_ctx=l // HGRN_CHUNK,
                          with_ctx=with_ctx),
        grid=(b,),
        in_specs=[seq(s)] * 5 + [seq(l)] * 5 + [_const_spec((2, MIX)), _const_spec((1, MIX))],
        out_specs=out_specs,
        out_shape=out_shape,
        scratch_shapes=scratch,
        compiler_params=_cparams(1),
        name="hgrn",
    )(*lat, *ctx, lb, norm_g4)
    return (res[0], res[1]) if with_ctx else (res[0], None)


def _na_bias_table(rpb):
    n_h = rpb.shape[0]
    w, nr = GRID_W, 2 * NA_KH - 1
    side = (w - 1) - (NA_KW - 1)
    p = jnp.pad(rpb.astype(F32), ((0, 0), (0, 0), (side, side + 1)))
    skew = jnp.broadcast_to(p[:, :, None, :], (n_h, nr, w, 2 * w)).reshape(n_h, nr, 2 * w * w)
    skew = skew[:, :, :w * (2 * w - 1)].reshape(n_h, nr, w, 2 * w - 1)
    toep = skew[..., w - 1:]
    qc = np.arange(w)[:, None]
    kc = np.arange(w)[None, :]
    win = np.clip(qc - NA_KW // 2, 0, w - NA_KW)
    valid = (kc >= win) & (kc < win + NA_KW)
    toep = jnp.where(valid[None, None], toep, NEG_INF)
    two = jnp.concatenate([toep[:, :-1], toep[:, 1:]], axis=-1)
    two = two.reshape(n_h // 2, 2, nr - 1, w, 2 * w).transpose(0, 2, 1, 3, 4)
    return two.reshape(n_h // 2, nr - 1, 2 * w, 2 * w)


def _softmax_pv(s_list, v_list):
    m = functools.reduce(jnp.maximum, [jnp.max(s, axis=-1, keepdims=True) for s in s_list])
    p_list = [jnp.exp(s - m) for s in s_list]
    l = functools.reduce(jnp.add, [jnp.sum(p, axis=-1, keepdims=True) for p in p_list])
    o = functools.reduce(jnp.add, [_dot(p.astype(BF16), v) for p, v in zip(p_list, v_list)])
    return o * (1.0 / l)


NA_ROWS_PER_STEP = 8


def _na_kernel(q_ref, k_ref, v_ref, kc_ref, vc_ref, bias_ref, o_ref, *, rows):
    lane = lax.broadcasted_iota(jnp.int32, (1, 2 * NA_HEAD_DIM), 1)
    first = lane < NA_HEAD_DIM
    kc = kc_ref[0]
    vc = vc_ref[0]
    win = NA_KH * GRID_W
    nq = 2 * GRID_W

    def body(g, carry):
        qs, kws, vws, dis, qrows = [], [], [], [], []
        for j in range(NA_ROWS_PER_STEP):
            r = g * NA_ROWS_PER_STEP + j
            rs = jnp.clip(r - NA_KH // 2, 0, rows - NA_KH)
            dis.append(r - rs)
            qrows.append(pl.ds(pl.multiple_of(r * GRID_W, GRID_W), GRID_W))
            krows = pl.ds(pl.multiple_of(rs * GRID_W, GRID_W), win)
            q = q_ref[0, qrows[-1], :]
            qs += [jnp.where(first, q, jnp.zeros_like(q)), jnp.where(first, jnp.zeros_like(q), q)]
            kws.append(k_ref[0, krows, :])
            vws.append(v_ref[0, krows, :])
        qs = jnp.concatenate(qs, axis=0)
        sx = _dot_nt(qs, kc)
        pws, pxs, inv_ls = [], [], []
        for j in range(NA_ROWS_PER_STEP):
            ri0 = NA_KH - 1 - dis[j]
            bias = jnp.concatenate([bias_ref[0, ri0 + 2 * c] for c in range(NA_KH // 2)], axis=1)
            sw = _dot_nt(qs[j * nq:(j + 1) * nq], kws[j]) + bias
            sxj = sx[j * nq:(j + 1) * nq]
            m = jnp.maximum(jnp.max(sw, axis=-1, keepdims=True), jnp.max(sxj, axis=-1, keepdims=True))
            pw = jnp.exp(sw - m)
            px = jnp.exp(sxj - m)
            inv_ls.append(1.0 / (jnp.sum(pw, axis=-1, keepdims=True) + jnp.sum(px, axis=-1, keepdims=True)))
            pws.append(pw.astype(BF16))
            pxs.append(px.astype(BF16))
        oc = _dot(jnp.concatenate(pxs, axis=0), vc)
        for j in range(NA_ROWS_PER_STEP):
            o = (_dot(pws[j], vws[j]) + oc[j * nq:(j + 1) * nq]) * inv_ls[j]
            o_ref[0, qrows[j], :] = jnp.where(first, o[:GRID_W], o[GRID_W:]).astype(BF16)
        return carry

    lax.fori_loop(0, rows // NA_ROWS_PER_STEP, body, 0)


def _na_latent(q, k, v, kc, vc, bias):
    b, s, _ = q.shape
    l = kc.shape[1]
    rows = s // GRID_W
    assert s % GRID_W == 0 and rows >= NA_KH and rows % NA_ROWS_PER_STEP == 0
    hp = 2 * NA_HEAD_DIM

    def seq(t):
        return pl.BlockSpec((1, t, hp), lambda i, j: (i, 0, j))

    return pl.pallas_call(
        functools.partial(_na_kernel, rows=rows),
        grid=(b, NA_HEADS // 2),
        in_specs=[seq(s), seq(s), seq(s), seq(l), seq(l),
                  pl.BlockSpec((1, 2 * NA_KH - 2, 2 * GRID_W, 2 * GRID_W),
                               lambda i, j: (j, 0, 0, 0))],
        out_specs=seq(s),
        out_shape=jax.ShapeDtypeStruct((b, s, NA_WIDTH), BF16),
        compiler_params=_cparams(2),
        name="na_latent",
    )(q, k, v, kc, vc, bias)


def _na_ctx_kernel(q_ref, k_ref, v_ref, o_ref):
    lane = lax.broadcasted_iota(jnp.int32, (1, 2 * NA_HEAD_DIM), 1)
    first = lane < NA_HEAD_DIM
    q, k, v = q_ref[0], k_ref[0], v_ref[0]
    outs = []
    for h in range(2):
        qh = jnp.where(first if h == 0 else ~first, q, jnp.zeros_like(q))
        outs.append(_softmax_pv([_dot_nt(qh, k)], [v]))
    o_ref[0] = jnp.where(first, outs[0], outs[1]).astype(BF16)


def _na_context(q, k, v):
    b, l, _ = q.shape
    spec = pl.BlockSpec((1, l, 2 * NA_HEAD_DIM), lambda i, j: (i, 0, j))
    return pl.pallas_call(
        _na_ctx_kernel,
        grid=(b, NA_HEADS // 2),
        in_specs=[spec, spec, spec],
        out_specs=spec,
        out_shape=jax.ShapeDtypeStruct((b, l, NA_WIDTH), BF16),
        compiler_params=_cparams(2),
        name="na_context",
    )(q, k, v)


def _ffn_kernel(x_ref, f_ref, h_ref, n_ref, g1_ref, sh_ref, sc_ref, g2_ref, ng_ref, fg_ref,
                wo_ref, wg_ref, wu_ref, wd_ref, o_ref, *, final):
    y = (_dot(f_ref[...], wo_ref[0:MIX, :]) + _dot(h_ref[...], wo_ref[MIX:2 * MIX, :])
         + _dot(n_ref[...], wo_ref[2 * MIX:, :]))
    x1 = x_ref[...] + g1_ref[0] * y
    hn = x1 * lax.rsqrt(jnp.mean(x1 * x1, axis=-1, keepdims=True) + EPS) * ng_ref[...]
    hm = (hn * (1.0 + sc_ref[0]) + sh_ref[0]).astype(BF16)
    d_ff = wg_ref.shape[1]
    acc = jnp.zeros(x1.shape, F32)
    for j in range(d_ff // FF_CHUNK):
        cols = slice(j * FF_CHUNK, (j + 1) * FF_CHUNK)
        g = _dot(hm, wg_ref[:, cols])
        u = _dot(hm, wu_ref[:, cols])
        act = (g * _sigmoid(g) * u).astype(BF16)
        acc = acc + _dot(act, wd_ref[cols, :])
    x2 = x1 + g2_ref[0] * acc
    if final:
        x2 = x2 * lax.rsqrt(jnp.mean(x2 * x2, axis=-1, keepdims=True) + EPS) * fg_ref[...]
    o_ref[...] = x2


def _ffn(x2d, f, h, n, norm_g, final_g, modv, mod_base, rows_per_mod, wo, wg, wu, wd, tm, final):
    nt, d = x2d.shape
    d_ff = wg.shape[1]
    assert d_ff % FF_CHUNK == 0

    def mod_idx(j):
        if rows_per_mod is None:
            return lambda i: (mod_base * 6 + j, 0, 0)
        tpm = rows_per_mod // tm
        return lambda i: ((mod_base + i // tpm) * 6 + j, 0, 0)

    def tok(w):
        return pl.BlockSpec((tm, w), lambda i: (i, 0))

    def vec():
        return pl.BlockSpec((1, d), lambda i: (0, 0))

    return pl.pallas_call(
        functools.partial(_ffn_kernel, final=final),
        grid=(nt // tm,),
        in_specs=[tok(d), tok(MIX), tok(MIX), tok(NA_WIDTH),
                  pl.BlockSpec((1, 1, d), mod_idx(2)), pl.BlockSpec((1, 1, d), mod_idx(3)),
                  pl.BlockSpec((1, 1, d), mod_idx(4)), pl.BlockSpec((1, 1, d), mod_idx(5)),
                  vec(), vec(),
                  _const_spec(wo.shape), _const_spec(wg.shape), _const_spec(wu.shape),
                  _const_spec(wd.shape)],
        out_specs=tok(d),
        out_shape=jax.ShapeDtypeStruct((nt, d), F32),
        compiler_params=_cparams(1),
        name="outproj_ffn",
    )(x2d, f, h, n, modv, modv, modv, modv, norm_g.reshape(1, d), final_g.reshape(1, d),
      wo, wg, wu, wd)


def _token_tile(n):
    for tm in (512, 256, 128, 64, 32, 16, 8):
        if n % tm == 0:
            return tm
    raise ValueError(f"token count {n} is not a multiple of 8")


def kernel(x, c, ctx, c_ctx, w_mod, b_mod, norm1_g, w_in, fourier_w, hgrn_lb, hgrn_norm_g, na_rpb,
           w_out, norm2_g, w_ffn_gate, w_ffn_up, w_ffn_down, final_norm_g):
    b, s, d = x.shape
    l = ctx.shape[1]
    depth = w_mod.shape[0]
    assert b < MOD_ROWS and d == 2 * MIX + NA_WIDTH

    cc = jnp.zeros((MOD_ROWS, d), F32).at[:b].set(c).at[b].set(c_ctx)
    mod = _modulation(cc, w_mod, b_mod)
    modv = mod.reshape(depth * MOD_ROWS * 6, 1, d)

    lbp = jax.nn.softmax(hgrn_lb.astype(F32), axis=0)
    lower = jnp.cumsum(lbp, axis=0) - lbp[0:1]
    norm_g4 = jnp.tile(hgrn_norm_g.astype(F32), (1, HGRN_HEADS))[:, None, :]

    tab_s, tab_l = _dft_table(s), _dft_table(l)
    c64, s64 = _channel_dft_consts()
    eye_g = jnp.eye(FOURIER_GROUPS, dtype=F32)

    tm_s = _token_tile(s)
    tm_c = _token_tile(b * l)
    xl = x.reshape(b * s, d)
    xc = ctx.reshape(b * l, d)
    for li in range(depth):
        with_ctx = li < depth - 1
        final = li == depth - 1
        base = li * MOD_ROWS
        w_in_b = w_in[li].astype(BF16)
        ul = _inproj(xl, norm1_g[li], modv, base, s, w_in_b, tm_s)
        uc = _inproj(xc, norm1_g[li], modv, base + b, None, w_in_b, tm_c)
        ul = [a.reshape(b, s, -1) for a in ul]
        uc = [a.reshape(b, l, -1) for a in uc]

        w_bd = (eye_g[:, None, :, None] * fourier_w[li][:, :, None, :]).reshape(MIX, MIX)
        f_lat = _fourier(ul[0], tab_s, c64, s64, w_bd)
        h_lat, h_ctx = _hgrn(ul[1:6], uc[1:6], lower[li], norm_g4[li], with_ctx)
        n_lat = _na_latent(ul[6], ul[7], ul[8], uc[7], uc[8], _na_bias_table(na_rpb[li]))

        wo, wg = w_out[li].astype(BF16), w_ffn_gate[li].astype(BF16)
        wu, wd = w_ffn_up[li].astype(BF16), w_ffn_down[li].astype(BF16)
        if with_ctx:
            f_ctx = _fourier(uc[0], tab_l, c64, s64, w_bd)
            n_ctx = _na_context(uc[6], uc[7], uc[8])
            xc = _ffn(xc, f_ctx.reshape(b * l, -1), h_ctx.reshape(b * l, -1),
                      n_ctx.reshape(b * l, -1), norm2_g[li], final_norm_g, modv, base + b, None,
                      wo, wg, wu, wd, tm_c, False)
        xl = _ffn(xl, f_lat.reshape(b * s, -1), h_lat.reshape(b * s, -1), n_lat.reshape(b * s, -1),
                  norm2_g[li], final_norm_g, modv, base, s, wo, wg, wu, wd, tm_s, final)
    return xl.reshape(b, s, d)
```

```python
import functools

import jax
import jax.numpy as jnp
import numpy as np
from jax import lax
from jax.experimental import pallas as pl
from jax.experimental.pallas import tpu as pltpu

F32 = jnp.float32
BF16 = jnp.bfloat16

EPS = 1e-6
NEG_INF = -1e30
GRID_W = 64
FOURIER_GROUPS = 4
FOURIER_GD = 64
HGRN_DK = 64
HGRN_HEADS = 4
HGRN_CHUNK = 32
NA_HEAD_DIM = 64
NA_HEADS = 8
NA_KH = 8
NA_KW = 16
MIX = 256
NA_WIDTH = NA_HEADS * NA_HEAD_DIM
MOD_ROWS = 16
FF_CHUNK = 256
VMEM_LIMIT = 56 * 1024 * 1024


def _cparams(n_axes):
    return pltpu.CompilerParams(dimension_semantics=("arbitrary",) * n_axes,
                                vmem_limit_bytes=VMEM_LIMIT)


def _split2(a):
    hi = a.astype(BF16)
    lo = (a - hi.astype(F32)).astype(BF16)
    return hi, lo


def _dot(a, b):
    return jnp.dot(a, b, preferred_element_type=F32)


def _dot_nt(a, b):
    return lax.dot_general(a, b, (((1,), (1,)), ((), ())), preferred_element_type=F32)


def _dot_tn(a, b):
    return lax.dot_general(a, b, (((0,), (0,)), ((), ())), preferred_element_type=F32)


def _dot_hilo(a, b):
    ah, al = _split2(a)
    bh, bl = _split2(b)
    return _dot(ah, bh) + _dot(al, bh) + _dot(ah, bl)


def _sigmoid(z):
    return 1.0 / (1.0 + jnp.exp(-z))


def _const_spec(shape):
    nd = len(shape)
    return pl.BlockSpec(shape, lambda *_: (0,) * nd, pipeline_mode=pl.Buffered(1))


def _mod_kernel(c_ref, w_ref, b_ref, o_ref):
    c = c_ref[...]
    s = c * _sigmoid(c)
    o_ref[0] = _dot_hilo(s, w_ref[0]) + b_ref[0]


def _modulation(cc, w_mod, b_mod):
    depth, d, n = w_mod.shape
    tn = 1024
    return pl.pallas_call(
        _mod_kernel,
        grid=(depth, n // tn),
        in_specs=[pl.BlockSpec((MOD_ROWS, d), lambda l, j: (0, 0)),
                  pl.BlockSpec((1, d, tn), lambda l, j: (l, 0, j)),
                  pl.BlockSpec((1, 1, tn), lambda l, j: (l, 0, j))],
        out_specs=pl.BlockSpec((1, MOD_ROWS, tn), lambda l, j: (l, 0, j)),
        out_shape=jax.ShapeDtypeStruct((depth, MOD_ROWS, n), F32),
        compiler_params=_cparams(2),
        name="modulation",
    )(cc, w_mod, b_mod.reshape(depth, 1, n))


def _inproj_kernel(x_ref, g_ref, sh_ref, sc_ref, w_ref,
                   uf_ref, hq_ref, zf_ref, zb_ref, hi_ref, hg_ref, q_ref, k_ref, v_ref):
    x = x_ref[...]
    y = x * lax.rsqrt(jnp.mean(x * x, axis=-1, keepdims=True) + EPS) * g_ref[...]
    a = (y * (1.0 + sc_ref[0]) + sh_ref[0]).astype(BF16)
    col = 0
    for ref in (uf_ref, hq_ref, zf_ref, zb_ref, hi_ref, hg_ref):
        ref[...] = _dot(a, w_ref[:, col:col + MIX]).astype(ref.dtype)
        col += MIX
    scale = NA_HEAD_DIM ** -0.5
    q_ref[...] = (_dot(a, w_ref[:, col:col + NA_WIDTH]) * scale).astype(BF16)
    col += NA_WIDTH
    k_ref[...] = _dot(a, w_ref[:, col:col + NA_WIDTH]).astype(BF16)
    col += NA_WIDTH
    v_ref[...] = _dot(a, w_ref[:, col:col + NA_WIDTH]).astype(BF16)


def _inproj(x2d, norm_g, modv, mod_base, rows_per_mod, w_in_b, tm):
    n, d = x2d.shape
    in_w = w_in_b.shape[1]

    def mod_idx(j):
        if rows_per_mod is None:
            return lambda i: (mod_base * 6 + j, 0, 0)
        tpm = rows_per_mod // tm
        return lambda i: ((mod_base + i // tpm) * 6 + j, 0, 0)

    def tok(w):
        return pl.BlockSpec((tm, w), lambda i: (i, 0))

    outs = [(MIX, BF16), (MIX, BF16), (MIX, F32), (MIX, F32), (MIX, BF16), (MIX, BF16),
            (NA_WIDTH, BF16), (NA_WIDTH, BF16), (NA_WIDTH, BF16)]
    return pl.pallas_call(
        _inproj_kernel,
        grid=(n // tm,),
        in_specs=[tok(d),
                  pl.BlockSpec((1, d), lambda i: (0, 0)),
                  pl.BlockSpec((1, 1, d), mod_idx(0)),
                  pl.BlockSpec((1, 1, d), mod_idx(1)),
                  _const_spec((d, in_w))],
        out_specs=[tok(w) for w, _ in outs],
        out_shape=[jax.ShapeDtypeStruct((n, w), dt) for w, dt in outs],
        compiler_params=_cparams(1),
        name="inproj",
    )(x2d, norm_g.reshape(1, d), modv, modv, w_in_b)


def _dft_table(t_len):
    kb = min(64, t_len)
    assert t_len % kb == 0
    t = jnp.arange(t_len // 2, dtype=jnp.int32)[None, :]
    k_hi = jnp.arange(t_len // kb, dtype=jnp.int32)[:, None] * kb
    k_lo = jnp.arange(kb, dtype=jnp.int32)[:, None]
    w = 2.0 * np.pi / t_len
    ang_hi = ((k_hi * t) % t_len).astype(F32) * w
    ang_lo = ((k_lo * t) % t_len).astype(F32) * w
    ch, sh = jnp.cos(ang_hi)[:, None, :], jnp.sin(ang_hi)[:, None, :]
    cl, sl = jnp.cos(ang_lo)[None, :, :], jnp.sin(ang_lo)[None, :, :]
    cos = (ch * cl - sh * sl).reshape(t_len, t_len // 2)
    sin = (sh * cl + ch * sl).reshape(t_len, t_len // 2)
    return jnp.concatenate([cos, sin], axis=1).astype(BF16)


def _channel_dft_consts():
    cd = np.outer(np.arange(FOURIER_GD), np.arange(FOURIER_GD)) % FOURIER_GD
    ang = 2.0 * np.pi * cd / FOURIER_GD
    eye = np.eye(FOURIER_GROUPS)
    return (jnp.asarray(np.kron(eye, np.cos(ang)), F32),
            jnp.asarray(np.kron(eye, np.sin(ang)), F32))


def _fourier_kernel(u_ref, tab_ref, c64_ref, s64_ref, w_ref, o_ref, pq_ref, pm_ref, *, t_len, tm):
    half = t_len // 2
    bs = min(256, half)
    nb = half // bs
    bi = pl.program_id(1)

    @pl.when(pl.program_id(0) == 0)
    def _():
        norm = (FOURIER_GD * t_len) ** -0.5
        w = w_ref[...]
        a = (_dot_hilo(c64_ref[...], w) * norm).astype(BF16)
        b = (_dot_hilo(s64_ref[...], w) * (-norm)).astype(BF16)
        p_i = lax.broadcasted_iota(jnp.int32, (bs, bs), 0)
        q_i = lax.broadcasted_iota(jnp.int32, (bs, bs), 1)
        rev_shift = jnp.where(q_i == bs - p_i, 1.0, 0.0).astype(BF16)
        first_row = jnp.where(p_i + q_i == 0, 1.0, 0.0).astype(BF16)
        for i in range(nb):
            src = half + (nb - 1 - i) * bs
            r = _dot(rev_shift, u_ref[0, src:src + bs, :])
            if i > 0:
                r = r + _dot(first_row, u_ref[0, src + bs:src + 2 * bs, :])
            u = u_ref[0, i * bs:(i + 1) * bs, :].astype(F32)
            pq_ref[bi, i * bs:(i + 1) * bs, :] = _dot((u + r).astype(BF16), a).astype(BF16)
            pq_ref[bi, half + i * bs:half + (i + 1) * bs, :] = _dot((u - r).astype(BF16),
                                                                    b).astype(BF16)
        pm_ref[bi] = _dot(u_ref[0, half:half + 8, :], a)

    k = pl.program_id(0) * tm + lax.broadcasted_iota(jnp.int32, (tm, 1), 0)
    sign = (1 - 2 * (k & 1)).astype(F32)
    o_ref[0] = (_dot(tab_ref[...], pq_ref[bi]) + sign * pm_ref[bi, 0:1, :]).astype(BF16)


def _fourier(u, tab, c64, s64, w_bd):
    b, t_len, _ = u.shape
    assert t_len % 64 == 0
    tm = min(512, t_len)
    return pl.pallas_call(
        functools.partial(_fourier_kernel, t_len=t_len, tm=tm),
        grid=(t_len // tm, b),
        in_specs=[pl.BlockSpec((1, t_len, MIX), lambda j, i: (jnp.where(j == 0, i, b - 1), 0, 0)),
                  pl.BlockSpec((tm, t_len), lambda j, i: (j, 0)),
                  _const_spec((MIX, MIX)), _const_spec((MIX, MIX)), _const_spec((MIX, MIX))],
        out_specs=pl.BlockSpec((1, tm, MIX), lambda j, i: (i, j, 0)),
        out_shape=jax.ShapeDtypeStruct((b, t_len, MIX), BF16),
        scratch_shapes=[pltpu.VMEM((b, t_len, MIX), BF16), pltpu.VMEM((b, 8, MIX), F32)],
        compiler_params=_cparams(2),
        name="fourier",
    )(u, tab, c64, s64, w_bd)


def _hgrn_kernel(ql_ref, zfl_ref, zbl_ref, il_ref, gl_ref,
                 qc_ref, zfc_ref, zbc_ref, ic_ref, gc_ref, lb_ref, ng_ref,
                 *rest, n_lat, n_ctx, with_ctx):
    if with_ctx:
        ol_ref, oc_ref, st_ref, qd_ref, ks_ref, vt_ref, eal_ref, os_ref = rest
    else:
        ol_ref, st_ref, qd_ref, ks_ref, vt_ref, eal_ref, os_ref = rest
        oc_ref = None
    C = HGRN_CHUNK
    W = MIX
    lane_head = lax.broadcasted_iota(jnp.int32, (1, W), 1) // HGRN_DK
    head_mask = [lane_head == h for h in range(HGRN_HEADS)]
    bd = (lax.broadcasted_iota(jnp.int32, (W, W), 0) // HGRN_DK
          == lax.broadcasted_iota(jnp.int32, (W, W), 1) // HGRN_DK)
    bd_ones = jnp.where(bd, 1.0, 0.0).astype(BF16)

    def chunk_rows(n):
        return pl.ds(pl.multiple_of(n * C, C), C)

    def stack_heads(x):
        return jnp.concatenate([jnp.where(m, x, jnp.zeros_like(x)) for m in head_mask], axis=0)

    def group_chunks(n_chunks):
        return 4 if n_chunks % 4 == 0 else 2

    def phase_a(q_ref, zf_ref, zb_ref, i_ref, n_chunks, need_o):
        gc = group_chunks(n_chunks)
        gt = gc * C
        t_i = lax.broadcasted_iota(jnp.int32, (gt, gt), 0)
        s_i = lax.broadcasted_iota(jnp.int32, (gt, gt), 1)
        same = (t_i // C) == (s_i // C)
        allow = (same & (s_i <= t_i), same & (s_i >= t_i))
        allow_b16 = tuple(jnp.where(m, 1.0, 0.0).astype(BF16) for m in allow)
        t4 = lax.broadcasted_iota(jnp.int32, (gt, HGRN_HEADS * gt), 0)
        s4 = lax.broadcasted_iota(jnp.int32, (gt, HGRN_HEADS * gt), 1) % gt
        same4 = (t4 // C) == (s4 // C)
        allow4 = (same4 & (s4 <= t4), same4 & (s4 >= t4))

        def body(g, carry):
            rows = pl.ds(pl.multiple_of(g * gt, gt), gt)
            v = i_ref[0, rows, :]
            vs = stack_heads(v) if need_o else None
            for c in range(gc):
                vt_ref[g * gc + c] = stack_heads(v[c * C:(c + 1) * C, :]).T
            fs, parts = [], []
            for d, z_ref in ((0, zf_ref), (1, zb_ref)):
                lb = lb_ref[d:d + 1, :]
                f = lb + (1.0 - lb) * _sigmoid(z_ref[0, rows, :])
                lf = jnp.log(f)
                h1 = lf.astype(BF16)
                r1 = lf - h1.astype(F32)
                h2 = r1.astype(BF16)
                h3 = (r1 - h2.astype(F32)).astype(BF16)
                fs.append(f)
                parts.append((h1, h2, h3))
            a_s = [_dot(allow_b16[d], parts[d][0]) + _dot(allow_b16[d], parts[d][1])
                   + _dot(allow_b16[d], parts[d][2]) for d in (0, 1)]
            scs = []
            for d in (0, 1):
                a = a_s[d]
                kd = (1.0 - fs[d]) * jnp.exp(-a)
                ends = [a[c * C + C - 1:c * C + C, :] if d == 0 else a[c * C:c * C + 1, :]
                        for c in range(gc)]
                eal = [jnp.exp(e) for e in ends]
                ks_ref[d, rows, :] = (kd * jnp.concatenate(
                    [jnp.broadcast_to(e, (C, W)) for e in eal], axis=0)).astype(BF16)
                eal_ref[d, pl.ds(pl.multiple_of(g * gc * 8, gc * 8), gc * 8), :] = jnp.concatenate(
                    [jnp.broadcast_to(e, (8, W)) for e in eal], axis=0)
                if need_o:
                    qdb = (q_ref[0, rows, :].astype(F32) * jnp.exp(a)).astype(BF16)
                    qd_ref[d, rows, :] = qdb
                    scs.append(jnp.where(allow4[d], _dot_nt(qdb, stack_heads(kd.astype(BF16))),
                                         0.0).astype(BF16))
            if need_o:
                os_ref[rows, :] = _dot(scs[0], vs) + _dot(scs[1], vs)
            return carry

        lax.fori_loop(0, n_chunks // gc, body, 0, unroll=2)

    def phase_b(i_ref, n_chunks, need_o):
        gc = group_chunks(n_chunks)
        pw = 2 * HGRN_DK
        pairs = [slice(p * pw, (p + 1) * pw) for p in range(HGRN_HEADS // 2)]

        def body(g, carry):
            idx = [[g * gc + c, n_chunks - 1 - (g * gc + c)] for c in range(gc)]

            def updates(c):
                out = []
                for d, m in enumerate(idx[c]):
                    vt = vt_ref[m]
                    kss = stack_heads(ks_ref[d, chunk_rows(m), :])
                    out.append([_dot(vt[s, :], kss[:, s]) for s in pairs])
                return out

            upd = {0: updates(0)}
            inters = []
            for c in range(gc):
                if c + 1 < gc:
                    upd[c + 1] = updates(c + 1)
                for d in (0, 1):
                    m = idx[c][d]
                    rows = chunk_rows(m)
                    eal = eal_ref[d, pl.ds(pl.multiple_of(m * 8, 8), 8), :][0:1, :]
                    sts = [st_ref[d, p] for p in range(len(pairs))]
                    if need_o:
                        qd = qd_ref[d, rows, :]
                        inter = [_dot_nt(qd[:, s], st.astype(BF16)) for s, st in zip(pairs, sts)]
                        inters.append((rows, jnp.concatenate(inter, axis=1)))
                    for p, s in enumerate(pairs):
                        st_ref[d, p] = sts[p] * eal[:, s] + upd[c][d][p]
            for rows, inter in inters:
                os_ref[rows, :] = os_ref[rows, :] + inter
            return carry

        lax.fori_loop(0, n_chunks // gc, body, 0, unroll=2)

    def phase_c(g_ref, o_ref, n_rows):
        rt = next(t for t in (256, 128, 64) if n_rows % t == 0)

        def body(t, carry):
            rows = pl.ds(pl.multiple_of(t * rt, rt), rt)
            o = os_ref[rows, :]
            hi, lo = _split2(o * o)
            ms = (_dot(hi, bd_ones) + _dot(lo, bd_ones)) * (1.0 / HGRN_DK)
            y = o * lax.rsqrt(ms + EPS) * ng_ref[...]
            g = g_ref[0, rows, :].astype(F32)
            o_ref[0, rows, :] = (y * (g * _sigmoid(g))).astype(BF16)
            return carry

        lax.fori_loop(0, n_rows // rt, body, 0)

    st_ref[...] = jnp.zeros_like(st_ref)
    phase_a(qc_ref, zfc_ref, zbc_ref, ic_ref, n_ctx, with_ctx)
    phase_b(ic_ref, n_ctx, with_ctx)
    if with_ctx:
        phase_c(gc_ref, oc_ref, n_ctx * C)
    phase_a(ql_ref, zfl_ref, zbl_ref, il_ref, n_lat, True)
    phase_b(il_ref, n_lat, True)
    phase_c(gl_ref, ol_ref, n_lat * C)


def _hgrn(lat, ctx, lb, norm_g4, with_ctx):
    b, s, _ = lat[0].shape
    l = ctx[0].shape[1]
    assert s % (2 * HGRN_CHUNK) == 0 and l % (2 * HGRN_CHUNK) == 0

    def seq(t):
        return pl.BlockSpec((1, t, MIX), lambda i: (i, 0, 0), pipeline_mode=pl.Buffered(1))

    out_specs = [pl.BlockSpec((1, s, MIX), lambda i: (i, 0, 0))]
    out_shape = [jax.ShapeDtypeStruct((b, s, MIX), BF16)]
    t_max = max(s, l)
    scratch = [pltpu.VMEM((2, HGRN_HEADS // 2, 2 * HGRN_DK, 2 * HGRN_DK), F32),
               pltpu.VMEM((2, t_max, MIX), BF16),
               pltpu.VMEM((2, t_max, MIX), BF16),
               pltpu.VMEM((t_max // HGRN_CHUNK, MIX, HGRN_HEADS * HGRN_CHUNK), BF16),
               pltpu.VMEM((2, t_max // HGRN_CHUNK * 8, MIX), F32),
               pltpu.VMEM((t_max, MIX), F32)]
    if with_ctx:
        out_specs.append(pl.BlockSpec((1, l, MIX), lambda i: (i, 0, 0)))
        out_shape.append(jax.ShapeDtypeStruct((b, l, MIX), BF16))
    res = pl.pallas_call(
        functools.partial(_hgrn_kernel, n_lat=s // HGRN_CHUNK, n_ctx=l // HGRN_CHUNK,
                          with_ctx=with_ctx),
        grid=(b,),
        in_specs=[seq(s)] * 5 + [seq(l)] * 5 + [_const_spec((2, MIX)), _const_spec((1, MIX))],
        out_specs=out_specs,
        out_shape=out_shape,
        scratch_shapes=scratch,
        compiler_params=_cparams(1),
        name="hgrn",
    )(*lat, *ctx, lb, norm_g4)
    return (res[0], res[1]) if with_ctx else (res[0], None)


def _na_bias_table(rpb):
    n_h = rpb.shape[0]
    w, nr = GRID_W, 2 * NA_KH - 1
    side = (w - 1) - (NA_KW - 1)
    p = jnp.pad(rpb.astype(F32), ((0, 0), (0, 0), (side, side + 1)))
    skew = jnp.broadcast_to(p[:, :, None, :], (n_h, nr, w, 2 * w)).reshape(n_h, nr, 2 * w * w)
    skew = skew[:, :, :w * (2 * w - 1)].reshape(n_h, nr, w, 2 * w - 1)
    toep = skew[..., w - 1:]
    qc = np.arange(w)[:, None]
    kc = np.arange(w)[None, :]
    win = np.clip(qc - NA_KW // 2, 0, w - NA_KW)
    valid = (kc >= win) & (kc < win + NA_KW)
    toep = jnp.where(valid[None, None], toep, NEG_INF)
    two = jnp.concatenate([toep[:, :-1], toep[:, 1:]], axis=-1)
    two = two.reshape(n_h // 2, 2, nr - 1, w, 2 * w).transpose(0, 2, 1, 3, 4)
    return two.reshape(n_h // 2, nr - 1, 2 * w, 2 * w)


def _softmax_pv(s_list, v_list):
    m = functools.reduce(jnp.maximum, [jnp.max(s, axis=-1, keepdims=True) for s in s_list])
    p_list = [jnp.exp(s - m) for s in s_list]
    l = functools.reduce(jnp.add, [jnp.sum(p, axis=-1, keepdims=True) for p in p_list])
    o = functools.reduce(jnp.add, [_dot(p.astype(BF16), v) for p, v in zip(p_list, v_list)])
    return o * (1.0 / l)


NA_ROWS_PER_STEP = 8


def _na_kernel(q_ref, k_ref, v_ref, kc_ref, vc_ref, bias_ref, o_ref, *, rows):
    lane = lax.broadcasted_iota(jnp.int32, (1, 2 * NA_HEAD_DIM), 1)
    first = lane < NA_HEAD_DIM
    kc = kc_ref[0]
    vc = vc_ref[0]
    win = NA_KH * GRID_W
    nq = 2 * GRID_W

    def body(g, carry):
        qs, kws, vws, dis, qrows = [], [], [], [], []
        for j in range(NA_ROWS_PER_STEP):
            r = g * NA_ROWS_PER_STEP + j
            rs = jnp.clip(r - NA_KH // 2, 0, rows - NA_KH)
            dis.append(r - rs)
            qrows.append(pl.ds(pl.multiple_of(r * GRID_W, GRID_W), GRID_W))
            krows = pl.ds(pl.multiple_of(rs * GRID_W, GRID_W), win)
            q = q_ref[0, qrows[-1], :]
            qs += [jnp.where(first, q, jnp.zeros_like(q)), jnp.where(first, jnp.zeros_like(q), q)]
            kws.append(k_ref[0, krows, :])
            vws.append(v_ref[0, krows, :])
        qs = jnp.concatenate(qs, axis=0)
        sx = _dot_nt(qs, kc)
        pws, pxs, inv_ls = [], [], []
        for j in range(NA_ROWS_PER_STEP):
            ri0 = NA_KH - 1 - dis[j]
            bias = jnp.concatenate([bias_ref[0, ri0 + 2 * c] for c in range(NA_KH // 2)], axis=1)
            sw = _dot_nt(qs[j * nq:(j + 1) * nq], kws[j]) + bias
            sxj = sx[j * nq:(j + 1) * nq]
            m = jnp.maximum(jnp.max(sw, axis=-1, keepdims=True), jnp.max(sxj, axis=-1, keepdims=True))
            pw = jnp.exp(sw - m)
            px = jnp.exp(sxj - m)
            inv_ls.append(1.0 / (jnp.sum(pw, axis=-1, keepdims=True) + jnp.sum(px, axis=-1, keepdims=True)))
            pws.append(pw.astype(BF16))
            pxs.append(px.astype(BF16))
        ows = [_dot(pws[j], vws[j]) for j in range(NA_ROWS_PER_STEP)]
        oc = _dot(jnp.concatenate(pxs, axis=0), vc)
        for j in range(NA_ROWS_PER_STEP):
            o = (ows[j] + oc[j * nq:(j + 1) * nq]) * inv_ls[j]
            o_ref[0, qrows[j], :] = jnp.where(first, o[:GRID_W], o[GRID_W:]).astype(BF16)
        return carry

    lax.fori_loop(0, rows // NA_ROWS_PER_STEP, body, 0)


def _na_latent(q, k, v, kc, vc, bias):
    b, s, _ = q.shape
    l = kc.shape[1]
    rows = s // GRID_W
    assert s % GRID_W == 0 and rows >= NA_KH and rows % NA_ROWS_PER_STEP == 0
    hp = 2 * NA_HEAD_DIM

    def seq(t):
        return pl.BlockSpec((1, t, hp), lambda i, j: (i, 0, j))

    return pl.pallas_call(
        functools.partial(_na_kernel, rows=rows),
        grid=(b, NA_HEADS // 2),
        in_specs=[seq(s), seq(s), seq(s), seq(l), seq(l),
                  pl.BlockSpec((1, 2 * NA_KH - 2, 2 * GRID_W, 2 * GRID_W),
                               lambda i, j: (j, 0, 0, 0))],
        out_specs=seq(s),
        out_shape=jax.ShapeDtypeStruct((b, s, NA_WIDTH), BF16),
        compiler_params=_cparams(2),
        name="na_latent",
    )(q, k, v, kc, vc, bias)


def _na_ctx_kernel(q_ref, k_ref, v_ref, o_ref):
    lane = lax.broadcasted_iota(jnp.int32, (1, 2 * NA_HEAD_DIM), 1)
    first = lane < NA_HEAD_DIM
    q, k, v = q_ref[0], k_ref[0], v_ref[0]
    outs = []
    for h in range(2):
        qh = jnp.where(first if h == 0 else ~first, q, jnp.zeros_like(q))
        outs.append(_softmax_pv([_dot_nt(qh, k)], [v]))
    o_ref[0] = jnp.where(first, outs[0], outs[1]).astype(BF16)


def _na_context(q, k, v):
    b, l, _ = q.shape
    spec = pl.BlockSpec((1, l, 2 * NA_HEAD_DIM), lambda i, j: (i, 0, j))
    return pl.pallas_call(
        _na_ctx_kernel,
        grid=(b, NA_HEADS // 2),
        in_specs=[spec, spec, spec],
        out_specs=spec,
        out_shape=jax.ShapeDtypeStruct((b, l, NA_WIDTH), BF16),
        compiler_params=_cparams(2),
        name="na_context",
    )(q, k, v)


def _ffn_kernel(x_ref, f_ref, h_ref, n_ref, g1_ref, sh_ref, sc_ref, g2_ref, ng_ref, fg_ref,
                wo_ref, wg_ref, wu_ref, wd_ref, o_ref, *, final):
    y = (_dot(f_ref[...], wo_ref[0:MIX, :]) + _dot(h_ref[...], wo_ref[MIX:2 * MIX, :])
         + _dot(n_ref[...], wo_ref[2 * MIX:, :]))
    x1 = x_ref[...] + g1_ref[0] * y
    hn = x1 * lax.rsqrt(jnp.mean(x1 * x1, axis=-1, keepdims=True) + EPS) * ng_ref[...]
    hm = (hn * (1.0 + sc_ref[0]) + sh_ref[0]).astype(BF16)
    d_ff = wg_ref.shape[1]
    acc = jnp.zeros(x1.shape, F32)
    for j in range(d_ff // FF_CHUNK):
        cols = slice(j * FF_CHUNK, (j + 1) * FF_CHUNK)
        g = _dot(hm, wg_ref[:, cols])
        u = _dot(hm, wu_ref[:, cols])
        act = (g * _sigmoid(g) * u).astype(BF16)
        acc = acc + _dot(act, wd_ref[cols, :])
    x2 = x1 + g2_ref[0] * acc
    if final:
        x2 = x2 * lax.rsqrt(jnp.mean(x2 * x2, axis=-1, keepdims=True) + EPS) * fg_ref[...]
    o_ref[...] = x2


def _ffn(x2d, f, h, n, norm_g, final_g, modv, mod_base, rows_per_mod, wo, wg, wu, wd, tm, final):
    nt, d = x2d.shape
    d_ff = wg.shape[1]
    assert d_ff % FF_CHUNK == 0

    def mod_idx(j):
        if rows_per_mod is None:
            return lambda i: (mod_base * 6 + j, 0, 0)
        tpm = rows_per_mod // tm
        return lambda i: ((mod_base + i // tpm) * 6 + j, 0, 0)

    def tok(w):
        return pl.BlockSpec((tm, w), lambda i: (i, 0))

    def vec():
        return pl.BlockSpec((1, d), lambda i: (0, 0))

    return pl.pallas_call(
        functools.partial(_ffn_kernel, final=final),
        grid=(nt // tm,),
        in_specs=[tok(d), tok(MIX), tok(MIX), tok(NA_WIDTH),
                  pl.BlockSpec((1, 1, d), mod_idx(2)), pl.BlockSpec((1, 1, d), mod_idx(3)),
                  pl.BlockSpec((1, 1, d), mod_idx(4)), pl.BlockSpec((1, 1, d), mod_idx(5)),
                  vec(), vec(),
                  _const_spec(wo.shape), _const_spec(wg.shape), _const_spec(wu.shape),
                  _const_spec(wd.shape)],
        out_specs=tok(d),
        out_shape=jax.ShapeDtypeStruct((nt, d), F32),
        compiler_params=_cparams(1),
        name="outproj_ffn",
    )(x2d, f, h, n, modv, modv, modv, modv, norm_g.reshape(1, d), final_g.reshape(1, d),
      wo, wg, wu, wd)


def _token_tile(n):
    for tm in (512, 256, 128, 64, 32, 16, 8):
        if n % tm == 0:
            return tm
    raise ValueError(f"token count {n} is not a multiple of 8")


def kernel(x, c, ctx, c_ctx, w_mod, b_mod, norm1_g, w_in, fourier_w, hgrn_lb, hgrn_norm_g, na_rpb,
           w_out, norm2_g, w_ffn_gate, w_ffn_up, w_ffn_down, final_norm_g):
    b, s, d = x.shape
    l = ctx.shape[1]
    depth = w_mod.shape[0]
    assert b < MOD_ROWS and d == 2 * MIX + NA_WIDTH

    cc = jnp.zeros((MOD_ROWS, d), F32).at[:b].set(c).at[b].set(c_ctx)
    mod = _modulation(cc, w_mod, b_mod)
    modv = mod.reshape(depth * MOD_ROWS * 6, 1, d)

    lbp = jax.nn.softmax(hgrn_lb.astype(F32), axis=0)
    lower = jnp.cumsum(lbp, axis=0) - lbp[0:1]
    norm_g4 = jnp.tile(hgrn_norm_g.astype(F32), (1, HGRN_HEADS))[:, None, :]

    tab_s, tab_l = _dft_table(s), _dft_table(l)
    c64, s64 = _channel_dft_consts()
    eye_g = jnp.eye(FOURIER_GROUPS, dtype=F32)

    tm_s = _token_tile(s)
    tm_c = _token_tile(b * l)
    xl = x.reshape(b * s, d)
    xc = ctx.reshape(b * l, d)
    for li in range(depth):
        with_ctx = li < depth - 1
        final = li == depth - 1
        base = li * MOD_ROWS
        w_in_b = w_in[li].astype(BF16)
        ul = _inproj(xl, norm1_g[li], modv, base, s, w_in_b, tm_s)
        uc = _inproj(xc, norm1_g[li], modv, base + b, None, w_in_b, tm_c)
        ul = [a.reshape(b, s, -1) for a in ul]
        uc = [a.reshape(b, l, -1) for a in uc]

        w_bd = (eye_g[:, None, :, None] * fourier_w[li][:, :, None, :]).reshape(MIX, MIX)
        f_lat = _fourier(ul[0], tab_s, c64, s64, w_bd)
        h_lat, h_ctx = _hgrn(ul[1:6], uc[1:6], lower[li], norm_g4[li], with_ctx)
        n_lat = _na_latent(ul[6], ul[7], ul[8], uc[7], uc[8], _na_bias_table(na_rpb[li]))

        wo, wg = w_out[li].astype(BF16), w_ffn_gate[li].astype(BF16)
        wu, wd = w_ffn_up[li].astype(BF16), w_ffn_down[li].astype(BF16)
        if with_ctx:
            f_ctx = _fourier(uc[0], tab_l, c64, s64, w_bd)
            n_ctx = _na_context(uc[6], uc[7], uc[8])
            xc = _ffn(xc, f_ctx.reshape(b * l, -1), h_ctx.reshape(b * l, -1),
                      n_ctx.reshape(b * l, -1), norm2_g[li], final_norm_g, modv, base + b, None,
                      wo, wg, wu, wd, tm_c, False)
        xl = _ffn(xl, f_lat.reshape(b * s, -1), h_lat.reshape(b * s, -1), n_lat.reshape(b * s, -1),
                  norm2_g[li], final_norm_g, modv, base, s, wo, wg, wu, wd, tm_s, final)
    return xl.reshape(b, s, d)
```

```python
import functools

import jax
import jax.numpy as jnp
import numpy as np
from jax import lax
from jax.experimental import pallas as pl
from jax.experimental.pallas import tpu as pltpu

F32 = jnp.float32
BF16 = jnp.bfloat16

EPS = 1e-6
NEG_INF = -1e30
GRID_W = 64
FOURIER_GROUPS = 4
FOURIER_GD = 64
HGRN_DK = 64
HGRN_HEADS = 4
HGRN_CHUNK = 32
NA_HEAD_DIM = 64
NA_HEADS = 8
NA_KH = 8
NA_KW = 16
MIX = 256
NA_WIDTH = NA_HEADS * NA_HEAD_DIM
MOD_ROWS = 16
FF_CHUNK = 256
VMEM_LIMIT = 56 * 1024 * 1024


def _cparams(n_axes):
    return pltpu.CompilerParams(dimension_semantics=("arbitrary",) * n_axes,
                                vmem_limit_bytes=VMEM_LIMIT)


def _split2(a):
    hi = a.astype(BF16)
    lo = (a - hi.astype(F32)).astype(BF16)
    return hi, lo


def _dot(a, b):
    return jnp.dot(a, b, preferred_element_type=F32)


def _dot_nt(a, b):
    return lax.dot_general(a, b, (((1,), (1,)), ((), ())), preferred_element_type=F32)


def _dot_tn(a, b):
    return lax.dot_general(a, b, (((0,), (0,)), ((), ())), preferred_element_type=F32)


def _dot_hilo(a, b):
    ah, al = _split2(a)
    bh, bl = _split2(b)
    return _dot(ah, bh) + _dot(al, bh) + _dot(ah, bl)


def _sigmoid(z):
    return 1.0 / (1.0 + jnp.exp(-z))


def _const_spec(shape):
    nd = len(shape)
    return pl.BlockSpec(shape, lambda *_: (0,) * nd, pipeline_mode=pl.Buffered(1))


def _mod_kernel(c_ref, w_ref, b_ref, o_ref):
    c = c_ref[...]
    s = c * _sigmoid(c)
    o_ref[0] = _dot_hilo(s, w_ref[0]) + b_ref[0]


def _modulation(cc, w_mod, b_mod):
    depth, d, n = w_mod.shape
    tn = 1024
    return pl.pallas_call(
        _mod_kernel,
        grid=(depth, n // tn),
        in_specs=[pl.BlockSpec((MOD_ROWS, d), lambda l, j: (0, 0)),
                  pl.BlockSpec((1, d, tn), lambda l, j: (l, 0, j)),
                  pl.BlockSpec((1, 1, tn), lambda l, j: (l, 0, j))],
        out_specs=pl.BlockSpec((1, MOD_ROWS, tn), lambda l, j: (l, 0, j)),
        out_shape=jax.ShapeDtypeStruct((depth, MOD_ROWS, n), F32),
        compiler_params=_cparams(2),
        name="modulation",
    )(cc, w_mod, b_mod.reshape(depth, 1, n))


def _inproj_kernel(x_ref, g_ref, sh_ref, sc_ref, w_ref,
                   uf_ref, hq_ref, zf_ref, zb_ref, hi_ref, hg_ref, q_ref, k_ref, v_ref):
    x = x_ref[...]
    y = x * lax.rsqrt(jnp.mean(x * x, axis=-1, keepdims=True) + EPS) * g_ref[...]
    a = (y * (1.0 + sc_ref[0]) + sh_ref[0]).astype(BF16)
    col = 0
    for ref in (uf_ref, hq_ref, zf_ref, zb_ref, hi_ref, hg_ref):
        ref[...] = _dot(a, w_ref[:, col:col + MIX]).astype(ref.dtype)
        col += MIX
    scale = NA_HEAD_DIM ** -0.5
    q_ref[...] = (_dot(a, w_ref[:, col:col + NA_WIDTH]) * scale).astype(BF16)
    col += NA_WIDTH
    k_ref[...] = _dot(a, w_ref[:, col:col + NA_WIDTH]).astype(BF16)
    col += NA_WIDTH
    v_ref[...] = _dot(a, w_ref[:, col:col + NA_WIDTH]).astype(BF16)


def _inproj(x2d, norm_g, modv, mod_base, rows_per_mod, w_in_b, tm):
    n, d = x2d.shape
    in_w = w_in_b.shape[1]

    def mod_idx(j):
        if rows_per_mod is None:
            return lambda i: (mod_base * 6 + j, 0, 0)
        tpm = rows_per_mod // tm
        return lambda i: ((mod_base + i // tpm) * 6 + j, 0, 0)

    def tok(w):
        return pl.BlockSpec((tm, w), lambda i: (i, 0))

    outs = [(MIX, BF16), (MIX, BF16), (MIX, F32), (MIX, F32), (MIX, BF16), (MIX, BF16),
            (NA_WIDTH, BF16), (NA_WIDTH, BF16), (NA_WIDTH, BF16)]
    return pl.pallas_call(
        _inproj_kernel,
        grid=(n // tm,),
        in_specs=[tok(d),
                  pl.BlockSpec((1, d), lambda i: (0, 0)),
                  pl.BlockSpec((1, 1, d), mod_idx(0)),
                  pl.BlockSpec((1, 1, d), mod_idx(1)),
                  _const_spec((d, in_w))],
        out_specs=[tok(w) for w, _ in outs],
        out_shape=[jax.ShapeDtypeStruct((n, w), dt) for w, dt in outs],
        compiler_params=_cparams(1),
        name="inproj",
    )(x2d, norm_g.reshape(1, d), modv, modv, w_in_b)


def _dft_table(t_len):
    kb = min(64, t_len)
    assert t_len % kb == 0
    t = jnp.arange(t_len // 2, dtype=jnp.int32)[None, :]
    k_hi = jnp.arange(t_len // kb, dtype=jnp.int32)[:, None] * kb
    k_lo = jnp.arange(kb, dtype=jnp.int32)[:, None]
    w = 2.0 * np.pi / t_len
    ang_hi = ((k_hi * t) % t_len).astype(F32) * w
    ang_lo = ((k_lo * t) % t_len).astype(F32) * w
    ch, sh = jnp.cos(ang_hi)[:, None, :], jnp.sin(ang_hi)[:, None, :]
    cl, sl = jnp.cos(ang_lo)[None, :, :], jnp.sin(ang_lo)[None, :, :]
    cos = (ch * cl - sh * sl).reshape(t_len, t_len // 2)
    sin = (sh * cl + ch * sl).reshape(t_len, t_len // 2)
    return jnp.concatenate([cos, sin], axis=1).astype(BF16)


def _channel_dft_consts():
    cd = np.outer(np.arange(FOURIER_GD), np.arange(FOURIER_GD)) % FOURIER_GD
    ang = 2.0 * np.pi * cd / FOURIER_GD
    eye = np.eye(FOURIER_GROUPS)
    return (jnp.asarray(np.kron(eye, np.cos(ang)), F32),
            jnp.asarray(np.kron(eye, np.sin(ang)), F32))


def _fourier_kernel(u_ref, tab_ref, c64_ref, s64_ref, w_ref, o_ref, pq_ref, pm_ref, *, t_len, tm):
    half = t_len // 2
    bs = min(256, half)
    nb = half // bs
    bi = pl.program_id(1)

    @pl.when(pl.program_id(0) == 0)
    def _():
        norm = (FOURIER_GD * t_len) ** -0.5
        w = w_ref[...]
        a = (_dot_hilo(c64_ref[...], w) * norm).astype(BF16)
        b = (_dot_hilo(s64_ref[...], w) * (-norm)).astype(BF16)
        p_i = lax.broadcasted_iota(jnp.int32, (bs, bs), 0)
        q_i = lax.broadcasted_iota(jnp.int32, (bs, bs), 1)
        rev_shift = jnp.where(q_i == bs - p_i, 1.0, 0.0).astype(BF16)
        first_row = jnp.where(p_i + q_i == 0, 1.0, 0.0).astype(BF16)
        for i in range(nb):
            src = half + (nb - 1 - i) * bs
            r = _dot(rev_shift, u_ref[0, src:src + bs, :])
            if i > 0:
                r = r + _dot(first_row, u_ref[0, src + bs:src + 2 * bs, :])
            u = u_ref[0, i * bs:(i + 1) * bs, :].astype(F32)
            pq_ref[bi, i * bs:(i + 1) * bs, :] = _dot((u + r).astype(BF16), a).astype(BF16)
            pq_ref[bi, half + i * bs:half + (i + 1) * bs, :] = _dot((u - r).astype(BF16),
                                                                    b).astype(BF16)
        pm_ref[bi] = _dot(u_ref[0, half:half + 8, :], a)

    k = pl.program_id(0) * tm + lax.broadcasted_iota(jnp.int32, (tm, 1), 0)
    sign = (1 - 2 * (k & 1)).astype(F32)
    o_ref[0] = (_dot(tab_ref[...], pq_ref[bi]) + sign * pm_ref[bi, 0:1, :]).astype(BF16)


def _fourier(u, tab, c64, s64, w_bd):
    b, t_len, _ = u.shape
    assert t_len % 64 == 0
    tm = min(512, t_len)
    return pl.pallas_call(
        functools.partial(_fourier_kernel, t_len=t_len, tm=tm),
        grid=(t_len // tm, b),
        in_specs=[pl.BlockSpec((1, t_len, MIX), lambda j, i: (jnp.where(j == 0, i, b - 1), 0, 0)),
                  pl.BlockSpec((tm, t_len), lambda j, i: (j, 0)),
                  _const_spec((MIX, MIX)), _const_spec((MIX, MIX)), _const_spec((MIX, MIX))],
        out_specs=pl.BlockSpec((1, tm, MIX), lambda j, i: (i, j, 0)),
        out_shape=jax.ShapeDtypeStruct((b, t_len, MIX), BF16),
        scratch_shapes=[pltpu.VMEM((b, t_len, MIX), BF16), pltpu.VMEM((b, 8, MIX), F32)],
        compiler_params=_cparams(2),
        name="fourier",
    )(u, tab, c64, s64, w_bd)


def _hgrn_kernel(ql_ref, zfl_ref, zbl_ref, il_ref, gl_ref,
                 qc_ref, zfc_ref, zbc_ref, ic_ref, gc_ref, lb_ref, ng_ref,
                 *rest, n_lat, n_ctx, with_ctx):
    if with_ctx:
        ol_ref, oc_ref, st_ref, qd_ref, ks_ref, vt_ref, eal_ref, os_ref = rest
    else:
        ol_ref, st_ref, qd_ref, ks_ref, vt_ref, eal_ref, os_ref = rest
        oc_ref = None
    C = HGRN_CHUNK
    W = MIX
    lane_head = lax.broadcasted_iota(jnp.int32, (1, W), 1) // HGRN_DK
    head_mask = [lane_head == h for h in range(HGRN_HEADS)]
    bd = (lax.broadcasted_iota(jnp.int32, (W, W), 0) // HGRN_DK
          == lax.broadcasted_iota(jnp.int32, (W, W), 1) // HGRN_DK)
    bd_ones = jnp.where(bd, 1.0, 0.0).astype(BF16)

    def chunk_rows(n):
        return pl.ds(pl.multiple_of(n * C, C), C)

    def stack_heads(x):
        return jnp.concatenate([jnp.where(m, x, jnp.zeros_like(x)) for m in head_mask], axis=0)

    def group_chunks(n_chunks):
        return 4 if n_chunks % 4 == 0 else 2

    def phase_a(q_ref, zf_ref, zb_ref, i_ref, n_chunks, need_o):
        gc = group_chunks(n_chunks)
        gt = gc * C
        t_i = lax.broadcasted_iota(jnp.int32, (gt, gt), 0)
        s_i = lax.broadcasted_iota(jnp.int32, (gt, gt), 1)
        same = (t_i // C) == (s_i // C)
        allow = (same & (s_i <= t_i), same & (s_i >= t_i))
        allow_b16 = tuple(jnp.where(m, 1.0, 0.0).astype(BF16) for m in allow)
        t4 = lax.broadcasted_iota(jnp.int32, (gt, HGRN_HEADS * gt), 0)
        s4 = lax.broadcasted_iota(jnp.int32, (gt, HGRN_HEADS * gt), 1) % gt
        same4 = (t4 // C) == (s4 // C)
        allow4 = (same4 & (s4 <= t4), same4 & (s4 >= t4))

        def body(g, carry):
            rows = pl.ds(pl.multiple_of(g * gt, gt), gt)
            v = i_ref[0, rows, :]
            vs = stack_heads(v) if need_o else None
            for c in range(gc):
                vt_ref[g * gc + c] = stack_heads(v[c * C:(c + 1) * C, :]).T
            fs, parts = [], []
            for d, z_ref in ((0, zf_ref), (1, zb_ref)):
                lb = lb_ref[d:d + 1, :]
                f = lb + (1.0 - lb) * _sigmoid(z_ref[0, rows, :])
                lf = jnp.log(f)
                h1 = lf.astype(BF16)
                r1 = lf - h1.astype(F32)
                h2 = r1.astype(BF16)
                h3 = (r1 - h2.astype(F32)).astype(BF16)
                fs.append(f)
                parts.append((h1, h2, h3))
            a_s = [_dot(allow_b16[d], parts[d][0]) + _dot(allow_b16[d], parts[d][1])
                   + _dot(allow_b16[d], parts[d][2]) for d in (0, 1)]
            scs = []
            for d in (0, 1):
                a = a_s[d]
                kd = (1.0 - fs[d]) * jnp.exp(-a)
                ends = [a[c * C + C - 1:c * C + C, :] if d == 0 else a[c * C:c * C + 1, :]
                        for c in range(gc)]
                eal = [jnp.exp(e) for e in ends]
                ks_ref[d, rows, :] = (kd * jnp.concatenate(
                    [jnp.broadcast_to(e, (C, W)) for e in eal], axis=0)).astype(BF16)
                eal_ref[d, pl.ds(pl.multiple_of(g * gc * 8, gc * 8), gc * 8), :] = jnp.concatenate(
                    [jnp.broadcast_to(e, (8, W)) for e in eal], axis=0)
                if need_o:
                    qdb = (q_ref[0, rows, :].astype(F32) * jnp.exp(a)).astype(BF16)
                    qd_ref[d, rows, :] = qdb
                    scs.append(jnp.where(allow4[d], _dot_nt(qdb, stack_heads(kd.astype(BF16))),
                                         0.0).astype(BF16))
            if need_o:
                os_ref[rows, :] = _dot(scs[0], vs) + _dot(scs[1], vs)
            return carry

        lax.fori_loop(0, n_chunks // gc, body, 0, unroll=2)

    def phase_b(i_ref, n_chunks, need_o):
        gc = group_chunks(n_chunks)
        pw = 2 * HGRN_DK
        pairs = [slice(p * pw, (p + 1) * pw) for p in range(HGRN_HEADS // 2)]

        def body(g, carry):
            idx = [[g * gc + c, n_chunks - 1 - (g * gc + c)] for c in range(gc)]

            def updates(c):
                out = []
                for d, m in enumerate(idx[c]):
                    vt = vt_ref[m]
                    kss = stack_heads(ks_ref[d, chunk_rows(m), :])
                    out.append([_dot(vt[s, :], kss[:, s]) for s in pairs])
                return out

            upd = {0: updates(0)}
            inters = []
            for c in range(gc):
                if c + 1 < gc:
                    upd[c + 1] = updates(c + 1)
                for d in (0, 1):
                    m = idx[c][d]
                    rows = chunk_rows(m)
                    eal = eal_ref[d, pl.ds(pl.multiple_of(m * 8, 8), 8), :][0:1, :]
                    sts = [st_ref[d, p] for p in range(len(pairs))]
                    if need_o:
                        qd = qd_ref[d, rows, :]
                        inter = [_dot_nt(qd[:, s], st.astype(BF16)) for s, st in zip(pairs, sts)]
                        inters.append((rows, jnp.concatenate(inter, axis=1)))
                    for p, s in enumerate(pairs):
                        st_ref[d, p] = sts[p] * eal[:, s] + upd[c][d][p]
            for rows, inter in inters:
                os_ref[rows, :] = os_ref[rows, :] + inter
            return carry

        lax.fori_loop(0, n_chunks // gc, body, 0, unroll=2)

    def phase_c(g_ref, o_ref, n_rows):
        rt = next(t for t in (256, 128, 64) if n_rows % t == 0)

        def body(t, carry):
            rows = pl.ds(pl.multiple_of(t * rt, rt), rt)
            o = os_ref[rows, :]
            hi, lo = _split2(o * o)
            ms = (_dot(hi, bd_ones) + _dot(lo, bd_ones)) * (1.0 / HGRN_DK)
            y = o * lax.rsqrt(ms + EPS) * ng_ref[...]
            g = g_ref[0, rows, :].astype(F32)
            o_ref[0, rows, :] = (y * (g * _sigmoid(g))).astype(BF16)
            return carry

        lax.fori_loop(0, n_rows // rt, body, 0)

    st_ref[...] = jnp.zeros_like(st_ref)
    phase_a(qc_ref, zfc_ref, zbc_ref, ic_ref, n_ctx, with_ctx)
    phase_b(ic_ref, n_ctx, with_ctx)
    if with_ctx:
        phase_c(gc_ref, oc_ref, n_ctx * C)
    phase_a(ql_ref, zfl_ref, zbl_ref, il_ref, n_lat, True)
    phase_b(il_ref, n_lat, True)
    phase_c(gl_ref, ol_ref, n_lat * C)


def _hgrn(lat, ctx, lb, norm_g4, with_ctx):
    b, s, _ = lat[0].shape
    l = ctx[0].shape[1]
    assert s % (2 * HGRN_CHUNK) == 0 and l % (2 * HGRN_CHUNK) == 0

    def seq(t):
        return pl.BlockSpec((1, t, MIX), lambda i: (i, 0, 0), pipeline_mode=pl.Buffered(1))

    out_specs = [pl.BlockSpec((1, s, MIX), lambda i: (i, 0, 0))]
    out_shape = [jax.ShapeDtypeStruct((b, s, MIX), BF16)]
    t_max = max(s, l)
    scratch = [pltpu.VMEM((2, HGRN_HEADS // 2, 2 * HGRN_DK, 2 * HGRN_DK), F32),
               pltpu.VMEM((2, t_max, MIX), BF16),
               pltpu.VMEM((2, t_max, MIX), BF16),
               pltpu.VMEM((t_max // HGRN_CHUNK, MIX, HGRN_HEADS * HGRN_CHUNK), BF16),
               pltpu.VMEM((2, t_max // HGRN_CHUNK * 8, MIX), F32),
               pltpu.VMEM((t_max, MIX), F32)]
    if with_ctx:
        out_specs.append(pl.BlockSpec((1, l, MIX), lambda i: (i, 0, 0)))
        out_shape.append(jax.ShapeDtypeStruct((b, l, MIX), BF16))
    res = pl.pallas_call(
        functools.partial(_hgrn_kernel, n_lat=s // HGRN_CHUNK, n_ctx=l // HGRN_CHUNK,
                          with_ctx=with_ctx),
        grid=(b,),
        in_specs=[seq(s)] * 5 + [seq(l)] * 5 + [_const_spec((2, MIX)), _const_spec((1, MIX))],
        out_specs=out_specs,
        out_shape=out_shape,
        scratch_shapes=scratch,
        compiler_params=_cparams(1),
        name="hgrn",
    )(*lat, *ctx, lb, norm_g4)
    return (res[0], res[1]) if with_ctx else (res[0], None)


def _na_bias_table(rpb):
    n_h = rpb.shape[0]
    w, nr = GRID_W, 2 * NA_KH - 1
    side = (w - 1) - (NA_KW - 1)
    p = jnp.pad(rpb.astype(F32), ((0, 0), (0, 0), (side, side + 1)))
    skew = jnp.broadcast_to(p[:, :, None, :], (n_h, nr, w, 2 * w)).reshape(n_h, nr, 2 * w * w)
    skew = skew[:, :, :w * (2 * w - 1)].reshape(n_h, nr, w, 2 * w - 1)
    toep = skew[..., w - 1:]
    qc = np.arange(w)[:, None]
    kc = np.arange(w)[None, :]
    win = np.clip(qc - NA_KW // 2, 0, w - NA_KW)
    valid = (kc >= win) & (kc < win + NA_KW)
    toep = jnp.where(valid[None, None], toep, NEG_INF)
    two = jnp.concatenate([toep[:, :-1], toep[:, 1:]], axis=-1)
    two = two.reshape(n_h // 2, 2, nr - 1, w, 2 * w).transpose(0, 2, 1, 3, 4)
    return two.reshape(n_h // 2, nr - 1, 2 * w, 2 * w)


def _softmax_pv(s_list, v_list):
    m = functools.reduce(jnp.maximum, [jnp.max(s, axis=-1, keepdims=True) for s in s_list])
    p_list = [jnp.exp(s - m) for s in s_list]
    l = functools.reduce(jnp.add, [jnp.sum(p, axis=-1, keepdims=True) for p in p_list])
    o = functools.reduce(jnp.add, [_dot(p.astype(BF16), v) for p, v in zip(p_list, v_list)])
    return o * (1.0 / l)


NA_ROWS_PER_STEP = 16


def _na_kernel(q_ref, k_ref, v_ref, kc_ref, vc_ref, bias_ref, o_ref, *, rows):
    lane = lax.broadcasted_iota(jnp.int32, (1, 2 * NA_HEAD_DIM), 1)
    first = lane < NA_HEAD_DIM
    kc = kc_ref[0]
    vc = vc_ref[0]
    win = NA_KH * GRID_W
    nq = 2 * GRID_W

    def body(g, carry):
        qs, kws, vws, dis, qrows = [], [], [], [], []
        for j in range(NA_ROWS_PER_STEP):
            r = g * NA_ROWS_PER_STEP + j
            rs = jnp.clip(r - NA_KH // 2, 0, rows - NA_KH)
            dis.append(r - rs)
            qrows.append(pl.ds(pl.multiple_of(r * GRID_W, GRID_W), GRID_W))
            krows = pl.ds(pl.multiple_of(rs * GRID_W, GRID_W), win)
            q = q_ref[0, qrows[-1], :]
            qs += [jnp.where(first, q, jnp.zeros_like(q)), jnp.where(first, jnp.zeros_like(q), q)]
            kws.append(k_ref[0, krows, :])
            vws.append(v_ref[0, krows, :])
        qs = jnp.concatenate(qs, axis=0)
        sx = _dot_nt(qs, kc)
        pws, pxs, inv_ls = [], [], []
        for j in range(NA_ROWS_PER_STEP):
            ri0 = NA_KH - 1 - dis[j]
            bias = jnp.concatenate([bias_ref[0, ri0 + 2 * c] for c in range(NA_KH // 2)], axis=1)
            sw = _dot_nt(qs[j * nq:(j + 1) * nq], kws[j]) + bias
            sxj = sx[j * nq:(j + 1) * nq]
            m = jnp.maximum(jnp.max(sw, axis=-1, keepdims=True), jnp.max(sxj, axis=-1, keepdims=True))
            pw = jnp.exp(sw - m)
            px = jnp.exp(sxj - m)
            inv_ls.append(1.0 / (jnp.sum(pw, axis=-1, keepdims=True) + jnp.sum(px, axis=-1, keepdims=True)))
            pws.append(pw.astype(BF16))
            pxs.append(px.astype(BF16))
        ows = [_dot(pws[j], vws[j]) for j in range(NA_ROWS_PER_STEP)]
        oc = _dot(jnp.concatenate(pxs, axis=0), vc)
        for j in range(NA_ROWS_PER_STEP):
            o = (ows[j] + oc[j * nq:(j + 1) * nq]) * inv_ls[j]
            o_ref[0, qrows[j], :] = jnp.where(first, o[:GRID_W], o[GRID_W:]).astype(BF16)
        return carry

    lax.fori_loop(0, rows // NA_ROWS_PER_STEP, body, 0)


def _na_latent(q, k, v, kc, vc, bias):
    b, s, _ = q.shape
    l = kc.shape[1]
    rows = s // GRID_W
    assert s % GRID_W == 0 and rows >= NA_KH and rows % NA_ROWS_PER_STEP == 0
    hp = 2 * NA_HEAD_DIM

    def seq(t):
        return pl.BlockSpec((1, t, hp), lambda i, j: (i, 0, j))

    return pl.pallas_call(
        functools.partial(_na_kernel, rows=rows),
        grid=(b, NA_HEADS // 2),
        in_specs=[seq(s), seq(s), seq(s), seq(l), seq(l),
                  pl.BlockSpec((1, 2 * NA_KH - 2, 2 * GRID_W, 2 * GRID_W),
                               lambda i, j: (j, 0, 0, 0))],
        out_specs=seq(s),
        out_shape=jax.ShapeDtypeStruct((b, s, NA_WIDTH), BF16),
        compiler_params=_cparams(2),
        name="na_latent",
    )(q, k, v, kc, vc, bias)


def _na_ctx_kernel(q_ref, k_ref, v_ref, o_ref):
    lane = lax.broadcasted_iota(jnp.int32, (1, 2 * NA_HEAD_DIM), 1)
    first = lane < NA_HEAD_DIM
    for p in range(NA_HEADS // 2):
        cols = slice(p * 2 * NA_HEAD_DIM, (p + 1) * 2 * NA_HEAD_DIM)
        q, k, v = q_ref[0, :, cols], k_ref[0, :, cols], v_ref[0, :, cols]
        outs = []
        for h in range(2):
            qh = jnp.where(first if h == 0 else ~first, q, jnp.zeros_like(q))
            outs.append(_softmax_pv([_dot_nt(qh, k)], [v]))
        o_ref[0, :, cols] = jnp.where(first, outs[0], outs[1]).astype(BF16)


def _na_context(q, k, v):
    b, l, _ = q.shape
    spec = pl.BlockSpec((1, l, NA_WIDTH), lambda i: (i, 0, 0))
    return pl.pallas_call(
        _na_ctx_kernel,
        grid=(b,),
        in_specs=[spec, spec, spec],
        out_specs=spec,
        out_shape=jax.ShapeDtypeStruct((b, l, NA_WIDTH), BF16),
        compiler_params=_cparams(1),
        name="na_context",
    )(q, k, v)


def _ffn_kernel(x_ref, f_ref, h_ref, n_ref, g1_ref, sh_ref, sc_ref, g2_ref, ng_ref, fg_ref,
                wo_ref, wg_ref, wu_ref, wd_ref, o_ref, *, final):
    y = (_dot(f_ref[...], wo_ref[0:MIX, :]) + _dot(h_ref[...], wo_ref[MIX:2 * MIX, :])
         + _dot(n_ref[...], wo_ref[2 * MIX:, :]))
    x1 = x_ref[...] + g1_ref[0] * y
    hn = x1 * lax.rsqrt(jnp.mean(x1 * x1, axis=-1, keepdims=True) + EPS) * ng_ref[...]
    hm = (hn * (1.0 + sc_ref[0]) + sh_ref[0]).astype(BF16)
    d_ff = wg_ref.shape[1]
    acc = jnp.zeros(x1.shape, F32)
    for j in range(d_ff // FF_CHUNK):
        cols = slice(j * FF_CHUNK, (j + 1) * FF_CHUNK)
        g = _dot(hm, wg_ref[:, cols])
        u = _dot(hm, wu_ref[:, cols])
        act = (g * _sigmoid(g) * u).astype(BF16)
        acc = acc + _dot(act, wd_ref[cols, :])
    x2 = x1 + g2_ref[0] * acc
    if final:
        x2 = x2 * lax.rsqrt(jnp.mean(x2 * x2, axis=-1, keepdims=True) + EPS) * fg_ref[...]
    o_ref[...] = x2


def _ffn(x2d, f, h, n, norm_g, final_g, modv, mod_base, rows_per_mod, wo, wg, wu, wd, tm, final):
    nt, d = x2d.shape
    d_ff = wg.shape[1]
    assert d_ff % FF_CHUNK == 0

    def mod_idx(j):
        if rows_per_mod is None:
            return lambda i: (mod_base * 6 + j, 0, 0)
        tpm = rows_per_mod // tm
        return lambda i: ((mod_base + i // tpm) * 6 + j, 0, 0)

    def tok(w):
        return pl.BlockSpec((tm, w), lambda i: (i, 0))

    def vec():
        return pl.BlockSpec((1, d), lambda i: (0, 0))

    return pl.pallas_call(
        functools.partial(_ffn_kernel, final=final),
        grid=(nt // tm,),
        in_specs=[tok(d), tok(MIX), tok(MIX), tok(NA_WIDTH),
                  pl.BlockSpec((1, 1, d), mod_idx(2)), pl.BlockSpec((1, 1, d), mod_idx(3)),
                  pl.BlockSpec((1, 1, d), mod_idx(4)), pl.BlockSpec((1, 1, d), mod_idx(5)),
                  vec(), vec(),
                  _const_spec(wo.shape), _const_spec(wg.shape), _const_spec(wu.shape),
                  _const_spec(wd.shape)],
        out_specs=tok(d),
        out_shape=jax.ShapeDtypeStruct((nt, d), F32),
        compiler_params=_cparams(1),
        name="outproj_ffn",
    )(x2d, f, h, n, modv, modv, modv, modv, norm_g.reshape(1, d), final_g.reshape(1, d),
      wo, wg, wu, wd)


def _token_tile(n):
    for tm in (512, 256, 128, 64, 32, 16, 8):
        if n % tm == 0:
            return tm
    raise ValueError(f"token count {n} is not a multiple of 8")


def kernel(x, c, ctx, c_ctx, w_mod, b_mod, norm1_g, w_in, fourier_w, hgrn_lb, hgrn_norm_g, na_rpb,
           w_out, norm2_g, w_ffn_gate, w_ffn_up, w_ffn_down, final_norm_g):
    b, s, d = x.shape
    l = ctx.shape[1]
    depth = w_mod.shape[0]
    assert b < MOD_ROWS and d == 2 * MIX + NA_WIDTH

    cc = jnp.zeros((MOD_ROWS, d), F32).at[:b].set(c).at[b].set(c_ctx)
    mod = _modulation(cc, w_mod, b_mod)
    modv = mod.reshape(depth * MOD_ROWS * 6, 1, d)

    lbp = jax.nn.softmax(hgrn_lb.astype(F32), axis=0)
    lower = jnp.cumsum(lbp, axis=0) - lbp[0:1]
    norm_g4 = jnp.tile(hgrn_norm_g.astype(F32), (1, HGRN_HEADS))[:, None, :]

    tab_s, tab_l = _dft_table(s), _dft_table(l)
    c64, s64 = _channel_dft_consts()
    eye_g = jnp.eye(FOURIER_GROUPS, dtype=F32)

    tm_s = _token_tile(s)
    tm_c = _token_tile(b * l)
    xl = x.reshape(b * s, d)
    xc = ctx.reshape(b * l, d)
    for li in range(depth):
        with_ctx = li < depth - 1
        final = li == depth - 1
        base = li * MOD_ROWS
        w_in_b = w_in[li].astype(BF16)
        ul = _inproj(xl, norm1_g[li], modv, base, s, w_in_b, tm_s)
        uc = _inproj(xc, norm1_g[li], modv, base + b, None, w_in_b, tm_c)
        ul = [a.reshape(b, s, -1) for a in ul]
        uc = [a.reshape(b, l, -1) for a in uc]

        w_bd = (eye_g[:, None, :, None] * fourier_w[li][:, :, None, :]).reshape(MIX, MIX)
        f_lat = _fourier(ul[0], tab_s, c64, s64, w_bd)
        h_lat, h_ctx = _hgrn(ul[1:6], uc[1:6], lower[li], norm_g4[li], with_ctx)
        n_lat = _na_latent(ul[6], ul[7], ul[8], uc[7], uc[8], _na_bias_table(na_rpb[li]))

        wo, wg = w_out[li].astype(BF16), w_ffn_gate[li].astype(BF16)
        wu, wd = w_ffn_up[li].astype(BF16), w_ffn_down[li].astype(BF16)
        if with_ctx:
            f_ctx = _fourier(uc[0], tab_l, c64, s64, w_bd)
            n_ctx = _na_context(uc[6], uc[7], uc[8])
            xc = _ffn(xc, f_ctx.reshape(b * l, -1), h_ctx.reshape(b * l, -1),
                      n_ctx.reshape(b * l, -1), norm2_g[li], final_norm_g, modv, base + b, None,
                      wo, wg, wu, wd, tm_c, False)
        xl = _ffn(xl, f_lat.reshape(b * s, -1), h_lat.reshape(b * s, -1), n_lat.reshape(b * s, -1),
                  norm2_g[li], final_norm_g, modv, base, s, wo, wg, wu, wd, tm_s, final)
    return xl.reshape(b, s, d)
```

```python
import functools

import jax
import jax.numpy as jnp
import numpy as np
from jax import lax
from jax.experimental import pallas as pl
from jax.experimental.pallas import tpu as pltpu

F32 = jnp.float32
BF16 = jnp.bfloat16

EPS = 1e-6
NEG_INF = -1e30
GRID_W = 64
FOURIER_GROUPS = 4
FOURIER_GD = 64
HGRN_DK = 64
HGRN_HEADS = 4
HGRN_CHUNK = 32
NA_HEAD_DIM = 64
NA_HEADS = 8
NA_KH = 8
NA_KW = 16
MIX = 256
NA_WIDTH = NA_HEADS * NA_HEAD_DIM
MOD_ROWS = 16
FF_CHUNK = 256
VMEM_LIMIT = 56 * 1024 * 1024


def _cparams(n_axes):
    return pltpu.CompilerParams(dimension_semantics=("arbitrary",) * n_axes,
                                vmem_limit_bytes=VMEM_LIMIT)


def _split2(a):
    hi = a.astype(BF16)
    lo = (a - hi.astype(F32)).astype(BF16)
    return hi, lo


def _dot(a, b):
    return jnp.dot(a, b, preferred_element_type=F32)


def _dot_nt(a, b):
    return lax.dot_general(a, b, (((1,), (1,)), ((), ())), preferred_element_type=F32)


def _dot_tn(a, b):
    return lax.dot_general(a, b, (((0,), (0,)), ((), ())), preferred_element_type=F32)


def _dot_hilo(a, b):
    ah, al = _split2(a)
    bh, bl = _split2(b)
    return _dot(ah, bh) + _dot(al, bh) + _dot(ah, bl)


def _sigmoid(z):
    return 1.0 / (1.0 + jnp.exp(-z))


def _const_spec(shape):
    nd = len(shape)
    return pl.BlockSpec(shape, lambda *_: (0,) * nd, pipeline_mode=pl.Buffered(1))


def _mod_kernel(c_ref, w_ref, b_ref, o_ref):
    c = c_ref[...]
    s = c * _sigmoid(c)
    o_ref[0] = _dot_hilo(s, w_ref[0]) + b_ref[0]


def _modulation(cc, w_mod, b_mod):
    depth, d, n = w_mod.shape
    tn = 1024
    return pl.pallas_call(
        _mod_kernel,
        grid=(depth, n // tn),
        in_specs=[pl.BlockSpec((MOD_ROWS, d), lambda l, j: (0, 0)),
                  pl.BlockSpec((1, d, tn), lambda l, j: (l, 0, j)),
                  pl.BlockSpec((1, 1, tn), lambda l, j: (l, 0, j))],
        out_specs=pl.BlockSpec((1, MOD_ROWS, tn), lambda l, j: (l, 0, j)),
        out_shape=jax.ShapeDtypeStruct((depth, MOD_ROWS, n), F32),
        compiler_params=_cparams(2),
        name="modulation",
    )(cc, w_mod, b_mod.reshape(depth, 1, n))


def _inproj_kernel(x_ref, g_ref, sh_ref, sc_ref, w_ref,
                   uf_ref, hq_ref, zf_ref, zb_ref, hi_ref, hg_ref, q_ref, k_ref, v_ref):
    x = x_ref[...]
    y = x * lax.rsqrt(jnp.mean(x * x, axis=-1, keepdims=True) + EPS) * g_ref[...]
    a = (y * (1.0 + sc_ref[0]) + sh_ref[0]).astype(BF16)
    col = 0
    for ref in (uf_ref, hq_ref, zf_ref, zb_ref, hi_ref, hg_ref):
        ref[...] = _dot(a, w_ref[:, col:col + MIX]).astype(ref.dtype)
        col += MIX
    scale = NA_HEAD_DIM ** -0.5
    q_ref[...] = (_dot(a, w_ref[:, col:col + NA_WIDTH]) * scale).astype(BF16)
    col += NA_WIDTH
    k_ref[...] = _dot(a, w_ref[:, col:col + NA_WIDTH]).astype(BF16)
    col += NA_WIDTH
    v_ref[...] = _dot(a, w_ref[:, col:col + NA_WIDTH]).astype(BF16)


def _inproj(x2d, norm_g, modv, mod_base, rows_per_mod, w_in_b, tm):
    n, d = x2d.shape
    in_w = w_in_b.shape[1]

    def mod_idx(j):
        if rows_per_mod is None:
            return lambda i: (mod_base * 6 + j, 0, 0)
        tpm = rows_per_mod // tm
        return lambda i: ((mod_base + i // tpm) * 6 + j, 0, 0)

    def tok(w):
        return pl.BlockSpec((tm, w), lambda i: (i, 0))

    outs = [(MIX, BF16), (MIX, BF16), (MIX, F32), (MIX, F32), (MIX, BF16), (MIX, BF16),
            (NA_WIDTH, BF16), (NA_WIDTH, BF16), (NA_WIDTH, BF16)]
    return pl.pallas_call(
        _inproj_kernel,
        grid=(n // tm,),
        in_specs=[tok(d),
                  pl.BlockSpec((1, d), lambda i: (0, 0)),
                  pl.BlockSpec((1, 1, d), mod_idx(0)),
                  pl.BlockSpec((1, 1, d), mod_idx(1)),
                  _const_spec((d, in_w))],
        out_specs=[tok(w) for w, _ in outs],
        out_shape=[jax.ShapeDtypeStruct((n, w), dt) for w, dt in outs],
        compiler_params=_cparams(1),
        name="inproj",
    )(x2d, norm_g.reshape(1, d), modv, modv, w_in_b)


def _dft_table(t_len):
    kb = min(64, t_len)
    assert t_len % kb == 0
    t = jnp.arange(t_len // 2, dtype=jnp.int32)[None, :]
    k_hi = jnp.arange(t_len // kb, dtype=jnp.int32)[:, None] * kb
    k_lo = jnp.arange(kb, dtype=jnp.int32)[:, None]
    w = 2.0 * np.pi / t_len
    ang_hi = ((k_hi * t) % t_len).astype(F32) * w
    ang_lo = ((k_lo * t) % t_len).astype(F32) * w
    ch, sh = jnp.cos(ang_hi)[:, None, :], jnp.sin(ang_hi)[:, None, :]
    cl, sl = jnp.cos(ang_lo)[None, :, :], jnp.sin(ang_lo)[None, :, :]
    cos = (ch * cl - sh * sl).reshape(t_len, t_len // 2)
    sin = (sh * cl + ch * sl).reshape(t_len, t_len // 2)
    return jnp.concatenate([cos, sin], axis=1).astype(BF16)


def _channel_dft_consts():
    cd = np.outer(np.arange(FOURIER_GD), np.arange(FOURIER_GD)) % FOURIER_GD
    ang = 2.0 * np.pi * cd / FOURIER_GD
    eye = np.eye(FOURIER_GROUPS)
    return (jnp.asarray(np.kron(eye, np.cos(ang)), F32),
            jnp.asarray(np.kron(eye, np.sin(ang)), F32))


def _fourier_kernel(u_ref, tab_ref, c64_ref, s64_ref, w_ref, o_ref, pq_ref, pm_ref, *, t_len, tm):
    half = t_len // 2
    bs = min(256, half)
    nb = half // bs
    bi = pl.program_id(1)

    @pl.when(pl.program_id(0) == 0)
    def _():
        norm = (FOURIER_GD * t_len) ** -0.5
        w = w_ref[...]
        a = (_dot_hilo(c64_ref[...], w) * norm).astype(BF16)
        b = (_dot_hilo(s64_ref[...], w) * (-norm)).astype(BF16)
        p_i = lax.broadcasted_iota(jnp.int32, (bs, bs), 0)
        q_i = lax.broadcasted_iota(jnp.int32, (bs, bs), 1)
        rev_shift = jnp.where(q_i == bs - p_i, 1.0, 0.0).astype(BF16)
        first_row = jnp.where(p_i + q_i == 0, 1.0, 0.0).astype(BF16)
        for i in range(nb):
            src = half + (nb - 1 - i) * bs
            r = _dot(rev_shift, u_ref[0, src:src + bs, :])
            if i > 0:
                r = r + _dot(first_row, u_ref[0, src + bs:src + 2 * bs, :])
            u = u_ref[0, i * bs:(i + 1) * bs, :].astype(F32)
            pq_ref[bi, i * bs:(i + 1) * bs, :] = _dot((u + r).astype(BF16), a).astype(BF16)
            pq_ref[bi, half + i * bs:half + (i + 1) * bs, :] = _dot((u - r).astype(BF16),
                                                                    b).astype(BF16)
        pm_ref[bi] = _dot(u_ref[0, half:half + 8, :], a)

    k = pl.program_id(0) * tm + lax.broadcasted_iota(jnp.int32, (tm, 1), 0)
    sign = (1 - 2 * (k & 1)).astype(F32)
    o_ref[0] = (_dot(tab_ref[...], pq_ref[bi]) + sign * pm_ref[bi, 0:1, :]).astype(BF16)


def _fourier(u, tab, c64, s64, w_bd):
    b, t_len, _ = u.shape
    assert t_len % 64 == 0
    tm = min(512, t_len)
    return pl.pallas_call(
        functools.partial(_fourier_kernel, t_len=t_len, tm=tm),
        grid=(t_len // tm, b),
        in_specs=[pl.BlockSpec((1, t_len, MIX), lambda j, i: (jnp.where(j == 0, i, b - 1), 0, 0)),
                  pl.BlockSpec((tm, t_len), lambda j, i: (j, 0)),
                  _const_spec((MIX, MIX)), _const_spec((MIX, MIX)), _const_spec((MIX, MIX))],
        out_specs=pl.BlockSpec((1, tm, MIX), lambda j, i: (i, j, 0)),
        out_shape=jax.ShapeDtypeStruct((b, t_len, MIX), BF16),
        scratch_shapes=[pltpu.VMEM((b, t_len, MIX), BF16), pltpu.VMEM((b, 8, MIX), F32)],
        compiler_params=_cparams(2),
        name="fourier",
    )(u, tab, c64, s64, w_bd)


def _hgrn_kernel(ql_ref, zfl_ref, zbl_ref, il_ref, gl_ref,
                 qc_ref, zfc_ref, zbc_ref, ic_ref, gc_ref, lb_ref, ng_ref,
                 *rest, n_lat, n_ctx, with_ctx):
    if with_ctx:
        ol_ref, oc_ref, st_ref, qd_ref, ks_ref, vt_ref, eal_ref, os_ref = rest
    else:
        ol_ref, st_ref, qd_ref, ks_ref, vt_ref, eal_ref, os_ref = rest
        oc_ref = None
    C = HGRN_CHUNK
    P2 = 2 * C
    W = MIX
    lane_head =lax.broadcasted_iota(jnp.int32, (1, W), 1) // HGRN_DK
    head_mask = [lane_head == h for h in range(HGRN_HEADS)]
    bd = (lax.broadcasted_iota(jnp.int32, (W, W), 0) // HGRN_DK
          == lax.broadcasted_iota(jnp.int32, (W, W), 1) // HGRN_DK)
    bd_ones = jnp.where(bd, 1.0, 0.0).astype(BF16)

    def stack_heads(x):
        return jnp.concatenate([jnp.where(m, x, jnp.zeros_like(x)) for m in head_mask], axis=0)

    def group_chunks(n_chunks):
        return 4 if n_chunks % 4 == 0 else 2

    def phase_a(q_ref, zf_ref, zb_ref, i_ref, n_chunks, need_o):
        gc = group_chunks(n_chunks)
        gt = gc * C
        t_i = lax.broadcasted_iota(jnp.int32, (gt, gt), 0)
        s_i = lax.broadcasted_iota(jnp.int32, (gt, gt), 1)
        same = (t_i // C) == (s_i // C)
        allow = (same & (s_i <= t_i), same & (s_i >= t_i))
        allow_b16 = tuple(jnp.where(m, 1.0, 0.0).astype(BF16) for m in allow)
        t4 = lax.broadcasted_iota(jnp.int32, (gt, HGRN_HEADS * gt), 0)
        s4 = lax.broadcasted_iota(jnp.int32, (gt, HGRN_HEADS * gt), 1) % gt
        same4 = (t4 // C) == (s4 // C)
        allow4 = (same4 & (s4 <= t4), same4 & (s4 >= t4))
        pair4 = (t4 // P2) == (s4 // P2)
        cross4 = (pair4 & (t4 // C == s4 // C + 1), pair4 & (t4 // C + 1 == s4 // C))
        ones_row = jnp.ones((C, W), F32)

        def body(g, carry):
            rows = pl.ds(pl.multiple_of(g * gt, gt), gt)
            v = i_ref[0, rows, :]
            vs = stack_heads(v) if need_o else None
            for j in range(gc // 2):
                vt_ref[g * (gc // 2) + j] = v[j * P2:(j + 1) * P2, :].T
            fs, parts = [], []
            for d, z_ref in ((0, zf_ref), (1, zb_ref)):
                lb = lb_ref[d:d + 1, :]
                f = lb + (1.0 - lb) * _sigmoid(z_ref[0, rows, :])
                lf = jnp.log(f)
                h1 = lf.astype(BF16)
                r1 = lf - h1.astype(F32)
                h2 = r1.astype(BF16)
                h3 = (r1 - h2.astype(F32)).astype(BF16)
                fs.append(f)
                parts.append((h1, h2, h3))
            a_s = [_dot(allow_b16[d], parts[d][0]) + _dot(allow_b16[d], parts[d][1])
                   + _dot(allow_b16[d], parts[d][2]) for d in (0, 1)]
            scs = []
            for d in (0, 1):
                a = a_s[d]
                kd = (1.0 - fs[d]) * jnp.exp(-a)
                ends = [a[c * C + C - 1:c * C + C, :] if d == 0 else a[c * C:c * C + 1, :]
                        for c in range(gc)]
                eal = [jnp.exp(e) for e in ends]
                ebc = [jnp.broadcast_to(e, (C, W)) for e in eal]
                ks = kd * jnp.concatenate(ebc, axis=0)
                is_first = [(c % 2 == 0) == (d == 0) for c in range(gc)]
                k_mul = jnp.concatenate([ebc[c ^ 1] if is_first[c] else ones_row
                                         for c in range(gc)], axis=0)
                ks_ref[d, rows, :] = (ks * k_mul).astype(BF16)
                n8 = (gc // 2) * 8
                eal_ref[d, pl.ds(pl.multiple_of(g * n8, n8), n8), :] = jnp.concatenate(
                    [jnp.broadcast_to(eal[2 * j] * eal[2 * j + 1], (8, W)) for j in range(gc // 2)],
                    axis=0)
                if need_o:
                    qd = q_ref[0, rows, :].astype(F32) * jnp.exp(a)
                    q_mul = jnp.concatenate([ones_row if is_first[c] else ebc[c ^ 1]
                                             for c in range(gc)], axis=0)
                    qd2 = (qd * q_mul).astype(BF16)
                    qd_ref[d, rows, :] = qd2
                    kds = stack_heads(kd.astype(BF16))
                    s_in = _dot_nt(qd.astype(BF16), kds)
                    s_x = _dot_nt(qd2, kds)
                    scs.append(jnp.where(allow4[d], s_in,
                                         jnp.where(cross4[d], s_x, 0.0)).astype(BF16))
            if need_o:
                os_ref[rows, :] = _dot(scs[0], vs) + _dot(scs[1], vs)
            return carry

        lax.fori_loop(0, n_chunks // gc, body, 0, unroll=4)

    def phase_b(n_chunks, need_o):
        n_steps = n_chunks // 2
        gc = next(t for t in (4, 2, 1) if n_steps % t == 0)
        pw = 2 * HGRN_DK
        pairs = [slice(p * pw, (p + 1) * pw) for p in range(HGRN_HEADS // 2)]
        same_head = (lax.broadcasted_iota(jnp.int32, (pw, pw), 0) // HGRN_DK
                     == lax.broadcasted_iota(jnp.int32, (pw, pw), 1) // HGRN_DK)

        def step_rows(m):
            return pl.ds(pl.multiple_of(m * P2, P2), P2)

        def body(g, carry):
            idx = [[g * gc + c, n_steps - 1 - (g * gc + c)] for c in range(gc)]

            def updates(c):
                out = []
                for d, m in enumerate(idx[c]):
                    vt = vt_ref[m]
                    ks = ks_ref[d, step_rows(m), :]
                    out.append([jnp.where(same_head, _dot(vt[s, :], ks[:, s]), 0.0) for s in pairs])
                return out

            upd = {0: updates(0)}
            inters = []
            for c in range(gc):
                if c + 1 < gc:
                    upd[c + 1] = updates(c + 1)
                for d in (0, 1):
                    m = idx[c][d]
                    rows = step_rows(m)
                    eal = eal_ref[d, pl.ds(pl.multiple_of(m * 8, 8), 8), :][0:1, :]
                    sts = [st_ref[d, p] for p in range(len(pairs))]
                    if need_o:
                        qd = qd_ref[d, rows, :]
                        inter = [_dot_nt(qd[:, s], st.astype(BF16)) for s, st in zip(pairs, sts)]
                        inters.append((rows, jnp.concatenate(inter, axis=1)))
                    for p, s in enumerate(pairs):
                        st_ref[d, p] = sts[p] * eal[:, s] + upd[c][d][p]
            for rows, inter in inters:
                os_ref[rows, :] = os_ref[rows, :] + inter
            return carry

        lax.fori_loop(0, n_steps // gc, body, 0, unroll=2)

    def phase_c(g_ref, o_ref, n_rows):
        rt = next(t for t in (256, 128, 64) if n_rows % t == 0)

        def body(t, carry):
            rows = pl.ds(pl.multiple_of(t * rt, rt), rt)
            o = os_ref[rows, :]
            hi, lo = _split2(o * o)
            ms = (_dot(hi, bd_ones) + _dot(lo, bd_ones)) * (1.0 / HGRN_DK)
            y = o * lax.rsqrt(ms + EPS) * ng_ref[...]
            g = g_ref[0, rows, :].astype(F32)
            o_ref[0, rows, :] = (y * (g * _sigmoid(g))).astype(BF16)
            return carry

        lax.fori_loop(0, n_rows // rt, body, 0)

    st_ref[...] = jnp.zeros_like(st_ref)
    phase_a(qc_ref, zfc_ref, zbc_ref, ic_ref, n_ctx, with_ctx)
    phase_b(n_ctx, with_ctx)
    if with_ctx:
        phase_c(gc_ref, oc_ref, n_ctx * C)
    phase_a(ql_ref, zfl_ref, zbl_ref, il_ref, n_lat, True)
    phase_b(n_lat, True)
    phase_c(gl_ref, ol_ref, n_lat * C)


def _hgrn(lat, ctx, lb, norm_g4, with_ctx):
    b, s, _ = lat[0].shape
    l = ctx[0].shape[1]
    assert s % (2 * HGRN_CHUNK) == 0 and l % (2 * HGRN_CHUNK) == 0

    def seq(t):
        return pl.BlockSpec((1, t, MIX), lambda i: (i, 0, 0), pipeline_mode=pl.Buffered(1))

    out_specs = [pl.BlockSpec((1, s, MIX), lambda i: (i, 0, 0))]
    out_shape = [jax.ShapeDtypeStruct((b, s, MIX), BF16)]
    t_max = max(s, l)
    scratch = [pltpu.VMEM((2, HGRN_HEADS // 2, 2 * HGRN_DK, 2 * HGRN_DK), F32),
               pltpu.VMEM((2, t_max, MIX), BF16),
               pltpu.VMEM((2, t_max, MIX), BF16),
               pltpu.VMEM((t_max // (2 * HGRN_CHUNK), MIX, 2 * HGRN_CHUNK), BF16),
               pltpu.VMEM((2, t_max // (2 * HGRN_CHUNK) * 8, MIX), F32),
               pltpu.VMEM((t_max, MIX), F32)]
    if with_ctx:
        out_specs.append(pl.BlockSpec((1, l, MIX), lambda i: (i, 0, 0)))
        out_shape.append(jax.ShapeDtypeStruct((b, l, MIX), BF16))
    res = pl.pallas_call(
        functools.partial(_hgrn_kernel, n_lat=s // HGRN_CHUNK, n_ctx=l // HGRN_CHUNK,
                          with_ctx=with_ctx),
        grid=(b,),
        in_specs=[seq(s)] * 5 + [seq(l)] * 5 + [_const_spec((2, MIX)), _const_spec((1, MIX))],
        out_specs=out_specs,
        out_shape=out_shape,
        scratch_shapes=scratch,
        compiler_params=_cparams(1),
        name="hgrn",
    )(*lat, *ctx, lb, norm_g4)
    return (res[0], res[1]) if with_ctx else (res[0], None)


def _na_bias_table(rpb):
    n_h = rpb.shape[0]
    w, nr = GRID_W, 2 * NA_KH - 1
    side = (w - 1) - (NA_KW - 1)
    p = jnp.pad(rpb.astype(F32), ((0, 0), (0, 0), (side, side + 1)))
    skew = jnp.broadcast_to(p[:, :, None, :], (n_h, nr, w, 2 * w)).reshape(n_h, nr, 2 * w * w)
    skew = skew[:, :, :w * (2 * w - 1)].reshape(n_h, nr, w, 2 * w - 1)
    toep = skew[..., w - 1:]
    qc = np.arange(w)[:, None]
    kc = np.arange(w)[None, :]
    win = np.clip(qc - NA_KW // 2, 0, w - NA_KW)
    valid = (kc >= win) & (kc < win + NA_KW)
    toep = jnp.where(valid[None, None], toep, NEG_INF)
    two = jnp.concatenate([toep[:, :-1], toep[:, 1:]], axis=-1)
    two = two.reshape(n_h // 2, 2, nr - 1, w, 2 * w).transpose(0, 2, 1, 3, 4)
    return two.reshape(n_h // 2, nr - 1, 2 * w, 2 * w)


def _softmax_pv(s_list, v_list):
    m = functools.reduce(jnp.maximum, [jnp.max(s, axis=-1, keepdims=True) for s in s_list])
    p_list = [jnp.exp(s - m) for s in s_list]
    l = functools.reduce(jnp.add, [jnp.sum(p, axis=-1, keepdims=True) for p in p_list])
    o = functools.reduce(jnp.add, [_dot(p.astype(BF16), v) for p, v in zip(p_list, v_list)])
    return o * (1.0 / l)


NA_ROWS_PER_STEP = 16


def _na_kernel(q_ref, k_ref, v_ref, kc_ref, vc_ref, bias_ref, o_ref, *, rows):
    lane = lax.broadcasted_iota(jnp.int32, (1, 2 * NA_HEAD_DIM), 1)
    first = lane < NA_HEAD_DIM
    kc = kc_ref[0]
    vc = vc_ref[0]
    win = NA_KH * GRID_W
    nq = 2 * GRID_W

    def body(g, carry):
        qs, kws, vws, dis, qrows = [], [], [], [], []
        for j in range(NA_ROWS_PER_STEP):
            r = g * NA_ROWS_PER_STEP + j
            rs = jnp.clip(r - NA_KH // 2, 0, rows - NA_KH)
            dis.append(r - rs)
            qrows.append(pl.ds(pl.multiple_of(r * GRID_W, GRID_W), GRID_W))
            krows = pl.ds(pl.multiple_of(rs * GRID_W, GRID_W), win)
            q = q_ref[0, qrows[-1], :]
            qs += [jnp.where(first, q, jnp.zeros_like(q)), jnp.where(first, jnp.zeros_like(q), q)]
            kws.append(k_ref[0, krows, :])
            vws.append(v_ref[0, krows, :])
        qs = jnp.concatenate(qs, axis=0)
        sx = _dot_nt(qs, kc)
        pws, pxs, inv_ls = [], [], []
        for j in range(NA_ROWS_PER_STEP):
            ri0 = NA_KH - 1 - dis[j]
            bias = jnp.concatenate([bias_ref[0, ri0 + 2 * c] for c in range(NA_KH // 2)], axis=1)
            sw = _dot_nt(qs[j * nq:(j + 1) * nq], kws[j]) + bias
            sxj = sx[j * nq:(j + 1) * nq]
            m = jnp.maximum(jnp.max(sw, axis=-1, keepdims=True), jnp.max(sxj, axis=-1, keepdims=True))
            pw = jnp.exp(sw - m)
            px = jnp.exp(sxj - m)
            inv_ls.append(1.0 / (jnp.sum(pw, axis=-1, keepdims=True) + jnp.sum(px, axis=-1, keepdims=True)))
            pws.append(pw.astype(BF16))
            pxs.append(px.astype(BF16))
        ows = [_dot(pws[j], vws[j]) for j in range(NA_ROWS_PER_STEP)]
        oc = _dot(jnp.concatenate(pxs, axis=0), vc)
        for j in range(NA_ROWS_PER_STEP):
            o = (ows[j] + oc[j * nq:(j + 1) * nq]) * inv_ls[j]
            o_ref[0, qrows[j], :] = jnp.where(first, o[:GRID_W], o[GRID_W:]).astype(BF16)
        return carry

    lax.fori_loop(0, rows // NA_ROWS_PER_STEP, body, 0)


def _na_latent(q, k, v, kc, vc, bias):
    b, s, _ = q.shape
    l = kc.shape[1]
    rows = s // GRID_W
    assert s % GRID_W == 0 and rows >= NA_KH and rows % NA_ROWS_PER_STEP == 0
    hp = 2 * NA_HEAD_DIM

    def seq(t):
        return pl.BlockSpec((1, t, hp), lambda i, j: (i, 0, j))

    return pl.pallas_call(
        functools.partial(_na_kernel, rows=rows),
        grid=(b, NA_HEADS // 2),
        in_specs=[seq(s), seq(s), seq(s), seq(l), seq(l),
                  pl.BlockSpec((1, 2 * NA_KH - 2, 2 * GRID_W, 2 * GRID_W),
                               lambda i, j: (j, 0, 0, 0))],
        out_specs=seq(s),
        out_shape=jax.ShapeDtypeStruct((b, s, NA_WIDTH), BF16),
        compiler_params=_cparams(2),
        name="na_latent",
    )(q, k, v, kc, vc, bias)


def _na_ctx_kernel(q_ref, k_ref, v_ref, o_ref):
    lane = lax.broadcasted_iota(jnp.int32, (1, 2 * NA_HEAD_DIM), 1)
    first = lane < NA_HEAD_DIM
    for p in range(NA_HEADS // 2):
        cols = slice(p * 2 * NA_HEAD_DIM, (p + 1) * 2 * NA_HEAD_DIM)
        q, k, v = q_ref[0, :, cols], k_ref[0, :, cols], v_ref[0, :, cols]
        outs = []
        for h in range(2):
            qh = jnp.where(first if h == 0 else ~first, q, jnp.zeros_like(q))
            outs.append(_softmax_pv([_dot_nt(qh, k)], [v]))
        o_ref[0, :, cols] = jnp.where(first, outs[0], outs[1]).astype(BF16)


def _na_context(q, k, v):
    b, l, _ = q.shape
    spec = pl.BlockSpec((1, l, NA_WIDTH), lambda i: (i, 0, 0))
    return pl.pallas_call(
        _na_ctx_kernel,
        grid=(b,),
        in_specs=[spec, spec, spec],
        out_specs=spec,
        out_shape=jax.ShapeDtypeStruct((b, l, NA_WIDTH), BF16),
        compiler_params=_cparams(1),
        name="na_context",
    )(q, k, v)


def _ffn_kernel(x_ref, f_ref, h_ref, n_ref, g1_ref, sh_ref, sc_ref, g2_ref, ng_ref, fg_ref,
                wo_ref, wg_ref, wu_ref, wd_ref, o_ref, *, final):
    y = (_dot(f_ref[...], wo_ref[0:MIX, :]) + _dot(h_ref[...], wo_ref[MIX:2 * MIX, :])
         + _dot(n_ref[...], wo_ref[2 * MIX:, :]))
    x1 = x_ref[...] + g1_ref[0] * y
    hn = x1 * lax.rsqrt(jnp.mean(x1 * x1, axis=-1, keepdims=True) + EPS) * ng_ref[...]
    hm = (hn * (1.0 + sc_ref[0]) + sh_ref[0]).astype(BF16)
    d_ff = wg_ref.shape[1]
    acc = jnp.zeros(x1.shape, F32)
    for j in range(d_ff // FF_CHUNK):
        cols = slice(j * FF_CHUNK, (j + 1) * FF_CHUNK)
        g = _dot(hm, wg_ref[:, cols])
        u = _dot(hm, wu_ref[:, cols])
        act = (g * _sigmoid(g) * u).astype(BF16)
        acc = acc + _dot(act, wd_ref[cols, :])
    x2 = x1 + g2_ref[0] * acc
    if final:
        x2 = x2 * lax.rsqrt(jnp.mean(x2 * x2, axis=-1, keepdims=True) + EPS) * fg_ref[...]
    o_ref[...] = x2


def _ffn(x2d, f, h, n, norm_g, final_g, modv, mod_base, rows_per_mod, wo, wg, wu, wd, tm, final):
    nt, d = x2d.shape
    d_ff = wg.shape[1]
    assert d_ff % FF_CHUNK == 0

    def mod_idx(j):
        if rows_per_mod is None:
            return lambda i: (mod_base * 6 + j, 0, 0)
        tpm = rows_per_mod // tm
        return lambda i: ((mod_base + i // tpm) * 6 + j, 0, 0)

    def tok(w):
        return pl.BlockSpec((tm, w), lambda i: (i, 0))

    def vec():
        return pl.BlockSpec((1, d), lambda i: (0, 0))

    return pl.pallas_call(
        functools.partial(_ffn_kernel, final=final),
        grid=(nt // tm,),
        in_specs=[tok(d), tok(MIX), tok(MIX), tok(NA_WIDTH),
                  pl.BlockSpec((1, 1, d), mod_idx(2)), pl.BlockSpec((1, 1, d), mod_idx(3)),
                  pl.BlockSpec((1, 1, d), mod_idx(4)), pl.BlockSpec((1, 1, d), mod_idx(5)),
                  vec(), vec(),
                  _const_spec(wo.shape), _const_spec(wg.shape), _const_spec(wu.shape),
                  _const_spec(wd.shape)],
        out_specs=tok(d),
        out_shape=jax.ShapeDtypeStruct((nt, d), F32),
        compiler_params=_cparams(1),
        name="outproj_ffn",
    )(x2d, f, h, n, modv, modv, modv, modv, norm_g.reshape(1, d), final_g.reshape(1, d),
      wo, wg, wu, wd)


def _token_tile(n):
    for tm in (512, 256, 128, 64, 32, 16, 8):
        if n % tm == 0:
            return tm
    raise ValueError(f"token count {n} is not a multiple of 8")


def kernel(x, c, ctx, c_ctx, w_mod, b_mod, norm1_g, w_in, fourier_w, hgrn_lb, hgrn_norm_g, na_rpb,
           w_out, norm2_g, w_ffn_gate, w_ffn_up, w_ffn_down, final_norm_g):
    b, s, d = x.shape
    l = ctx.shape[1]
    depth = w_mod.shape[0]
    assert b < MOD_ROWS and d == 2 * MIX + NA_WIDTH

    cc = jnp.zeros((MOD_ROWS, d), F32).at[:b].set(c).at[b].set(c_ctx)
    mod = _modulation(cc, w_mod, b_mod)
    modv = mod.reshape(depth * MOD_ROWS * 6, 1, d)

    lbp = jax.nn.softmax(hgrn_lb.astype(F32), axis=0)
    lower = jnp.cumsum(lbp, axis=0) - lbp[0:1]
    norm_g4 = jnp.tile(hgrn_norm_g.astype(F32), (1, HGRN_HEADS))[:, None, :]

    tab_s, tab_l = _dft_table(s), _dft_table(l)
    c64, s64 = _channel_dft_consts()
    eye_g = jnp.eye(FOURIER_GROUPS, dtype=F32)

    tm_s = _token_tile(s)
    tm_c = _token_tile(b * l)
    xl = x.reshape(b * s, d)
    xc = ctx.reshape(b * l, d)
    for li in range(depth):
        with_ctx = li < depth - 1
        final = li == depth - 1
        base = li * MOD_ROWS
        w_in_b = w_in[li].astype(BF16)
        ul = _inproj(xl, norm1_g[li], modv, base, s, w_in_b, tm_s)
        uc = _inproj(xc, norm1_g[li], modv, base + b, None, w_in_b, tm_c)
        ul = [a.reshape(b, s, -1) for a in ul]
        uc = [a.reshape(b, l, -1) for a in uc]

        w_bd = (eye_g[:, None, :, None] * fourier_w[li][:, :, None, :]).reshape(MIX, MIX)
        f_lat = _fourier(ul[0], tab_s, c64, s64, w_bd)
        h_lat, h_ctx = _hgrn(ul[1:6], uc[1:6], lower[li], norm_g4[li], with_ctx)
        n_lat = _na_latent(ul[6], ul[7], ul[8], uc[7], uc[8], _na_bias_table(na_rpb[li]))

        wo, wg = w_out[li].astype(BF16), w_ffn_gate[li].astype(BF16)
        wu, wd = w_ffn_up[li].astype(BF16), w_ffn_down[li].astype(BF16)
        if with_ctx:
            f_ctx = _fourier(uc[0], tab_l, c64, s64, w_bd)
            n_ctx = _na_context(uc[6], uc[7], uc[8])
            xc = _ffn(xc, f_ctx.reshape(b * l, -1), h_ctx.reshape(b * l, -1),
                      n_ctx.reshape(b * l, -1), norm2_g[li], final_norm_g, modv, base + b, None,
                      wo, wg, wu, wd, tm_c, False)
        xl = _ffn(xl, f_lat.reshape(b * s, -1), h_lat.reshape(b * s, -1), n_lat.reshape(b * s, -1),
                  norm2_g[li], final_norm_g, modv, base, s, wo, wg, wu, wd, tm_s, final)
    return xl.reshape(b, s, d)
```

```python
import functools

import jax
import jax.numpy as jnp
import numpy as np
from jax import lax
from jax.experimental import pallas as pl
from jax.experimental.pallas import tpu as pltpu

F32 = jnp.float32
BF16 = jnp.bfloat16

EPS = 1e-6
NEG_INF = -1e30
GRID_W = 64
FOURIER_GROUPS = 4
FOURIER_GD = 64
HGRN_DK = 64
HGRN_HEADS = 4
HGRN_CHUNK = 32
NA_HEAD_DIM = 64
NA_HEADS = 8
NA_KH = 8
NA_KW = 16
MIX = 256
NA_WIDTH = NA_HEADS * NA_HEAD_DIM
MOD_ROWS = 16
FF_CHUNK = 256
VMEM_LIMIT = 56 * 1024 * 1024


def _cparams(n_axes):
    return pltpu.CompilerParams(dimension_semantics=("arbitrary",) * n_axes,
                                vmem_limit_bytes=VMEM_LIMIT)


def _split2(a):
    hi = a.astype(BF16)
    lo = (a - hi.astype(F32)).astype(BF16)
    return hi, lo


def _dot(a, b):
    return jnp.dot(a, b, preferred_element_type=F32)


def _dot_nt(a, b):
    return lax.dot_general(a, b, (((1,), (1,)), ((), ())), preferred_element_type=F32)


def _dot_tn(a, b):
    return lax.dot_general(a, b, (((0,), (0,)), ((), ())), preferred_element_type=F32)


def _dot_hilo(a, b):
    ah, al = _split2(a)
    bh, bl = _split2(b)
    return _dot(ah, bh) + _dot(al, bh) + _dot(ah, bl)


def _sigmoid(z):
    return 1.0 / (1.0 + jnp.exp(-z))


def _const_spec(shape):
    nd = len(shape)
    return pl.BlockSpec(shape, lambda *_: (0,) * nd, pipeline_mode=pl.Buffered(1))


def _mod_kernel(c_ref, w_ref, b_ref, o_ref):
    c = c_ref[...]
    s = c * _sigmoid(c)
    o_ref[0] = _dot_hilo(s, w_ref[0]) + b_ref[0]


def _modulation(cc, w_mod, b_mod):
    depth, d, n = w_mod.shape
    tn = 1024
    return pl.pallas_call(
        _mod_kernel,
        grid=(depth, n // tn),
        in_specs=[pl.BlockSpec((MOD_ROWS, d), lambda l, j: (0, 0)),
                  pl.BlockSpec((1, d, tn), lambda l, j: (l, 0, j)),
                  pl.BlockSpec((1, 1, tn), lambda l, j: (l, 0, j))],
        out_specs=pl.BlockSpec((1, MOD_ROWS, tn), lambda l, j: (l, 0, j)),
        out_shape=jax.ShapeDtypeStruct((depth, MOD_ROWS, n), F32),
        compiler_params=_cparams(2),
        name="modulation",
    )(cc, w_mod, b_mod.reshape(depth, 1, n))


def _inproj_kernel(x_ref, g_ref, sh_ref, sc_ref, w_ref,
                   uf_ref, hq_ref, zf_ref, zb_ref, hi_ref, hg_ref, q_ref, k_ref, v_ref):
    x = x_ref[...]
    y = x * lax.rsqrt(jnp.mean(x * x, axis=-1, keepdims=True) + EPS) * g_ref[...]
    a = (y * (1.0 + sc_ref[0]) + sh_ref[0]).astype(BF16)
    col = 0
    for ref in (uf_ref, hq_ref, zf_ref, zb_ref, hi_ref, hg_ref):
        ref[...] = _dot(a, w_ref[:, col:col + MIX]).astype(ref.dtype)
        col += MIX
    scale = NA_HEAD_DIM ** -0.5
    q_ref[...] = (_dot(a, w_ref[:, col:col + NA_WIDTH]) * scale).astype(BF16)
    col += NA_WIDTH
    k_ref[...] = _dot(a, w_ref[:, col:col + NA_WIDTH]).astype(BF16)
    col += NA_WIDTH
    v_ref[...] = _dot(a, w_ref[:, col:col + NA_WIDTH]).astype(BF16)


def _inproj(x2d, norm_g, modv, mod_base, rows_per_mod, w_in_b, tm):
    n, d = x2d.shape
    in_w = w_in_b.shape[1]

    def mod_idx(j):
        if rows_per_mod is None:
            return lambda i: (mod_base * 6 + j, 0, 0)
        tpm = rows_per_mod // tm
        return lambda i: ((mod_base + i // tpm) * 6 + j, 0, 0)

    def tok(w):
        return pl.BlockSpec((tm, w), lambda i: (i, 0))

    outs = [(MIX, BF16), (MIX, BF16), (MIX, F32), (MIX, F32), (MIX, BF16), (MIX, BF16),
            (NA_WIDTH, BF16), (NA_WIDTH, BF16), (NA_WIDTH, BF16)]
    return pl.pallas_call(
        _inproj_kernel,
        grid=(n // tm,),
        in_specs=[tok(d),
                  pl.BlockSpec((1, d), lambda i: (0, 0)),
                  pl.BlockSpec((1, 1, d), mod_idx(0)),
                  pl.BlockSpec((1, 1, d), mod_idx(1)),
                  _const_spec((d, in_w))],
        out_specs=[tok(w) for w, _ in outs],
        out_shape=[jax.ShapeDtypeStruct((n, w), dt) for w, dt in outs],
        compiler_params=_cparams(1),
        name="inproj",
    )(x2d, norm_g.reshape(1, d), modv, modv, w_in_b)


def _dft_table(t_len):
    kb = min(64, t_len)
    assert t_len % kb == 0
    t = jnp.arange(t_len // 2, dtype=jnp.int32)[None, :]
    k_hi = jnp.arange(t_len // kb, dtype=jnp.int32)[:, None] * kb
    k_lo = jnp.arange(kb, dtype=jnp.int32)[:, None]
    w = 2.0 * np.pi / t_len
    ang_hi = ((k_hi * t) % t_len).astype(F32) * w
    ang_lo = ((k_lo * t) % t_len).astype(F32) * w
    ch, sh = jnp.cos(ang_hi)[:, None, :], jnp.sin(ang_hi)[:, None, :]
    cl, sl = jnp.cos(ang_lo)[None, :, :], jnp.sin(ang_lo)[None, :, :]
    cos = (ch * cl - sh * sl).reshape(t_len, t_len // 2)
    sin = (sh * cl + ch * sl).reshape(t_len, t_len // 2)
    return jnp.concatenate([cos, sin], axis=1).astype(BF16)


def _channel_dft_consts():
    cd = np.outer(np.arange(FOURIER_GD), np.arange(FOURIER_GD)) % FOURIER_GD
    ang = 2.0 * np.pi * cd / FOURIER_GD
    eye = np.eye(FOURIER_GROUPS)
    return (jnp.asarray(np.kron(eye, np.cos(ang)), F32),
            jnp.asarray(np.kron(eye, np.sin(ang)), F32))


def _fourier_kernel(u_ref, tab_ref, c64_ref, s64_ref, w_ref, o_ref, pq_ref, pm_ref, *, t_len, tm):
    half = t_len // 2
    bs = min(256, half)
    nb = half // bs
    bi = pl.program_id(1)

    @pl.when(pl.program_id(0) == 0)
    def _():
        norm = (FOURIER_GD * t_len) ** -0.5
        w = w_ref[...]
        a = (_dot_hilo(c64_ref[...], w) * norm).astype(BF16)
        b = (_dot_hilo(s64_ref[...], w) * (-norm)).astype(BF16)
        p_i = lax.broadcasted_iota(jnp.int32, (bs, bs), 0)
        q_i = lax.broadcasted_iota(jnp.int32, (bs, bs), 1)
        rev_shift = jnp.where(q_i == bs - p_i, 1.0, 0.0).astype(BF16)
        first_row = jnp.where(p_i + q_i == 0, 1.0, 0.0).astype(BF16)
        for i in range(nb):
            src = half + (nb - 1 - i) * bs
            r = _dot(rev_shift, u_ref[0, src:src + bs, :])
            if i > 0:
                r = r + _dot(first_row, u_ref[0, src + bs:src + 2 * bs, :])
            u = u_ref[0, i * bs:(i + 1) * bs, :].astype(F32)
            pq_ref[bi, i * bs:(i + 1) * bs, :] = _dot((u + r).astype(BF16), a).astype(BF16)
            pq_ref[bi, half + i * bs:half + (i + 1) * bs, :] = _dot((u - r).astype(BF16),
                                                                    b).astype(BF16)
        pm_ref[bi] = _dot(u_ref[0, half:half + 8, :], a)

    k = pl.program_id(0) * tm + lax.broadcasted_iota(jnp.int32, (tm, 1), 0)
    sign = (1 - 2 * (k & 1)).astype(F32)
    o_ref[0] = (_dot(tab_ref[...], pq_ref[bi]) + sign * pm_ref[bi, 0:1, :]).astype(BF16)


def _fourier(u, tab, c64, s64, w_bd):
    b, t_len, _ = u.shape
    assert t_len % 64 == 0
    tm = min(512, t_len)
    return pl.pallas_call(
        functools.partial(_fourier_kernel, t_len=t_len, tm=tm),
        grid=(t_len // tm, b),
        in_specs=[pl.BlockSpec((1, t_len, MIX), lambda j, i: (jnp.where(j == 0, i, b - 1), 0, 0)),
                  pl.BlockSpec((tm, t_len), lambda j, i: (j, 0)),
                  _const_spec((MIX, MIX)), _const_spec((MIX, MIX)), _const_spec((MIX, MIX))],
        out_specs=pl.BlockSpec((1, tm, MIX), lambda j, i: (i, j, 0)),
        out_shape=jax.ShapeDtypeStruct((b, t_len, MIX), BF16),
        scratch_shapes=[pltpu.VMEM((b, t_len, MIX), BF16), pltpu.VMEM((b, 8, MIX), F32)],
        compiler_params=_cparams(2),
        name="fourier",
    )(u, tab, c64, s64, w_bd)


def _hgrn_kernel(ql_ref, zfl_ref, zbl_ref, il_ref, gl_ref,
                 qc_ref, zfc_ref, zbc_ref, ic_ref, gc_ref, lb_ref, ng_ref,
                 *rest, n_lat, n_ctx, with_ctx):
    if with_ctx:
        ol_ref, oc_ref, st_ref, qd_ref, ks_ref, vt_ref, eal_ref, os_ref = rest
    else:
        ol_ref, st_ref, qd_ref, ks_ref, vt_ref, eal_ref, os_ref = rest
        oc_ref = None
    C = HGRN_CHUNK
    P2 = 2 * C
    W = MIX
    lane_head =lax.broadcasted_iota(jnp.int32, (1, W), 1) // HGRN_DK
    head_mask = [lane_head == h for h in range(HGRN_HEADS)]
    bd = (lax.broadcasted_iota(jnp.int32, (W, W), 0) // HGRN_DK
          == lax.broadcasted_iota(jnp.int32, (W, W), 1) // HGRN_DK)
    bd_ones = jnp.where(bd, 1.0, 0.0).astype(BF16)

    def stack_heads(x):
        return jnp.concatenate([jnp.where(m, x, jnp.zeros_like(x)) for m in head_mask], axis=0)

    def group_chunks(n_chunks):
        return 4 if n_chunks % 4 == 0 else 2

    def phase_a(q_ref, zf_ref, zb_ref, i_ref, n_chunks, need_o):
        gc = group_chunks(n_chunks)
        gt = gc * C
        t_i = lax.broadcasted_iota(jnp.int32, (gt, gt), 0)
        s_i = lax.broadcasted_iota(jnp.int32, (gt, gt), 1)
        same = (t_i // C) == (s_i // C)
        allow = (same & (s_i <= t_i), same & (s_i >= t_i))
        allow_b16 = tuple(jnp.where(m, 1.0, 0.0).astype(BF16) for m in allow)
        t4 = lax.broadcasted_iota(jnp.int32, (gt, HGRN_HEADS * gt), 0)
        s4 = lax.broadcasted_iota(jnp.int32, (gt, HGRN_HEADS * gt), 1) % gt
        same4 = (t4 // C) == (s4 // C)
        allow4 = (same4 & (s4 <= t4), same4 & (s4 >= t4))
        pair4 = (t4 // P2) == (s4 // P2)
        cross4 = (pair4 & (t4 // C == s4 // C + 1), pair4 & (t4 // C + 1 == s4 // C))
        ones_row = jnp.ones((C, W), F32)

        def body(g, carry):
            rows = pl.ds(pl.multiple_of(g * gt, gt), gt)
            v = i_ref[0, rows, :]
            vs = stack_heads(v) if need_o else None
            for j in range(gc // 2):
                vt_ref[g * (gc // 2) + j] = v[j * P2:(j + 1) * P2, :].T
            fs, parts = [], []
            for d, z_ref in ((0, zf_ref), (1, zb_ref)):
                lb = lb_ref[d:d + 1, :]
                f = lb + (1.0 - lb) * _sigmoid(z_ref[0, rows, :])
                lf = jnp.log(f)
                h1 = lf.astype(BF16)
                r1 = lf - h1.astype(F32)
                h2 = r1.astype(BF16)
                h3 = (r1 - h2.astype(F32)).astype(BF16)
                fs.append(f)
                parts.append((h1, h2, h3))
            a_s = [_dot(allow_b16[d], parts[d][0]) + _dot(allow_b16[d], parts[d][1])
                   + _dot(allow_b16[d], parts[d][2]) for d in (0, 1)]
            scs = []
            for d in (0, 1):
                a = a_s[d]
                kd = (1.0 - fs[d]) * jnp.exp(-a)
                ends = [a[c * C + C - 1:c * C + C, :] if d == 0 else a[c * C:c * C + 1, :]
                        for c in range(gc)]
                eal = [jnp.exp(e) for e in ends]
                ebc = [jnp.broadcast_to(e, (C, W)) for e in eal]
                ks = kd * jnp.concatenate(ebc, axis=0)
                is_first = [(c % 2 == 0) == (d == 0) for c in range(gc)]
                k_mul = jnp.concatenate([ebc[c ^ 1] if is_first[c] else ones_row
                                         for c in range(gc)], axis=0)
                ks_ref[d, rows, :] = (ks * k_mul).astype(BF16)
                n8 = (gc // 2) * 8
                eal_ref[d, pl.ds(pl.multiple_of(g * n8, n8), n8), :] = jnp.concatenate(
                    [jnp.broadcast_to(eal[2 * j] * eal[2 * j + 1], (8, W)) for j in range(gc // 2)],
                    axis=0)
                if need_o:
                    qd = q_ref[0, rows, :].astype(F32) * jnp.exp(a)
                    q_mul = jnp.concatenate([ones_row if is_first[c] else ebc[c ^ 1]
                                             for c in range(gc)], axis=0)
                    qd2 = (qd * q_mul).astype(BF16)
                    qd_ref[d, rows, :] = qd2
                    s2 = _dot_nt(jnp.concatenate([qd.astype(BF16), qd2], axis=0),
                                 stack_heads(kd.astype(BF16)))
                    scs.append(jnp.where(allow4[d], s2[:gt],
                                         jnp.where(cross4[d], s2[gt:], 0.0)).astype(BF16))
            if need_o:
                os_ref[rows, :] = _dot(scs[0], vs) + _dot(scs[1], vs)
            return carry

        lax.fori_loop(0, n_chunks // gc, body, 0, unroll=4)

    def phase_b(n_chunks, need_o):
        n_steps = n_chunks // 2
        gc = next(t for t in (4, 2, 1) if n_steps % t == 0)
        pw = 2 * HGRN_DK
        pairs = [slice(p * pw, (p + 1) * pw) for p in range(HGRN_HEADS // 2)]
        same_head = (lax.broadcasted_iota(jnp.int32, (pw, pw), 0) // HGRN_DK
                     == lax.broadcasted_iota(jnp.int32, (pw, pw), 1) // HGRN_DK)

        def step_rows(m):
            return pl.ds(pl.multiple_of(m * P2, P2), P2)

        def body(g, carry):
            idx = [[g * gc + c, n_steps - 1 - (g * gc + c)] for c in range(gc)]

            def updates(c):
                out = []
                for d, m in enumerate(idx[c]):
                    vt = vt_ref[m]
                    ks = ks_ref[d, step_rows(m), :]
                    out.append([jnp.where(same_head, _dot(vt[s, :], ks[:, s]), 0.0) for s in pairs])
                return out

            upd = {0: updates(0)}
            inters = []
            for c in range(gc):
                if c + 1 < gc:
                    upd[c + 1] = updates(c + 1)
                for d in (0, 1):
                    m = idx[c][d]
                    rows = step_rows(m)
                    eal = eal_ref[d, pl.ds(pl.multiple_of(m * 8, 8), 8), :][0:1, :]
                    sts = [st_ref[d, p] for p in range(len(pairs))]
                    if need_o:
                        qd = qd_ref[d, rows, :]
                        inter = [_dot_nt(qd[:, s], st.astype(BF16)) for s, st in zip(pairs, sts)]
                        inters.append((rows, jnp.concatenate(inter, axis=1)))
                    for p, s in enumerate(pairs):
                        st_ref[d, p] = sts[p] * eal[:, s] + upd[c][d][p]
            for rows, inter in inters:
                os_ref[rows, :] = os_ref[rows, :] + inter
            return carry

        lax.fori_loop(0, n_steps // gc, body, 0, unroll=2)

    def phase_c(g_ref, o_ref, n_rows):
        rt = next(t for t in (256, 128, 64) if n_rows % t == 0)

        def body(t, carry):
            rows = pl.ds(pl.multiple_of(t * rt, rt), rt)
            o = os_ref[rows, :]
            hi, lo = _split2(o * o)
            ms = (_dot(hi, bd_ones) + _dot(lo, bd_ones)) * (1.0 / HGRN_DK)
            y = o * lax.rsqrt(ms + EPS) * ng_ref[...]
            g = g_ref[0, rows, :].astype(F32)
            o_ref[0, rows, :] = (y * (g * _sigmoid(g))).astype(BF16)
            return carry

        lax.fori_loop(0, n_rows // rt, body, 0)

    st_ref[...] = jnp.zeros_like(st_ref)
    phase_a(qc_ref, zfc_ref, zbc_ref, ic_ref, n_ctx, with_ctx)
    phase_b(n_ctx, with_ctx)
    if with_ctx:
        phase_c(gc_ref, oc_ref, n_ctx * C)
    phase_a(ql_ref, zfl_ref, zbl_ref, il_ref, n_lat, True)
    phase_b(n_lat, True)
    phase_c(gl_ref, ol_ref, n_lat * C)


def _hgrn(lat, ctx, lb, norm_g4, with_ctx):
    b, s, _ = lat[0].shape
    l = ctx[0].shape[1]
    assert s % (2 * HGRN_CHUNK) == 0 and l % (2 * HGRN_CHUNK) == 0

    def seq(t):
        return pl.BlockSpec((1, t, MIX), lambda i: (i, 0, 0), pipeline_mode=pl.Buffered(1))

    out_specs = [pl.BlockSpec((1, s, MIX), lambda i: (i, 0, 0))]
    out_shape = [jax.ShapeDtypeStruct((b, s, MIX), BF16)]
    t_max = max(s, l)
    scratch = [pltpu.VMEM((2, HGRN_HEADS // 2, 2 * HGRN_DK, 2 * HGRN_DK), F32),
               pltpu.VMEM((2, t_max, MIX), BF16),
               pltpu.VMEM((2, t_max, MIX), BF16),
               pltpu.VMEM((t_max // (2 * HGRN_CHUNK), MIX, 2 * HGRN_CHUNK), BF16),
               pltpu.VMEM((2, t_max // (2 * HGRN_CHUNK) * 8, MIX), F32),
               pltpu.VMEM((t_max, MIX), F32)]
    if with_ctx:
        out_specs.append(pl.BlockSpec((1, l, MIX), lambda i: (i, 0, 0)))
        out_shape.append(jax.ShapeDtypeStruct((b, l, MIX), BF16))
    res = pl.pallas_call(
        functools.partial(_hgrn_kernel, n_lat=s // HGRN_CHUNK, n_ctx=l // HGRN_CHUNK,
                          with_ctx=with_ctx),
        grid=(b,),
        in_specs=[seq(s)] * 5 + [seq(l)] * 5 + [_const_spec((2, MIX)), _const_spec((1, MIX))],
        out_specs=out_specs,
        out_shape=out_shape,
        scratch_shapes=scratch,
        compiler_params=_cparams(1),
        name="hgrn",
    )(*lat, *ctx, lb, norm_g4)
    return (res[0], res[1]) if with_ctx else (res[0], None)


def _na_bias_table(rpb):
    n_h, nr, nc = rpb.shape
    w = GRID_W
    assert 2 * w == 128 and nc <= w
    padded = jnp.pad(rpb.astype(F32), ((0, 0), (0, 16 - nr), (0, 2 * w - nc)))
    return pl.pallas_call(
        _na_bias_kernel,
        grid=(n_h // 2,),
        in_specs=[pl.BlockSpec((2, 16, 2 * w), lambda i: (i, 0, 0))],
        out_specs=pl.BlockSpec((1, nr - 1, 2 * w, 2 * w), lambda i: (i, 0, 0, 0)),
        out_shape=jax.ShapeDtypeStruct((n_h // 2, nr - 1, 2 * w, 2 * w), F32),
        compiler_params=_cparams(1),
        name="na_bias",
    )(padded)


def _na_bias_kernel(rpb_ref, o_ref):
    w = GRID_W
    q = lax.broadcasted_iota(jnp.int32, (w, 2 * w), 0)
    lane = lax.broadcasted_iota(jnp.int32, (w, 2 * w), 1)
    kc = lane % w
    win = jnp.clip(q - NA_KW // 2, 0, w - NA_KW)
    valid = (kc >= win) & (kc < win + NA_KW)
    lower = lane < w
    shift_lo = 2 * w - (NA_KW - 1)
    shift_hi = shift_lo + w
    for h in range(2):
        for j in range(2 * NA_KH - 2):
            x0 = jnp.broadcast_to(rpb_ref[h, j:j + 1, :], (w, 2 * w))
            x1 = jnp.broadcast_to(rpb_ref[h, j + 1:j + 2, :], (w, 2 * w))
            t0 = pltpu.roll(x0, shift_lo, 1, stride=1, stride_axis=0)
            t1 = pltpu.roll(x1, shift_hi % (2 * w), 1, stride=1, stride_axis=0)
            o_ref[0, j, h * w:(h + 1) * w, :] = jnp.where(valid, jnp.where(lower, t0, t1), NEG_INF)


def _softmax_pv(s_list, v_list):
    m = functools.reduce(jnp.maximum, [jnp.max(s, axis=-1, keepdims=True) for s in s_list])
    p_list = [jnp.exp(s - m) for s in s_list]
    l = functools.reduce(jnp.add, [jnp.sum(p, axis=-1, keepdims=True) for p in p_list])
    o = functools.reduce(jnp.add, [_dot(p.astype(BF16), v) for p, v in zip(p_list, v_list)])
    return o * (1.0 / l)


NA_ROWS_PER_STEP = 16


def _na_kernel(q_ref, k_ref, v_ref, kc_ref, vc_ref, bias_ref, o_ref, *, rows):
    lane = lax.broadcasted_iota(jnp.int32, (1, 2 * NA_HEAD_DIM), 1)
    first = lane < NA_HEAD_DIM
    kc = kc_ref[0]
    vc = vc_ref[0]
    win = NA_KH * GRID_W
    nq = 2 * GRID_W

    def body(g, carry):
        qs, kws, vws, dis, qrows = [], [], [], [], []
        for j in range(NA_ROWS_PER_STEP):
            r = g * NA_ROWS_PER_STEP + j
            rs = jnp.clip(r - NA_KH // 2, 0, rows - NA_KH)
            dis.append(r - rs)
            qrows.append(pl.ds(pl.multiple_of(r * GRID_W, GRID_W), GRID_W))
            krows = pl.ds(pl.multiple_of(rs * GRID_W, GRID_W), win)
            q = q_ref[0, qrows[-1], :]
            qs += [jnp.where(first, q, jnp.zeros_like(q)), jnp.where(first, jnp.zeros_like(q), q)]
            kws.append(k_ref[0, krows, :])
            vws.append(v_ref[0, krows, :])
        qs = jnp.concatenate(qs, axis=0)
        sx = _dot_nt(qs, kc)
        pws, pxs, inv_ls = [], [], []
        for j in range(NA_ROWS_PER_STEP):
            ri0 = NA_KH - 1 - dis[j]
            bias = jnp.concatenate([bias_ref[0, ri0 + 2 * c] for c in range(NA_KH // 2)], axis=1)
            sw = _dot_nt(qs[j * nq:(j + 1) * nq], kws[j]) + bias
            sxj = sx[j * nq:(j + 1) * nq]
            m = jnp.maximum(jnp.max(sw, axis=-1, keepdims=True), jnp.max(sxj, axis=-1, keepdims=True))
            pw = jnp.exp(sw - m)
            px = jnp.exp(sxj - m)
            inv_ls.append(1.0 / (jnp.sum(pw, axis=-1, keepdims=True) + jnp.sum(px, axis=-1, keepdims=True)))
            pws.append(pw.astype(BF16))
            pxs.append(px.astype(BF16))
        ows = [_dot(pws[j], vws[j]) for j in range(NA_ROWS_PER_STEP)]
        oc = _dot(jnp.concatenate(pxs, axis=0), vc)
        for j in range(NA_ROWS_PER_STEP):
            o = (ows[j] + oc[j * nq:(j + 1) * nq]) * inv_ls[j]
            o_ref[0, qrows[j], :] = jnp.where(first, o[:GRID_W], o[GRID_W:]).astype(BF16)
        return carry

    lax.fori_loop(0, rows // NA_ROWS_PER_STEP, body, 0)


def _na_latent(q, k, v, kc, vc, bias):
    b, s, _ = q.shape
    l = kc.shape[1]
    rows = s // GRID_W
    assert s % GRID_W == 0 and rows >= NA_KH and rows % NA_ROWS_PER_STEP == 0
    hp = 2 * NA_HEAD_DIM

    def seq(t):
        return pl.BlockSpec((1, t, hp), lambda i, j: (i, 0, j))

    return pl.pallas_call(
        functools.partial(_na_kernel, rows=rows),
        grid=(b, NA_HEADS // 2),
        in_specs=[seq(s), seq(s), seq(s), seq(l), seq(l),
                  pl.BlockSpec((1, 2 * NA_KH - 2, 2 * GRID_W, 2 * GRID_W),
                               lambda i, j: (j, 0, 0, 0))],
        out_specs=seq(s),
        out_shape=jax.ShapeDtypeStruct((b, s, NA_WIDTH), BF16),
        compiler_params=_cparams(2),
        name="na_latent",
    )(q, k, v, kc, vc, bias)


def _na_ctx_kernel(q_ref, k_ref, v_ref, o_ref):
    lane = lax.broadcasted_iota(jnp.int32, (1, 2 * NA_HEAD_DIM), 1)
    first = lane < NA_HEAD_DIM
    for p in range(NA_HEADS // 2):
        cols = slice(p * 2 * NA_HEAD_DIM, (p + 1) * 2 * NA_HEAD_DIM)
        q, k, v = q_ref[0, :, cols], k_ref[0, :, cols], v_ref[0, :, cols]
        outs = []
        for h in range(2):
            qh = jnp.where(first if h == 0 else ~first, q, jnp.zeros_like(q))
            outs.append(_softmax_pv([_dot_nt(qh, k)], [v]))
        o_ref[0, :, cols] = jnp.where(first, outs[0], outs[1]).astype(BF16)


def _na_context(q, k, v):
    b, l, _ = q.shape
    spec = pl.BlockSpec((1, l, NA_WIDTH), lambda i: (i, 0, 0))
    return pl.pallas_call(
        _na_ctx_kernel,
        grid=(b,),
        in_specs=[spec, spec, spec],
        out_specs=spec,
        out_shape=jax.ShapeDtypeStruct((b, l, NA_WIDTH), BF16),
        compiler_params=_cparams(1),
        name="na_context",
    )(q, k, v)


def _ffn_kernel(x_ref, f_ref, h_ref, n_ref, g1_ref, sh_ref, sc_ref, g2_ref, ng_ref, fg_ref,
                wo_ref, wg_ref, wu_ref, wd_ref, o_ref, *, final):
    y = (_dot(f_ref[...], wo_ref[0:MIX, :]) + _dot(h_ref[...], wo_ref[MIX:2 * MIX, :])
         + _dot(n_ref[...], wo_ref[2 * MIX:, :]))
    x1 = x_ref[...] + g1_ref[0] * y
    hn = x1 * lax.rsqrt(jnp.mean(x1 * x1, axis=-1, keepdims=True) + EPS) * ng_ref[...]
    hm = (hn * (1.0 + sc_ref[0]) + sh_ref[0]).astype(BF16)
    d_ff = wg_ref.shape[1]
    acc = jnp.zeros(x1.shape, F32)
    for j in range(d_ff // FF_CHUNK):
        cols = slice(j * FF_CHUNK, (j + 1) * FF_CHUNK)
        g = _dot(hm, wg_ref[:, cols])
        u = _dot(hm, wu_ref[:, cols])
        act = (g * _sigmoid(g) * u).astype(BF16)
        acc = acc + _dot(act, wd_ref[cols, :])
    x2 = x1 + g2_ref[0] * acc
    if final:
        x2 = x2 * lax.rsqrt(jnp.mean(x2 * x2, axis=-1, keepdims=True) + EPS) * fg_ref[...]
    o_ref[...] = x2


def _ffn(x2d, f, h, n, norm_g, final_g, modv, mod_base, rows_per_mod, wo, wg, wu, wd, tm, final):
    nt, d = x2d.shape
    d_ff = wg.shape[1]
    assert d_ff % FF_CHUNK == 0

    def mod_idx(j):
        if rows_per_mod is None:
            return lambda i: (mod_base * 6 + j, 0, 0)
        tpm = rows_per_mod // tm
        return lambda i: ((mod_base + i // tpm) * 6 + j, 0, 0)

    def tok(w):
        return pl.BlockSpec((tm, w), lambda i: (i, 0))

    def vec():
        return pl.BlockSpec((1, d), lambda i: (0, 0))

    return pl.pallas_call(
        functools.partial(_ffn_kernel, final=final),
        grid=(nt // tm,),
        in_specs=[tok(d), tok(MIX), tok(MIX), tok(NA_WIDTH),
                  pl.BlockSpec((1, 1, d), mod_idx(2)), pl.BlockSpec((1, 1, d), mod_idx(3)),
                  pl.BlockSpec((1, 1, d), mod_idx(4)), pl.BlockSpec((1, 1, d), mod_idx(5)),
                  vec(), vec(),
                  _const_spec(wo.shape), _const_spec(wg.shape), _const_spec(wu.shape),
                  _const_spec(wd.shape)],
        out_specs=tok(d),
        out_shape=jax.ShapeDtypeStruct((nt, d), F32),
        compiler_params=_cparams(1),
        name="outproj_ffn",
    )(x2d, f, h, n, modv, modv, modv, modv, norm_g.reshape(1, d), final_g.reshape(1, d),
      wo, wg, wu, wd)


def _token_tile(n):
    for tm in (512, 256, 128, 64, 32, 16, 8):
        if n % tm == 0:
            return tm
    raise ValueError(f"token count {n} is not a multiple of 8")


def kernel(x, c, ctx, c_ctx, w_mod, b_mod, norm1_g, w_in, fourier_w, hgrn_lb, hgrn_norm_g, na_rpb,
           w_out, norm2_g, w_ffn_gate, w_ffn_up, w_ffn_down, final_norm_g):
    b, s, d = x.shape
    l = ctx.shape[1]
    depth = w_mod.shape[0]
    assert b < MOD_ROWS and d == 2 * MIX + NA_WIDTH

    cc = jnp.zeros((MOD_ROWS, d), F32).at[:b].set(c).at[b].set(c_ctx)
    mod = _modulation(cc, w_mod, b_mod)
    modv = mod.reshape(depth * MOD_ROWS * 6, 1, d)

    lbp = jax.nn.softmax(hgrn_lb.astype(F32), axis=0)
    lower = jnp.cumsum(lbp, axis=0) - lbp[0:1]
    norm_g4 = jnp.tile(hgrn_norm_g.astype(F32), (1, HGRN_HEADS))[:, None, :]

    tab_s, tab_l = _dft_table(s), _dft_table(l)
    c64, s64 = _channel_dft_consts()
    eye_g = jnp.eye(FOURIER_GROUPS, dtype=F32)

    tm_s = _token_tile(s)
    tm_c = _token_tile(b * l)
    xl = x.reshape(b * s, d)
    xc = ctx.reshape(b * l, d)
    for li in range(depth):
        with_ctx = li < depth - 1
        final = li == depth - 1
        base = li * MOD_ROWS
        w_in_b = w_in[li].astype(BF16)
        ul = _inproj(xl, norm1_g[li], modv, base, s, w_in_b, tm_s)
        uc = _inproj(xc, norm1_g[li], modv, base + b, None, w_in_b, tm_c)
        ul = [a.reshape(b, s, -1) for a in ul]
        uc = [a.reshape(b, l, -1) for a in uc]

        w_bd = (eye_g[:, None, :, None] * fourier_w[li][:, :, None, :]).reshape(MIX, MIX)
        f_lat = _fourier(ul[0], tab_s, c64, s64, w_bd)
        h_lat, h_ctx = _hgrn(ul[1:6], uc[1:6], lower[li], norm_g4[li], with_ctx)
        n_lat = _na_latent(ul[6], ul[7], ul[8], uc[7], uc[8], _na_bias_table(na_rpb[li]))

        wo, wg = w_out[li].astype(BF16), w_ffn_gate[li].astype(BF16)
        wu, wd = w_ffn_up[li].astype(BF16), w_ffn_down[li].astype(BF16)
        if with_ctx:
            f_ctx = _fourier(uc[0], tab_l, c64, s64, w_bd)
            n_ctx = _na_context(uc[6], uc[7], uc[8])
            xc = _ffn(xc, f_ctx.reshape(b * l, -1), h_ctx.reshape(b * l, -1),
                      n_ctx.reshape(b * l, -1), norm2_g[li], final_norm_g, modv, base + b, None,
                      wo, wg, wu, wd, tm_c, False)
        xl = _ffn(xl, f_lat.reshape(b * s, -1), h_lat.reshape(b * s, -1), n_lat.reshape(b * s, -1),
                  norm2_g[li], final_norm_g, modv, base, s, wo, wg, wu, wd, tm_s, final)
    return xl.reshape(b, s, d)
```

```python
import functools

import jax
import jax.numpy as jnp
import numpy as np
from jax import lax
from jax.experimental import pallas as pl
from jax.experimental.pallas import tpu as pltpu

F32 = jnp.float32
BF16 = jnp.bfloat16

EPS = 1e-6
NEG_INF = -1e30
GRID_W = 64
FOURIER_GROUPS = 4
FOURIER_GD = 64
HGRN_DK = 64
HGRN_HEADS = 4
HGRN_CHUNK = 32
NA_HEAD_DIM = 64
NA_HEADS = 8
NA_KH = 8
NA_KW = 16
MIX = 256
NA_WIDTH = NA_HEADS * NA_HEAD_DIM
MOD_ROWS = 16
FF_CHUNK = 256
VMEM_LIMIT = 56 * 1024 * 1024


def _cparams(n_axes):
    return pltpu.CompilerParams(dimension_semantics=("arbitrary",) * n_axes,
                                vmem_limit_bytes=VMEM_LIMIT)


def _split2(a):
    hi = a.astype(BF16)
    lo = (a - hi.astype(F32)).astype(BF16)
    return hi, lo


def _dot(a, b):
    return jnp.dot(a, b, preferred_element_type=F32)


def _dot_nt(a, b):
    return lax.dot_general(a, b, (((1,), (1,)), ((), ())), preferred_element_type=F32)


def _dot_tn(a, b):
    return lax.dot_general(a, b, (((0,), (0,)), ((), ())), preferred_element_type=F32)


def _dot_hilo(a, b):
    ah, al = _split2(a)
    bh, bl = _split2(b)
    return _dot(ah, bh) + _dot(al, bh) + _dot(ah, bl)


def _sigmoid(z):
    return 1.0 / (1.0 + jnp.exp(-z))


def _const_spec(shape):
    nd = len(shape)
    return pl.BlockSpec(shape, lambda *_: (0,) * nd, pipeline_mode=pl.Buffered(1))


def _mod_kernel(c_ref, w_ref, b_ref, o_ref):
    c = c_ref[...]
    s = c * _sigmoid(c)
    o_ref[0] = _dot_hilo(s, w_ref[0]) + b_ref[0]


def _modulation(cc, w_mod, b_mod):
    depth, d, n = w_mod.shape
    tn = 1024
    return pl.pallas_call(
        _mod_kernel,
        grid=(depth, n // tn),
        in_specs=[pl.BlockSpec((MOD_ROWS, d), lambda l, j: (0, 0)),
                  pl.BlockSpec((1, d, tn), lambda l, j: (l, 0, j)),
                  pl.BlockSpec((1, 1, tn), lambda l, j: (l, 0, j))],
        out_specs=pl.BlockSpec((1, MOD_ROWS, tn), lambda l, j: (l, 0, j)),
        out_shape=jax.ShapeDtypeStruct((depth, MOD_ROWS, n), F32),
        compiler_params=_cparams(2),
        name="modulation",
    )(cc, w_mod, b_mod.reshape(depth, 1, n))


def _inproj_kernel(x_ref, g_ref, sh_ref, sc_ref, w_ref,
                   uf_ref, hq_ref, zf_ref, zb_ref, hi_ref, hg_ref, q_ref, k_ref, v_ref):
    x = x_ref[...]
    y = x * lax.rsqrt(jnp.mean(x * x, axis=-1, keepdims=True) + EPS) * g_ref[...]
    a = (y * (1.0 + sc_ref[0]) + sh_ref[0]).astype(BF16)
    col = 0
    for ref in (uf_ref, hq_ref, zf_ref, zb_ref, hi_ref, hg_ref):
        ref[...] = _dot(a, w_ref[:, col:col + MIX]).astype(ref.dtype)
        col += MIX
    scale = NA_HEAD_DIM ** -0.5
    q_ref[...] = (_dot(a, w_ref[:, col:col + NA_WIDTH]) * scale).astype(BF16)
    col += NA_WIDTH
    k_ref[...] = _dot(a, w_ref[:, col:col + NA_WIDTH]).astype(BF16)
    col += NA_WIDTH
    v_ref[...] = _dot(a, w_ref[:, col:col + NA_WIDTH]).astype(BF16)


def _inproj(x2d, norm_g, modv, mod_base, rows_per_mod, w_in_b, tm):
    n, d = x2d.shape
    in_w = w_in_b.shape[1]

    def mod_idx(j):
        if rows_per_mod is None:
            return lambda i: (mod_base * 6 + j, 0, 0)
        tpm = rows_per_mod // tm
        return lambda i: ((mod_base + i // tpm) * 6 + j, 0, 0)

    def tok(w):
        return pl.BlockSpec((tm, w), lambda i: (i, 0))

    outs = [(MIX, BF16), (MIX, BF16), (MIX, F32), (MIX, F32), (MIX, BF16), (MIX, BF16),
            (NA_WIDTH, BF16), (NA_WIDTH, BF16), (NA_WIDTH, BF16)]
    return pl.pallas_call(
        _inproj_kernel,
        grid=(n // tm,),
        in_specs=[tok(d),
                  pl.BlockSpec((1, d), lambda i: (0, 0)),
                  pl.BlockSpec((1, 1, d), mod_idx(0)),
                  pl.BlockSpec((1, 1, d), mod_idx(1)),
                  _const_spec((d, in_w))],
        out_specs=[tok(w) for w, _ in outs],
        out_shape=[jax.ShapeDtypeStruct((n, w), dt) for w, dt in outs],
        compiler_params=_cparams(1),
        name="inproj",
    )(x2d, norm_g.reshape(1, d), modv, modv, w_in_b)


def _dft_table(t_len):
    kb = min(64, t_len)
    assert t_len % kb == 0
    t = jnp.arange(t_len // 2, dtype=jnp.int32)[None, :]
    k_hi = jnp.arange(t_len // kb, dtype=jnp.int32)[:, None] * kb
    k_lo = jnp.arange(kb, dtype=jnp.int32)[:, None]
    w = 2.0 * np.pi / t_len
    ang_hi = ((k_hi * t) % t_len).astype(F32) * w
    ang_lo = ((k_lo * t) % t_len).astype(F32) * w
    ch, sh = jnp.cos(ang_hi)[:, None, :], jnp.sin(ang_hi)[:, None, :]
    cl, sl = jnp.cos(ang_lo)[None, :, :], jnp.sin(ang_lo)[None, :, :]
    cos = (ch * cl - sh * sl).reshape(t_len, t_len // 2)
    sin = (sh * cl + ch * sl).reshape(t_len, t_len // 2)
    return jnp.concatenate([cos, sin], axis=1).astype(BF16)


def _channel_dft_consts():
    cd = np.outer(np.arange(FOURIER_GD), np.arange(FOURIER_GD)) % FOURIER_GD
    ang = 2.0 * np.pi * cd / FOURIER_GD
    eye = np.eye(FOURIER_GROUPS)
    return (jnp.asarray(np.kron(eye, np.cos(ang)), F32),
            jnp.asarray(np.kron(eye, np.sin(ang)), F32))


def _fourier_kernel(u_ref, tab_ref, c64_ref, s64_ref, w_ref, o_ref, pq_ref, pm_ref, *, t_len, tm):
    half = t_len // 2
    bs = min(256, half)
    nb = half // bs
    bi = pl.program_id(1)

    @pl.when(pl.program_id(0) == 0)
    def _():
        norm = (FOURIER_GD * t_len) ** -0.5
        w = w_ref[...]
        a = (_dot_hilo(c64_ref[...], w) * norm).astype(BF16)
        b = (_dot_hilo(s64_ref[...], w) * (-norm)).astype(BF16)
        p_i = lax.broadcasted_iota(jnp.int32, (bs, bs), 0)
        q_i = lax.broadcasted_iota(jnp.int32, (bs, bs), 1)
        rev_shift = jnp.where(q_i == bs - p_i, 1.0, 0.0).astype(BF16)
        first_row = jnp.where(p_i + q_i == 0, 1.0, 0.0).astype(BF16)
        for i in range(nb):
            src = half + (nb - 1 - i) * bs
            r = _dot(rev_shift, u_ref[0, src:src + bs, :])
            if i > 0:
                r = r + _dot(first_row, u_ref[0, src + bs:src + 2 * bs, :])
            u = u_ref[0, i * bs:(i + 1) * bs, :].astype(F32)
            pq_ref[bi, i * bs:(i + 1) * bs, :] = _dot((u + r).astype(BF16), a).astype(BF16)
            pq_ref[bi, half + i * bs:half + (i + 1) * bs, :] = _dot((u - r).astype(BF16),
                                                                    b).astype(BF16)
        pm_ref[bi] = _dot(u_ref[0, half:half + 8, :], a)

    k = pl.program_id(0) * tm + lax.broadcasted_iota(jnp.int32, (tm, 1), 0)
    sign = (1 - 2 * (k & 1)).astype(F32)
    o_ref[0] = (_dot(tab_ref[...], pq_ref[bi]) + sign * pm_ref[bi, 0:1, :]).astype(BF16)


def _fourier(u, tab, c64, s64, w_bd):
    b, t_len, _ = u.shape
    assert t_len % 64 == 0
    tm = min(512, t_len)
    return pl.pallas_call(
        functools.partial(_fourier_kernel, t_len=t_len, tm=tm),
        grid=(t_len // tm, b),
        in_specs=[pl.BlockSpec((1, t_len, MIX), lambda j, i: (jnp.where(j == 0, i, b - 1), 0, 0)),
                  pl.BlockSpec((tm, t_len), lambda j, i: (j, 0)),
                  _const_spec((MIX, MIX)), _const_spec((MIX, MIX)), _const_spec((MIX, MIX))],
        out_specs=pl.BlockSpec((1, tm, MIX), lambda j, i: (i, j, 0)),
        out_shape=jax.ShapeDtypeStruct((b, t_len, MIX), BF16),
        scratch_shapes=[pltpu.VMEM((b, t_len, MIX), BF16), pltpu.VMEM((b, 8, MIX), F32)],
        compiler_params=_cparams(2),
        name="fourier",
    )(u, tab, c64, s64, w_bd)


def _hgrn_kernel(ql_ref, zfl_ref, zbl_ref, il_ref, gl_ref,
                 qc_ref, zfc_ref, zbc_ref, ic_ref, gc_ref, lb_ref, ng_ref,
                 *rest, n_lat, n_ctx, with_ctx):
    if with_ctx:
        ol_ref, oc_ref, st_ref, qd_ref, ks_ref, vt_ref, eal_ref, os_ref = rest
    else:
        ol_ref, st_ref, qd_ref, ks_ref, vt_ref, eal_ref, os_ref = rest
        oc_ref = None
    C = HGRN_CHUNK
    P2 = 2 * C
    W = MIX
    lane_head =lax.broadcasted_iota(jnp.int32, (1, W), 1) // HGRN_DK
    head_mask = [lane_head == h for h in range(HGRN_HEADS)]
    bd = (lax.broadcasted_iota(jnp.int32, (W, W), 0) // HGRN_DK
          == lax.broadcasted_iota(jnp.int32, (W, W), 1) // HGRN_DK)
    bd_ones = jnp.where(bd, 1.0, 0.0).astype(BF16)

    def stack_heads(x):
        return jnp.concatenate([jnp.where(m, x, jnp.zeros_like(x)) for m in head_mask], axis=0)

    def group_chunks(n_chunks):
        return 4 if n_chunks % 4 == 0 else 2

    def phase_a(q_ref, zf_ref, zb_ref, i_ref, n_chunks, need_o):
        gc = group_chunks(n_chunks)
        gt = gc * C
        t_i = lax.broadcasted_iota(jnp.int32, (gt, gt), 0)
        s_i = lax.broadcasted_iota(jnp.int32, (gt, gt), 1)
        same = (t_i // C) == (s_i // C)
        allow = (same & (s_i <= t_i), same & (s_i >= t_i))
        allow_b16 = tuple(jnp.where(m, 1.0, 0.0).astype(BF16) for m in allow)
        t4 = lax.broadcasted_iota(jnp.int32, (gt, HGRN_HEADS * gt), 0)
        s4 = lax.broadcasted_iota(jnp.int32, (gt, HGRN_HEADS * gt), 1) % gt
        same4 = (t4 // C) == (s4 // C)
        allow4 = (same4 & (s4 <= t4), same4 & (s4 >= t4))
        pair4 = (t4 // P2) == (s4 // P2)
        cross4 = (pair4 & (t4 // C == s4 // C + 1), pair4 & (t4 // C + 1 == s4 // C))
        ones_row = jnp.ones((C, W), F32)

        def body(g, carry):
            rows = pl.ds(pl.multiple_of(g * gt, gt), gt)
            v = i_ref[0, rows, :]
            vs = stack_heads(v) if need_o else None
            for j in range(gc // 2):
                vt_ref[g * (gc // 2) + j] = v[j * P2:(j + 1) * P2, :].T
            fs, parts = [], []
            for d, z_ref in ((0, zf_ref), (1, zb_ref)):
                lb = lb_ref[d:d + 1, :]
                f = lb + (1.0 - lb) * _sigmoid(z_ref[0, rows, :])
                lf = jnp.log(f)
                h1 = lf.astype(BF16)
                r1 = lf - h1.astype(F32)
                h2 = r1.astype(BF16)
                h3 = (r1 - h2.astype(F32)).astype(BF16)
                fs.append(f)
                parts.append((h1, h2, h3))
            a_s = [_dot(allow_b16[d], parts[d][0]) + _dot(allow_b16[d], parts[d][1])
                   + _dot(allow_b16[d], parts[d][2]) for d in (0, 1)]
            scs = []
            for d in (0, 1):
                a = a_s[d]
                kd = (1.0 - fs[d]) * jnp.exp(-a)
                ends = [a[c * C + C - 1:c * C + C, :] if d == 0 else a[c * C:c * C + 1, :]
                        for c in range(gc)]
                eal = [jnp.exp(e) for e in ends]
                ebc = [jnp.broadcast_to(e, (C, W)) for e in eal]
                ks = kd * jnp.concatenate(ebc, axis=0)
                is_first = [(c % 2 == 0) == (d == 0) for c in range(gc)]
                k_mul = jnp.concatenate([ebc[c ^ 1] if is_first[c] else ones_row
                                         for c in range(gc)], axis=0)
                ks_ref[d, rows, :] = (ks * k_mul).astype(BF16)
                n8 = (gc // 2) * 8
                eal_ref[d, pl.ds(pl.multiple_of(g * n8, n8), n8), :] = jnp.concatenate(
                    [jnp.broadcast_to(eal[2 * j] * eal[2 * j + 1], (8, W)) for j in range(gc // 2)],
                    axis=0)
                if need_o:
                    qd = q_ref[0, rows, :].astype(F32) * jnp.exp(a)
                    q_mul = jnp.concatenate([ones_row if is_first[c] else ebc[c ^ 1]
                                             for c in range(gc)], axis=0)
                    qd2 = (qd * q_mul).astype(BF16)
                    qd_ref[d, rows, :] = qd2
                    s2 = _dot_nt(jnp.concatenate([qd.astype(BF16), qd2], axis=0),
                                 stack_heads(kd.astype(BF16)))
                    scs.append(jnp.where(allow4[d], s2[:gt],
                                         jnp.where(cross4[d], s2[gt:], 0.0)).astype(BF16))
            if need_o:
                os_ref[rows, :] = _dot(scs[0], vs) + _dot(scs[1], vs)
            return carry

        lax.fori_loop(0, n_chunks // gc, body, 0, unroll=4)

    def phase_b(n_chunks, need_o):
        n_steps = n_chunks // 2
        gc = next(t for t in (4, 2, 1) if n_steps % t == 0)
        pw = 2 * HGRN_DK
        pairs = [slice(p * pw, (p + 1) * pw) for p in range(HGRN_HEADS // 2)]
        same_head = (lax.broadcasted_iota(jnp.int32, (pw, pw), 0) // HGRN_DK
                     == lax.broadcasted_iota(jnp.int32, (pw, pw), 1) // HGRN_DK)

        def step_rows(m):
            return pl.ds(pl.multiple_of(m * P2, P2), P2)

        def body(g, carry):
            idx = [[g * gc + c, n_steps - 1 - (g * gc + c)] for c in range(gc)]

            def updates(c):
                out = []
                for d, m in enumerate(idx[c]):
                    vt = vt_ref[m]
                    ks = ks_ref[d, step_rows(m), :]
                    out.append([jnp.where(same_head, _dot(vt[s, :], ks[:, s]), 0.0) for s in pairs])
                return out

            upd = {0: updates(0)}
            inters = []
            for c in range(gc):
                if c + 1 < gc:
                    upd[c + 1] = updates(c + 1)
                for d in (0, 1):
                    m = idx[c][d]
                    rows = step_rows(m)
                    eal = eal_ref[d, pl.ds(pl.multiple_of(m * 8, 8), 8), :][0:1, :]
                    sts = [st_ref[d, p] for p in range(len(pairs))]
                    if need_o:
                        qd = qd_ref[d, rows, :]
                        inter = [_dot_nt(qd[:, s], st.astype(BF16)) for s, st in zip(pairs, sts)]
                        inters.append((rows, jnp.concatenate(inter, axis=1)))
                    for p, s in enumerate(pairs):
                        st_ref[d, p] = sts[p] * eal[:, s] + upd[c][d][p]
            for rows, inter in inters:
                os_ref[rows, :] = os_ref[rows, :] + inter
            return carry

        lax.fori_loop(0, n_steps // gc, body, 0, unroll=2)

    def phase_c(g_ref, o_ref, n_rows):
        rt = next(t for t in (256, 128, 64) if n_rows % t == 0)

        def body(t, carry):
            rows = pl.ds(pl.multiple_of(t * rt, rt), rt)
            o = os_ref[rows, :]
            hi, lo = _split2(o * o)
            ms = (_dot(hi, bd_ones) + _dot(lo, bd_ones)) * (1.0 / HGRN_DK)
            y = o * lax.rsqrt(ms + EPS) * ng_ref[...]
            g = g_ref[0, rows, :].astype(F32)
            o_ref[0, rows, :] = (y * (g * _sigmoid(g))).astype(BF16)
            return carry

        lax.fori_loop(0, n_rows // rt, body, 0)

    st_ref[...] = jnp.zeros_like(st_ref)
    phase_a(qc_ref, zfc_ref, zbc_ref, ic_ref, n_ctx, with_ctx)
    phase_b(n_ctx, with_ctx)
    if with_ctx:
        phase_c(gc_ref, oc_ref, n_ctx * C)
    phase_a(ql_ref, zfl_ref, zbl_ref, il_ref, n_lat, True)
    phase_b(n_lat, True)
    phase_c(gl_ref, ol_ref, n_lat * C)


def _hgrn(lat, ctx, lb, norm_g4, with_ctx):
    b, s, _ = lat[0].shape
    l = ctx[0].shape[1]
    assert s % (2 * HGRN_CHUNK) == 0 and l % (2 * HGRN_CHUNK) == 0

    def seq(t):
        return pl.BlockSpec((1, t, MIX), lambda i: (i, 0, 0))

    out_specs = [pl.BlockSpec((1, s, MIX), lambda i: (i, 0, 0))]
    out_shape = [jax.ShapeDtypeStruct((b, s, MIX), BF16)]
    t_max = max(s, l)
    scratch = [pltpu.VMEM((2, HGRN_HEADS // 2, 2 * HGRN_DK, 2 * HGRN_DK), F32),
               pltpu.VMEM((2, t_max, MIX), BF16),
               pltpu.VMEM((2, t_max, MIX), BF16),
               pltpu.VMEM((t_max // (2 * HGRN_CHUNK), MIX, 2 * HGRN_CHUNK), BF16),
               pltpu.VMEM((2, t_max // (2 * HGRN_CHUNK) * 8, MIX), F32),
               pltpu.VMEM((t_max, MIX), F32)]
    if with_ctx:
        out_specs.append(pl.BlockSpec((1, l, MIX), lambda i: (i, 0, 0)))
        out_shape.append(jax.ShapeDtypeStruct((b, l, MIX), BF16))
    res = pl.pallas_call(
        functools.partial(_hgrn_kernel, n_lat=s // HGRN_CHUNK, n_ctx=l // HGRN_CHUNK,
                          with_ctx=with_ctx),
        grid=(b,),
        in_specs=[seq(s)] * 5 + [seq(l)] * 5 + [_const_spec((2, MIX)), _const_spec((1, MIX))],
        out_specs=out_specs,
        out_shape=out_shape,
        scratch_shapes=scratch,
        compiler_params=_cparams(1),
        name="hgrn",
    )(*lat, *ctx, lb, norm_g4)
    return (res[0], res[1]) if with_ctx else (res[0], None)


def _na_bias_table(rpb):
    n_h, nr, nc = rpb.shape
    w = GRID_W
    assert 2 * w == 128 and nc <= w
    padded = jnp.pad(rpb.astype(F32), ((0, 0), (0, 16 - nr), (0, 2 * w - nc)))
    return pl.pallas_call(
        _na_bias_kernel,
        grid=(n_h // 2,),
        in_specs=[pl.BlockSpec((2, 16, 2 * w), lambda i: (i, 0, 0))],
        out_specs=pl.BlockSpec((1, nr - 1, 2 * w, 2 * w), lambda i: (i, 0, 0, 0)),
        out_shape=jax.ShapeDtypeStruct((n_h // 2, nr - 1, 2 * w, 2 * w), F32),
        compiler_params=_cparams(1),
        name="na_bias",
    )(padded)


def _na_bias_kernel(rpb_ref, o_ref):
    w = GRID_W
    q = lax.broadcasted_iota(jnp.int32, (w, 2 * w), 0)
    lane = lax.broadcasted_iota(jnp.int32, (w, 2 * w), 1)
    kc = lane % w
    win = jnp.clip(q - NA_KW // 2, 0, w - NA_KW)
    valid = (kc >= win) & (kc < win + NA_KW)
    lower = lane < w
    shift_lo = 2 * w - (NA_KW - 1)
    shift_hi = shift_lo + w
    for h in range(2):
        for j in range(2 * NA_KH - 2):
            x0 = jnp.broadcast_to(rpb_ref[h, j:j + 1, :], (w, 2 * w))
            x1 = jnp.broadcast_to(rpb_ref[h, j + 1:j + 2, :], (w, 2 * w))
            t0 = pltpu.roll(x0, shift_lo, 1, stride=1, stride_axis=0)
            t1 = pltpu.roll(x1, shift_hi % (2 * w), 1, stride=1, stride_axis=0)
            o_ref[0, j, h * w:(h + 1) * w, :] = jnp.where(valid, jnp.where(lower, t0, t1), NEG_INF)


def _softmax_pv(s_list, v_list):
    m = functools.reduce(jnp.maximum, [jnp.max(s, axis=-1, keepdims=True) for s in s_list])
    p_list = [jnp.exp(s - m) for s in s_list]
    l = functools.reduce(jnp.add, [jnp.sum(p, axis=-1, keepdims=True) for p in p_list])
    o = functools.reduce(jnp.add, [_dot(p.astype(BF16), v) for p, v in zip(p_list, v_list)])
    return o * (1.0 / l)


NA_ROWS_PER_STEP = 16


def _na_kernel(q_ref, k_ref, v_ref, kc_ref, vc_ref, bias_ref, o_ref, *, rows):
    lane = lax.broadcasted_iota(jnp.int32, (1, 2 * NA_HEAD_DIM), 1)
    first = lane < NA_HEAD_DIM
    kc = kc_ref[0]
    vc = vc_ref[0]
    win = NA_KH * GRID_W
    nq = 2 * GRID_W

    def body(g, carry):
        qs, kws, vws, dis, qrows = [], [], [], [], []
        for j in range(NA_ROWS_PER_STEP):
            r = g * NA_ROWS_PER_STEP + j
            rs = jnp.clip(r - NA_KH // 2, 0, rows - NA_KH)
            dis.append(r - rs)
            qrows.append(pl.ds(pl.multiple_of(r * GRID_W, GRID_W), GRID_W))
            krows = pl.ds(pl.multiple_of(rs * GRID_W, GRID_W), win)
            q = q_ref[0, qrows[-1], :]
            qs += [jnp.where(first, q, jnp.zeros_like(q)), jnp.where(first, jnp.zeros_like(q), q)]
            kws.append(k_ref[0, krows, :])
            vws.append(v_ref[0, krows, :])
        qs = jnp.concatenate(qs, axis=0)
        sx = _dot_nt(qs, kc)
        pws, pxs, inv_ls = [], [], []
        for j in range(NA_ROWS_PER_STEP):
            ri0 = NA_KH - 1 - dis[j]
            bias = jnp.concatenate([bias_ref[0, ri0 + 2 * c] for c in range(NA_KH // 2)], axis=1)
            sw = _dot_nt(qs[j * nq:(j + 1) * nq], kws[j]) + bias
            sxj = sx[j * nq:(j + 1) * nq]
            m = jnp.maximum(jnp.max(sw, axis=-1, keepdims=True), jnp.max(sxj, axis=-1, keepdims=True))
            pw = jnp.exp(sw - m)
            px = jnp.exp(sxj - m)
            inv_ls.append(1.0 / (jnp.sum(pw, axis=-1, keepdims=True) + jnp.sum(px, axis=-1, keepdims=True)))
            pws.append(pw.astype(BF16))
            pxs.append(px.astype(BF16))
        ows = [_dot(pws[j], vws[j]) for j in range(NA_ROWS_PER_STEP)]
        oc = _dot(jnp.concatenate(pxs, axis=0), vc)
        for j in range(NA_ROWS_PER_STEP):
            o = (ows[j] + oc[j * nq:(j + 1) * nq]) * inv_ls[j]
            o_ref[0, qrows[j], :] = jnp.where(first, o[:GRID_W], o[GRID_W:]).astype(BF16)
        return carry

    lax.fori_loop(0, rows // NA_ROWS_PER_STEP, body, 0)


def _na_latent(q, k, v, kc, vc, bias):
    b, s, _ = q.shape
    l = kc.shape[1]
    rows = s // GRID_W
    assert s % GRID_W == 0 and rows >= NA_KH and rows % NA_ROWS_PER_STEP == 0
    hp = 2 * NA_HEAD_DIM

    def seq(t):
        return pl.BlockSpec((1, t, hp), lambda i, j: (i, 0, j))

    return pl.pallas_call(
        functools.partial(_na_kernel, rows=rows),
        grid=(b, NA_HEADS // 2),
        in_specs=[seq(s), seq(s), seq(s), seq(l), seq(l),
                  pl.BlockSpec((1, 2 * NA_KH - 2, 2 * GRID_W, 2 * GRID_W),
                               lambda i, j: (j, 0, 0, 0))],
        out_specs=seq(s),
        out_shape=jax.ShapeDtypeStruct((b, s, NA_WIDTH), BF16),
        compiler_params=_cparams(2),
        name="na_latent",
    )(q, k, v, kc, vc, bias)


def _na_ctx_kernel(q_ref, k_ref, v_ref, o_ref):
    lane = lax.broadcasted_iota(jnp.int32, (1, 2 * NA_HEAD_DIM), 1)
    first = lane < NA_HEAD_DIM
    for p in range(NA_HEADS // 2):
        cols = slice(p * 2 * NA_HEAD_DIM, (p + 1) * 2 * NA_HEAD_DIM)
        q, k, v = q_ref[0, :, cols], k_ref[0, :, cols], v_ref[0, :, cols]
        outs = []
        for h in range(2):
            qh = jnp.where(first if h == 0 else ~first, q, jnp.zeros_like(q))
            outs.append(_softmax_pv([_dot_nt(qh, k)], [v]))
        o_ref[0, :, cols] = jnp.where(first, outs[0], outs[1]).astype(BF16)


def _na_context(q, k, v):
    b, l, _ = q.shape
    spec = pl.BlockSpec((1, l, NA_WIDTH), lambda i: (i, 0, 0))
    return pl.pallas_call(
        _na_ctx_kernel,
        grid=(b,),
        in_specs=[spec, spec, spec],
        out_specs=spec,
        out_shape=jax.ShapeDtypeStruct((b, l, NA_WIDTH), BF16),
        compiler_params=_cparams(1),
        name="na_context",
    )(q, k, v)


def _ffn_kernel(x_ref, f_ref, h_ref, n_ref, g1_ref, sh_ref, sc_ref, g2_ref, ng_ref, fg_ref,
                wo_ref, wg_ref, wu_ref, wd_ref, o_ref, *, final):
    y = (_dot(f_ref[...], wo_ref[0:MIX, :]) + _dot(h_ref[...], wo_ref[MIX:2 * MIX, :])
         + _dot(n_ref[...], wo_ref[2 * MIX:, :]))
    x1 = x_ref[...] + g1_ref[0] * y
    hn = x1 * lax.rsqrt(jnp.mean(x1 * x1, axis=-1, keepdims=True) + EPS) * ng_ref[...]
    hm = (hn * (1.0 + sc_ref[0]) + sh_ref[0]).astype(BF16)
    d_ff = wg_ref.shape[1]
    acc = jnp.zeros(x1.shape, F32)
    for j in range(d_ff // FF_CHUNK):
        cols = slice(j * FF_CHUNK, (j + 1) * FF_CHUNK)
        g = _dot(hm, wg_ref[:, cols])
        u = _dot(hm, wu_ref[:, cols])
        act = (g * _sigmoid(g) * u).astype(BF16)
        acc = acc + _dot(act, wd_ref[cols, :])
    x2 = x1 + g2_ref[0] * acc
    if final:
        x2 = x2 * lax.rsqrt(jnp.mean(x2 * x2, axis=-1, keepdims=True) + EPS) * fg_ref[...]
    o_ref[...] = x2


def _ffn(x2d, f, h, n, norm_g, final_g, modv, mod_base, rows_per_mod, wo, wg, wu, wd, tm, final):
    nt, d = x2d.shape
    d_ff = wg.shape[1]
    assert d_ff % FF_CHUNK == 0

    def mod_idx(j):
        if rows_per_mod is None:
            return lambda i: (mod_base * 6 + j, 0, 0)
        tpm = rows_per_mod // tm
        return lambda i: ((mod_base + i // tpm) * 6 + j, 0, 0)

    def tok(w):
        return pl.BlockSpec((tm, w), lambda i: (i, 0))

    def vec():
        return pl.BlockSpec((1, d), lambda i: (0, 0))

    return pl.pallas_call(
        functools.partial(_ffn_kernel, final=final),
        grid=(nt // tm,),
        in_specs=[tok(d), tok(MIX), tok(MIX), tok(NA_WIDTH),
                  pl.BlockSpec((1, 1, d), mod_idx(2)), pl.BlockSpec((1, 1, d), mod_idx(3)),
                  pl.BlockSpec((1, 1, d), mod_idx(4)), pl.BlockSpec((1, 1, d), mod_idx(5)),
                  vec(), vec(),
                  _const_spec(wo.shape), _const_spec(wg.shape), _const_spec(wu.shape),
                  _const_spec(wd.shape)],
        out_specs=tok(d),
        out_shape=jax.ShapeDtypeStruct((nt, d), F32),
        compiler_params=_cparams(1),
        name="outproj_ffn",
    )(x2d, f, h, n, modv, modv, modv, modv, norm_g.reshape(1, d), final_g.reshape(1, d),
      wo, wg, wu, wd)


def _token_tile(n, largest):
    for tm in (1024, 512, 256, 128, 64, 32, 16, 8):
        if tm <= largest and n % tm == 0:
            return tm
    raise ValueError(f"token count {n} is not a multiple of 8")


def kernel(x, c, ctx, c_ctx, w_mod, b_mod, norm1_g, w_in, fourier_w, hgrn_lb, hgrn_norm_g, na_rpb,
           w_out, norm2_g, w_ffn_gate, w_ffn_up, w_ffn_down, final_norm_g):
    b, s, d = x.shape
    l = ctx.shape[1]
    depth = w_mod.shape[0]
    assert b < MOD_ROWS and d == 2 * MIX + NA_WIDTH

    cc = jnp.zeros((MOD_ROWS, d), F32).at[:b].set(c).at[b].set(c_ctx)
    mod = _modulation(cc, w_mod, b_mod)
    modv = mod.reshape(depth * MOD_ROWS * 6, 1, d)

    lbp = jax.nn.softmax(hgrn_lb.astype(F32), axis=0)
    lower = jnp.cumsum(lbp, axis=0) - lbp[0:1]
    norm_g4 = jnp.tile(hgrn_norm_g.astype(F32), (1, HGRN_HEADS))[:, None, :]

    tab_s, tab_l = _dft_table(s), _dft_table(l)
    c64, s64 = _channel_dft_consts()
    eye_g = jnp.eye(FOURIER_GROUPS, dtype=F32)

    tm_s, tm_c = _token_tile(s, 512), _token_tile(b * l, 512)
    tm_in_s, tm_in_c = _token_tile(s, 1024), _token_tile(b * l, 1024)
    xl = x.reshape(b * s, d)
    xc = ctx.reshape(b * l, d)
    for li in range(depth):
        with_ctx = li < depth - 1
        final = li == depth - 1
        base = li * MOD_ROWS
        w_in_b = w_in[li].astype(BF16)
        ul = _inproj(xl, norm1_g[li], modv, base, s, w_in_b, tm_in_s)
        uc = _inproj(xc, norm1_g[li], modv, base + b, None, w_in_b, tm_in_c)
        ul = [a.reshape(b, s, -1) for a in ul]
        uc = [a.reshape(b, l, -1) for a in uc]

        w_bd = (eye_g[:, None, :, None] * fourier_w[li][:, :, None, :]).reshape(MIX, MIX)
        f_lat = _fourier(ul[0], tab_s, c64, s64, w_bd)
        h_lat, h_ctx = _hgrn(ul[1:6], uc[1:6], lower[li], norm_g4[li], with_ctx)
        n_lat = _na_latent(ul[6], ul[7], ul[8], uc[7], uc[8], _na_bias_table(na_rpb[li]))

        wo, wg = w_out[li].astype(BF16), w_ffn_gate[li].astype(BF16)
        wu, wd = w_ffn_up[li].astype(BF16), w_ffn_down[li].astype(BF16)
        if with_ctx:
            f_ctx = _fourier(uc[0], tab_l, c64, s64, w_bd)
            n_ctx = _na_context(uc[6], uc[7], uc[8])
            xc = _ffn(xc, f_ctx.reshape(b * l, -1), h_ctx.reshape(b * l, -1),
                      n_ctx.reshape(b * l, -1), norm2_g[li], final_norm_g, modv, base + b, None,
                      wo, wg, wu, wd, tm_c, False)
        xl = _ffn(xl, f_lat.reshape(b * s, -1), h_lat.reshape(b * s, -1), n_lat.reshape(b * s, -1),
                  norm2_g[li], final_norm_g, modv, base, s, wo, wg, wu, wd, tm_s, final)
    return xl.reshape(b, s, d)
```

```python
import functools

import jax
import jax.numpy as jnp
import numpy as np
from jax import lax
from jax.experimental import pallas as pl
from jax.experimental.pallas import tpu as pltpu

F32 = jnp.float32
BF16 = jnp.bfloat16

EPS = 1e-6
NEG_INF = -1e30
GRID_W = 64
FOURIER_GROUPS = 4
FOURIER_GD = 64
HGRN_DK = 64
HGRN_HEADS = 4
HGRN_CHUNK = 32
NA_HEAD_DIM = 64
NA_HEADS = 8
NA_KH = 8
NA_KW = 16
MIX = 256
NA_WIDTH = NA_HEADS * NA_HEAD_DIM
MOD_ROWS = 16
FF_CHUNK = 256
VMEM_LIMIT = 56 * 1024 * 1024


def _cparams(n_axes):
    return pltpu.CompilerParams(dimension_semantics=("arbitrary",) * n_axes,
                                vmem_limit_bytes=VMEM_LIMIT)


def _split2(a):
    hi = a.astype(BF16)
    lo = (a - hi.astype(F32)).astype(BF16)
    return hi, lo


def _dot(a, b):
    return jnp.dot(a, b, preferred_element_type=F32)


def _dot_nt(a, b):
    return lax.dot_general(a, b, (((1,), (1,)), ((), ())), preferred_element_type=F32)


def _dot_hilo(a, b):
    ah, al = _split2(a)
    bh, bl = _split2(b)
    return _dot(ah, bh) + _dot(al, bh) + _dot(ah, bl)


def _sigmoid(z):
    return 1.0 / (1.0 + jnp.exp(-z))


def _const_spec(shape):
    nd = len(shape)
    return pl.BlockSpec(shape, lambda *_: (0,) * nd, pipeline_mode=pl.Buffered(1))


def _mod_kernel(c_ref, w_ref, b_ref, o_ref):
    c = c_ref[...]
    s = c * _sigmoid(c)
    o_ref[0] = _dot_hilo(s, w_ref[0]) + b_ref[0]


def _modulation(cc, w_mod, b_mod):
    depth, d, n = w_mod.shape
    tn = 1024
    return pl.pallas_call(
        _mod_kernel,
        grid=(depth, n // tn),
        in_specs=[pl.BlockSpec((MOD_ROWS, d), lambda l, j: (0, 0)),
                  pl.BlockSpec((1, d, tn), lambda l, j: (l, 0, j)),
                  pl.BlockSpec((1, 1, tn), lambda l, j: (l, 0, j))],
        out_specs=pl.BlockSpec((1, MOD_ROWS, tn), lambda l, j: (l, 0, j)),
        out_shape=jax.ShapeDtypeStruct((depth, MOD_ROWS, n), F32),
        compiler_params=_cparams(2),
        name="modulation",
    )(cc, w_mod, b_mod.reshape(depth, 1, n))


def _inproj_kernel(x_ref, g_ref, sh_ref, sc_ref, w_ref,
                   uf_ref, hq_ref, zf_ref, zb_ref, hi_ref, hg_ref, q_ref, k_ref, v_ref):
    x = x_ref[...]
    y = x * lax.rsqrt(jnp.mean(x * x, axis=-1, keepdims=True) + EPS) * g_ref[...]
    a = (y * (1.0 + sc_ref[0]) + sh_ref[0]).astype(BF16)
    col = 0
    for ref in (uf_ref, hq_ref, zf_ref, zb_ref, hi_ref, hg_ref):
        ref[...] = _dot(a, w_ref[:, col:col + MIX]).astype(ref.dtype)
        col += MIX
    scale = NA_HEAD_DIM ** -0.5
    q_ref[...] = (_dot(a, w_ref[:, col:col + NA_WIDTH]) * scale).astype(BF16)
    col += NA_WIDTH
    k_ref[...] = _dot(a, w_ref[:, col:col + NA_WIDTH]).astype(BF16)
    col += NA_WIDTH
    v_ref[...] = _dot(a, w_ref[:, col:col + NA_WIDTH]).astype(BF16)


def _inproj(x2d, norm_g, modv, mod_base, rows_per_mod, w_in_b, tm):
    n, d = x2d.shape
    in_w = w_in_b.shape[1]

    def mod_idx(j):
        if rows_per_mod is None:
            return lambda i: (mod_base * 6 + j, 0, 0)
        tpm = rows_per_mod // tm
        return lambda i: ((mod_base + i // tpm) * 6 + j, 0, 0)

    def tok(w):
        return pl.BlockSpec((tm, w), lambda i: (i, 0))

    outs = [(MIX, BF16), (MIX, BF16), (MIX, F32), (MIX, F32), (MIX, BF16), (MIX, BF16),
            (NA_WIDTH, BF16), (NA_WIDTH, BF16), (NA_WIDTH, BF16)]
    return pl.pallas_call(
        _inproj_kernel,
        grid=(n // tm,),
        in_specs=[tok(d),
                  pl.BlockSpec((1, d), lambda i: (0, 0)),
                  pl.BlockSpec((1, 1, d), mod_idx(0)),
                  pl.BlockSpec((1, 1, d), mod_idx(1)),
                  _const_spec((d, in_w))],
        out_specs=[tok(w) for w, _ in outs],
        out_shape=[jax.ShapeDtypeStruct((n, w), dt) for w, dt in outs],
        compiler_params=_cparams(1),
        name="inproj",
    )(x2d, norm_g.reshape(1, d), modv, modv, w_in_b)


def _dft_table(t_len):
    kb = min(64, t_len)
    assert t_len % kb == 0
    t = jnp.arange(t_len // 2, dtype=jnp.int32)[None, :]
    k_hi = jnp.arange(t_len // kb, dtype=jnp.int32)[:, None] * kb
    k_lo = jnp.arange(kb, dtype=jnp.int32)[:, None]
    w = 2.0 * np.pi / t_len
    ang_hi = ((k_hi * t) % t_len).astype(F32) * w
    ang_lo = ((k_lo * t) % t_len).astype(F32) * w
    ch, sh = jnp.cos(ang_hi)[:, None, :], jnp.sin(ang_hi)[:, None, :]
    cl, sl = jnp.cos(ang_lo)[None, :, :], jnp.sin(ang_lo)[None, :, :]
    cos = (ch * cl - sh * sl).reshape(t_len, t_len // 2)
    sin = (sh * cl + ch * sl).reshape(t_len, t_len // 2)
    return jnp.concatenate([cos, sin], axis=1).astype(BF16)


def _channel_dft_consts():
    cd = np.outer(np.arange(FOURIER_GD), np.arange(FOURIER_GD)) % FOURIER_GD
    ang = 2.0 * np.pi * cd / FOURIER_GD
    eye = np.eye(FOURIER_GROUPS)
    return (jnp.asarray(np.kron(eye, np.cos(ang)), F32),
            jnp.asarray(np.kron(eye, np.sin(ang)), F32))


def _fourier_kernel(u_ref, tab_ref, c64_ref, s64_ref, w_ref, o_ref, pq_ref, pm_ref, *, t_len, tm):
    half = t_len // 2
    bs = min(256, half)
    nb = half // bs
    bi = pl.program_id(1)

    @pl.when(pl.program_id(0) == 0)
    def _():
        norm = (FOURIER_GD * t_len) ** -0.5
        w = w_ref[...]
        a = (_dot_hilo(c64_ref[...], w) * norm).astype(BF16)
        b = (_dot_hilo(s64_ref[...], w) * (-norm)).astype(BF16)
        p_i = lax.broadcasted_iota(jnp.int32, (bs, bs), 0)
        q_i = lax.broadcasted_iota(jnp.int32, (bs, bs), 1)
        rev_shift = jnp.where(q_i == bs - p_i, 1.0, 0.0).astype(BF16)
        first_row = jnp.where(p_i + q_i == 0, 1.0, 0.0).astype(BF16)
        for i in range(nb):
            src = half + (nb - 1 - i) * bs
            r = _dot(rev_shift, u_ref[0, src:src + bs, :])
            if i > 0:
                r = r + _dot(first_row, u_ref[0, src + bs:src + 2 * bs, :])
            u = u_ref[0, i * bs:(i + 1) * bs, :].astype(F32)
            pq_ref[bi, i * bs:(i + 1) * bs, :] = _dot((u + r).astype(BF16), a).astype(BF16)
            pq_ref[bi, half + i * bs:half + (i + 1) * bs, :] = _dot((u - r).astype(BF16),
                                                                    b).astype(BF16)
        pm_ref[bi] = _dot(u_ref[0, half:half + 8, :], a)

    k = pl.program_id(0) * tm + lax.broadcasted_iota(jnp.int32, (tm, 1), 0)
    sign = (1 - 2 * (k & 1)).astype(F32)
    o_ref[0] = (_dot(tab_ref[...], pq_ref[bi]) + sign * pm_ref[bi, 0:1, :]).astype(BF16)


def _fourier(u, tab, c64, s64, w_bd):
    b, t_len, _ = u.shape
    assert t_len % 64 == 0
    tm = min(512, t_len)
    return pl.pallas_call(
        functools.partial(_fourier_kernel, t_len=t_len, tm=tm),
        grid=(t_len // tm, b),
        in_specs=[pl.BlockSpec((1, t_len, MIX), lambda j, i: (jnp.where(j == 0, i, b - 1), 0, 0)),
                  pl.BlockSpec((tm, t_len), lambda j, i: (j, 0)),
                  _const_spec((MIX, MIX)), _const_spec((MIX, MIX)), _const_spec((MIX, MIX))],
        out_specs=pl.BlockSpec((1, tm, MIX), lambda j, i: (i, j, 0)),
        out_shape=jax.ShapeDtypeStruct((b, t_len, MIX), BF16),
        scratch_shapes=[pltpu.VMEM((b, t_len, MIX), BF16), pltpu.VMEM((b, 8, MIX), F32)],
        compiler_params=_cparams(2),
        name="fourier",
    )(u, tab, c64, s64, w_bd)


def _hgrn_kernel(ql_ref, zfl_ref, zbl_ref, il_ref, gl_ref,
                 qc_ref, zfc_ref, zbc_ref, ic_ref, gc_ref, lb_ref, ng_ref,
                 *rest, n_lat, n_ctx, with_ctx):
    if with_ctx:
        ol_ref, oc_ref, st_ref, qd_ref, ks_ref, vt_ref, eal_ref, os_ref = rest
    else:
        ol_ref, st_ref, qd_ref, ks_ref, vt_ref, eal_ref, os_ref = rest
        oc_ref = None
    C = HGRN_CHUNK
    P2 = 2 * C
    W = MIX
    lane_head =lax.broadcasted_iota(jnp.int32, (1, W), 1) // HGRN_DK
    head_mask = [lane_head == h for h in range(HGRN_HEADS)]
    bd = (lax.broadcasted_iota(jnp.int32, (W, W), 0) // HGRN_DK
          == lax.broadcasted_iota(jnp.int32, (W, W), 1) // HGRN_DK)
    bd_ones = jnp.where(bd, 1.0, 0.0).astype(BF16)

    def stack_heads(x):
        return jnp.concatenate([jnp.where(m, x, jnp.zeros_like(x)) for m in head_mask], axis=0)

    def group_chunks(n_chunks):
        return 4 if n_chunks % 4 == 0 else 2

    def phase_a(q_ref, zf_ref, zb_ref, i_ref, n_chunks, need_o):
        gc = group_chunks(n_chunks)
        gt = gc * C
        t_i = lax.broadcasted_iota(jnp.int32, (gt, gt), 0)
        s_i = lax.broadcasted_iota(jnp.int32, (gt, gt), 1)
        same = (t_i // C) == (s_i // C)
        allow = (same & (s_i <= t_i), same & (s_i >= t_i))
        allow_b16 = tuple(jnp.where(m, 1.0, 0.0).astype(BF16) for m in allow)
        t4 = lax.broadcasted_iota(jnp.int32, (gt, HGRN_HEADS * gt), 0)
        s4 = lax.broadcasted_iota(jnp.int32, (gt, HGRN_HEADS * gt), 1) % gt
        same4 = (t4 // C) == (s4 // C)
        allow4 = (same4 & (s4 <= t4), same4 & (s4 >= t4))
        pair4 = (t4 // P2) == (s4 // P2)
        cross4 = (pair4 & (t4 // C == s4 // C + 1), pair4 & (t4 // C + 1 == s4 // C))
        ones_row = jnp.ones((C, W), F32)

        def body(g, carry):
            rows = pl.ds(pl.multiple_of(g * gt, gt), gt)
            v = i_ref[0, rows, :]
            vs = stack_heads(v) if need_o else None
            for j in range(gc // 2):
                vt_ref[g * (gc // 2) + j] = v[j * P2:(j + 1) * P2, :].T
            fs, parts = [], []
            for d, z_ref in ((0, zf_ref), (1, zb_ref)):
                lb = lb_ref[d:d + 1, :]
                f = lb + (1.0 - lb) * _sigmoid(z_ref[0, rows, :])
                lf = jnp.log(f)
                fs.append(f)
                parts.append(_split2(lf))
            a_s = [_dot(allow_b16[d], parts[d][0]) + _dot(allow_b16[d], parts[d][1])
                   for d in (0, 1)]
            scs = []
            for d in (0, 1):
                a = a_s[d]
                kd = (1.0 - fs[d]) * jnp.exp(-a)
                ends = [a[c * C + C - 1:c * C + C, :] if d == 0 else a[c * C:c * C + 1, :]
                        for c in range(gc)]
                eal = [jnp.exp(e) for e in ends]
                ebc = [jnp.broadcast_to(e, (C, W)) for e in eal]
                ks = kd * jnp.concatenate(ebc, axis=0)
                is_first = [(c % 2 == 0) == (d == 0) for c in range(gc)]
                k_mul = jnp.concatenate([ebc[c ^ 1] if is_first[c] else ones_row
                                         for c in range(gc)], axis=0)
                ks_ref[d, rows, :] = (ks * k_mul).astype(BF16)
                n8 = (gc // 2) * 8
                eal_ref[d, pl.ds(pl.multiple_of(g * n8, n8), n8), :] = jnp.concatenate(
                    [jnp.broadcast_to(eal[2 * j] * eal[2 * j + 1], (8, W)) for j in range(gc // 2)],
                    axis=0)
                if need_o:
                    qd = q_ref[0, rows, :].astype(F32) * jnp.exp(a)
                    q_mul = jnp.concatenate([ones_row if is_first[c] else ebc[c ^ 1]
                                             for c in range(gc)], axis=0)
                    qd2 = (qd * q_mul).astype(BF16)
                    qd_ref[d, rows, :] = qd2
                    s2 = _dot_nt(jnp.concatenate([qd.astype(BF16), qd2], axis=0),
                                 stack_heads(kd.astype(BF16)))
                    scs.append(jnp.where(allow4[d], s2[:gt],
                                         jnp.where(cross4[d], s2[gt:], 0.0)).astype(BF16))
            if need_o:
                os_ref[rows, :] = _dot(scs[0], vs) + _dot(scs[1], vs)
            return carry

        lax.fori_loop(0, n_chunks // gc, body, 0, unroll=4)

    def phase_b(n_chunks, need_o):
        n_steps = n_chunks // 2
        gc = next(t for t in (4, 2, 1) if n_steps % t == 0)
        pw = 2 * HGRN_DK
        pairs = [slice(p * pw, (p + 1) * pw) for p in range(HGRN_HEADS // 2)]
        same_head = (lax.broadcasted_iota(jnp.int32, (pw, pw), 0) // HGRN_DK
                     == lax.broadcasted_iota(jnp.int32, (pw, pw), 1) // HGRN_DK)

        def step_rows(m):
            return pl.ds(pl.multiple_of(m * P2, P2), P2)

        def body(g, carry):
            idx = [[g * gc + c, n_steps - 1 - (g * gc + c)] for c in range(gc)]

            def updates(c):
                out = []
                for d, m in enumerate(idx[c]):
                    vt = vt_ref[m]
                    ks = ks_ref[d, step_rows(m), :]
                    out.append([jnp.where(same_head, _dot(vt[s, :], ks[:, s]), 0.0) for s in pairs])
                return out

            upd = {0: updates(0)}
            inters = []
            for c in range(gc):
                if c + 1 < gc:
                    upd[c + 1] = updates(c + 1)
                for d in (0, 1):
                    m = idx[c][d]
                    rows = step_rows(m)
                    eal = eal_ref[d, pl.ds(pl.multiple_of(m * 8, 8), 8), :][0:1, :]
                    sts = [st_ref[d, p] for p in range(len(pairs))]
                    if need_o:
                        qd = qd_ref[d, rows, :]
                        inter = [_dot_nt(qd[:, s], st.astype(BF16)) for s, st in zip(pairs, sts)]
                        inters.append((rows, jnp.concatenate(inter, axis=1)))
                    for p, s in enumerate(pairs):
                        st_ref[d, p] = sts[p] * eal[:, s] + upd[c][d][p]
            for rows, inter in inters:
                os_ref[rows, :] = os_ref[rows, :] + inter
            return carry

        lax.fori_loop(0, n_steps // gc, body, 0, unroll=2)

    def phase_c(g_ref, o_ref, n_rows):
        rt = next(t for t in (256, 128, 64) if n_rows % t == 0)

        def body(t, carry):
            rows = pl.ds(pl.multiple_of(t * rt, rt), rt)
            o = os_ref[rows, :]
            hi, lo = _split2(o * o)
            ms = (_dot(hi, bd_ones) + _dot(lo, bd_ones)) * (1.0 / HGRN_DK)
            y = o * lax.rsqrt(ms + EPS) * ng_ref[...]
            g = g_ref[0, rows, :].astype(F32)
            o_ref[0, rows, :] = (y * (g * _sigmoid(g))).astype(BF16)
            return carry

        lax.fori_loop(0, n_rows // rt, body, 0)

    st_ref[...] = jnp.zeros_like(st_ref)
    phase_a(qc_ref, zfc_ref, zbc_ref, ic_ref, n_ctx, with_ctx)
    phase_b(n_ctx, with_ctx)
    if with_ctx:
        phase_c(gc_ref, oc_ref, n_ctx * C)
    phase_a(ql_ref, zfl_ref, zbl_ref, il_ref, n_lat, True)
    phase_b(n_lat, True)
    phase_c(gl_ref, ol_ref, n_lat * C)


def _hgrn(lat, ctx, lb, norm_g4, with_ctx):
    b, s, _ = lat[0].shape
    l = ctx[0].shape[1]
    assert s % (2 * HGRN_CHUNK) == 0 and l % (2 * HGRN_CHUNK) == 0

    def seq(t):
        return pl.BlockSpec((1, t, MIX), lambda i: (i, 0, 0))

    out_specs = [pl.BlockSpec((1, s, MIX), lambda i: (i, 0, 0))]
    out_shape = [jax.ShapeDtypeStruct((b, s, MIX), BF16)]
    t_max = max(s, l)
    scratch = [pltpu.VMEM((2, HGRN_HEADS // 2, 2 * HGRN_DK, 2 * HGRN_DK), F32),
               pltpu.VMEM((2, t_max, MIX), BF16),
               pltpu.VMEM((2, t_max, MIX), BF16),
               pltpu.VMEM((t_max // (2 * HGRN_CHUNK), MIX, 2 * HGRN_CHUNK), BF16),
               pltpu.VMEM((2, t_max // (2 * HGRN_CHUNK) * 8, MIX), F32),
               pltpu.VMEM((t_max, MIX), F32)]
    if with_ctx:
        out_specs.append(pl.BlockSpec((1, l, MIX), lambda i: (i, 0, 0)))
        out_shape.append(jax.ShapeDtypeStruct((b, l, MIX), BF16))
    res = pl.pallas_call(
        functools.partial(_hgrn_kernel, n_lat=s // HGRN_CHUNK, n_ctx=l // HGRN_CHUNK,
                          with_ctx=with_ctx),
        grid=(b,),
        in_specs=[seq(s)] * 5 + [seq(l)] * 5 + [_const_spec((2, MIX)), _const_spec((1, MIX))],
        out_specs=out_specs,
        out_shape=out_shape,
        scratch_shapes=scratch,
        compiler_params=_cparams(1),
        name="hgrn",
    )(*lat, *ctx, lb, norm_g4)
    return (res[0], res[1]) if with_ctx else (res[0], None)


def _na_bias_table(rpb):
    n_h, nr, nc = rpb.shape
    w = GRID_W
    assert 2 * w == 128 and nc <= w
    padded = jnp.pad(rpb.astype(F32), ((0, 0), (0, 16 - nr), (0, 2 * w - nc)))
    return pl.pallas_call(
        _na_bias_kernel,
        grid=(n_h // 2,),
        in_specs=[pl.BlockSpec((2, 16, 2 * w), lambda i: (i, 0, 0))],
        out_specs=pl.BlockSpec((1, nr - 1, 2 * w, 2 * w), lambda i: (i, 0, 0, 0)),
        out_shape=jax.ShapeDtypeStruct((n_h // 2, nr - 1, 2 * w, 2 * w), F32),
        compiler_params=_cparams(1),
        name="na_bias",
    )(padded)


def _na_bias_kernel(rpb_ref, o_ref):
    w = GRID_W
    q = lax.broadcasted_iota(jnp.int32, (w, 2 * w), 0)
    lane = lax.broadcasted_iota(jnp.int32, (w, 2 * w), 1)
    kc = lane % w
    win = jnp.clip(q - NA_KW // 2, 0, w - NA_KW)
    valid = (kc >= win) & (kc < win + NA_KW)
    lower = lane < w
    shift_lo = 2 * w - (NA_KW - 1)
    shift_hi = shift_lo + w
    for h in range(2):
        for j in range(2 * NA_KH - 2):
            x0 = jnp.broadcast_to(rpb_ref[h, j:j + 1, :], (w, 2 * w))
            x1 = jnp.broadcast_to(rpb_ref[h, j + 1:j + 2, :], (w, 2 * w))
            t0 = pltpu.roll(x0, shift_lo, 1, stride=1, stride_axis=0)
            t1 = pltpu.roll(x1, shift_hi % (2 * w), 1, stride=1, stride_axis=0)
            o_ref[0, j, h * w:(h + 1) * w, :] = jnp.where(valid, jnp.where(lower, t0, t1), NEG_INF)


def _softmax_pv(s_list, v_list):
    m = functools.reduce(jnp.maximum, [jnp.max(s, axis=-1, keepdims=True) for s in s_list])
    p_list = [jnp.exp(s - m) for s in s_list]
    l = functools.reduce(jnp.add, [jnp.sum(p, axis=-1, keepdims=True) for p in p_list])
    o = functools.reduce(jnp.add, [_dot(p.astype(BF16), v) for p, v in zip(p_list, v_list)])
    return o * (1.0 / l)


NA_ROWS_PER_STEP = 16


def _na_kernel(q_ref, k_ref, v_ref, kc_ref, vc_ref, bias_ref, o_ref, *, rows):
    lane = lax.broadcasted_iota(jnp.int32, (1, 2 * NA_HEAD_DIM), 1)
    first = lane < NA_HEAD_DIM
    kc = kc_ref[0]
    vc = vc_ref[0]
    win = NA_KH * GRID_W
    nq = 2 * GRID_W

    def body(g, carry):
        qs, kws, vws, dis, qrows = [], [], [], [], []
        for j in range(NA_ROWS_PER_STEP):
            r = g * NA_ROWS_PER_STEP + j
            rs = jnp.clip(r - NA_KH // 2, 0, rows - NA_KH)
            dis.append(r - rs)
            qrows.append(pl.ds(pl.multiple_of(r * GRID_W, GRID_W), GRID_W))
            krows = pl.ds(pl.multiple_of(rs * GRID_W, GRID_W), win)
            q = q_ref[0, qrows[-1], :]
            qs += [jnp.where(first, q, jnp.zeros_like(q)), jnp.where(first, jnp.zeros_like(q), q)]
            kws.append(k_ref[0, krows, :])
            vws.append(v_ref[0, krows, :])
        qs = jnp.concatenate(qs, axis=0)
        sx = _dot_nt(qs, kc)
        pws, pxs, inv_ls = [], [], []
        for j in range(NA_ROWS_PER_STEP):
            ri0 = NA_KH - 1 - dis[j]
            bias = jnp.concatenate([bias_ref[0, ri0 + 2 * c] for c in range(NA_KH // 2)], axis=1)
            sw = _dot_nt(qs[j * nq:(j + 1) * nq], kws[j]) + bias
            sxj = sx[j * nq:(j + 1) * nq]
            m = jnp.maximum(jnp.max(sw, axis=-1, keepdims=True), jnp.max(sxj, axis=-1, keepdims=True))
            pw = jnp.exp(sw - m)
            px = jnp.exp(sxj - m)
            inv_ls.append(1.0 / (jnp.sum(pw, axis=-1, keepdims=True) + jnp.sum(px, axis=-1, keepdims=True)))
            pws.append(pw.astype(BF16))
            pxs.append(px.astype(BF16))
        ows = [_dot(pws[j], vws[j]) for j in range(NA_ROWS_PER_STEP)]
        oc = _dot(jnp.concatenate(pxs, axis=0), vc)
        for j in range(NA_ROWS_PER_STEP):
            o = (ows[j] + oc[j * nq:(j + 1) * nq]) * inv_ls[j]
            o_ref[0, qrows[j], :] = jnp.where(first, o[:GRID_W], o[GRID_W:]).astype(BF16)
        return carry

    lax.fori_loop(0, rows // NA_ROWS_PER_STEP, body, 0)


def _na_latent(q, k, v, kc, vc, bias):
    b, s, _ = q.shape
    l = kc.shape[1]
    rows = s // GRID_W
    assert s % GRID_W == 0 and rows >= NA_KH and rows % NA_ROWS_PER_STEP == 0
    hp = 2 * NA_HEAD_DIM

    def seq(t):
        return pl.BlockSpec((1, t, hp), lambda i, j: (i, 0, j))

    return pl.pallas_call(
        functools.partial(_na_kernel, rows=rows),
        grid=(b, NA_HEADS // 2),
        in_specs=[seq(s), seq(s), seq(s), seq(l), seq(l),
                  pl.BlockSpec((1, 2 * NA_KH - 2, 2 * GRID_W, 2 * GRID_W),
                               lambda i, j: (j, 0, 0, 0))],
        out_specs=seq(s),
        out_shape=jax.ShapeDtypeStruct((b, s, NA_WIDTH), BF16),
        compiler_params=_cparams(2),
        name="na_latent",
    )(q, k, v, kc, vc, bias)


def _na_ctx_kernel(q_ref, k_ref, v_ref, o_ref):
    lane = lax.broadcasted_iota(jnp.int32, (1, 2 * NA_HEAD_DIM), 1)
    first = lane < NA_HEAD_DIM
    for p in range(NA_HEADS // 2):
        cols = slice(p * 2 * NA_HEAD_DIM, (p + 1) * 2 * NA_HEAD_DIM)
        q, k, v = q_ref[0, :, cols], k_ref[0, :, cols], v_ref[0, :, cols]
        outs = []
        for h in range(2):
            qh = jnp.where(first if h == 0 else ~first, q, jnp.zeros_like(q))
            outs.append(_softmax_pv([_dot_nt(qh, k)], [v]))
        o_ref[0, :, cols] = jnp.where(first, outs[0], outs[1]).astype(BF16)


def _na_context(q, k, v):
    b, l, _ = q.shape
    spec = pl.BlockSpec((1, l, NA_WIDTH), lambda i: (i, 0, 0))
    return pl.pallas_call(
        _na_ctx_kernel,
        grid=(b,),
        in_specs=[spec, spec, spec],
        out_specs=spec,
        out_shape=jax.ShapeDtypeStruct((b, l, NA_WIDTH), BF16),
        compiler_params=_cparams(1),
        name="na_context",
    )(q, k, v)


def _ffn_kernel(x_ref, f_ref, h_ref, n_ref, g1_ref, sh_ref, sc_ref, g2_ref, ng_ref, fg_ref,
                wo_ref, wg_ref, wu_ref, wd_ref, o_ref, *, final):
    y = (_dot(f_ref[...], wo_ref[0:MIX, :]) + _dot(h_ref[...], wo_ref[MIX:2 * MIX, :])
         + _dot(n_ref[...], wo_ref[2 * MIX:, :]))
    x1 = x_ref[...] + g1_ref[0] * y
    hn = x1 * lax.rsqrt(jnp.mean(x1 * x1, axis=-1, keepdims=True) + EPS) * ng_ref[...]
    hm = (hn * (1.0 + sc_ref[0]) + sh_ref[0]).astype(BF16)
    d_ff = wg_ref.shape[1]
    acc = jnp.zeros(x1.shape, F32)
    for j in range(d_ff // FF_CHUNK):
        cols = slice(j * FF_CHUNK, (j + 1) * FF_CHUNK)
        g = _dot(hm, wg_ref[:, cols])
        u = _dot(hm, wu_ref[:, cols])
        act = (g * _sigmoid(g) * u).astype(BF16)
        acc = acc + _dot(act, wd_ref[cols, :])
    x2 = x1 + g2_ref[0] * acc
    if final:
        x2 = x2 * lax.rsqrt(jnp.mean(x2 * x2, axis=-1, keepdims=True) + EPS) * fg_ref[...]
    o_ref[...] = x2


def _ffn(x2d, f, h, n, norm_g, final_g, modv, mod_base, rows_per_mod, wo, wg, wu, wd, tm, final):
    nt, d = x2d.shape
    d_ff = wg.shape[1]
    assert d_ff % FF_CHUNK == 0

    def mod_idx(j):
        if rows_per_mod is None:
            return lambda i: (mod_base * 6 + j, 0, 0)
        tpm = rows_per_mod // tm
        return lambda i: ((mod_base + i // tpm) * 6 + j, 0, 0)

    def tok(w):
        return pl.BlockSpec((tm, w), lambda i: (i, 0))

    def vec():
        return pl.BlockSpec((1, d), lambda i: (0, 0))

    return pl.pallas_call(
        functools.partial(_ffn_kernel, final=final),
        grid=(nt // tm,),
        in_specs=[tok(d), tok(MIX), tok(MIX), tok(NA_WIDTH),
                  pl.BlockSpec((1, 1, d), mod_idx(2)), pl.BlockSpec((1, 1, d), mod_idx(3)),
                  pl.BlockSpec((1, 1, d), mod_idx(4)), pl.BlockSpec((1, 1, d), mod_idx(5)),
                  vec(), vec(),
                  _const_spec(wo.shape), _const_spec(wg.shape), _const_spec(wu.shape),
                  _const_spec(wd.shape)],
        out_specs=tok(d),
        out_shape=jax.ShapeDtypeStruct((nt, d), F32),
        compiler_params=_cparams(1),
        name="outproj_ffn",
    )(x2d, f, h, n, modv, modv, modv, modv, norm_g.reshape(1, d), final_g.reshape(1, d),
      wo, wg, wu, wd)


def _token_tile(n, largest):
    for tm in (1024, 512, 256, 128, 64, 32, 16, 8):
        if tm <= largest and n % tm == 0:
            return tm
    raise ValueError(f"token count {n} is not a multiple of 8")


def kernel(x, c, ctx, c_ctx, w_mod, b_mod, norm1_g, w_in, fourier_w, hgrn_lb, hgrn_norm_g, na_rpb,
           w_out, norm2_g, w_ffn_gate, w_ffn_up, w_ffn_down, final_norm_g):
    b, s, d = x.shape
    l = ctx.shape[1]
    depth = w_mod.shape[0]
    assert b < MOD_ROWS and d == 2 * MIX + NA_WIDTH

    cc = jnp.zeros((MOD_ROWS, d), F32).at[:b].set(c).at[b].set(c_ctx)
    mod = _modulation(cc, w_mod, b_mod)
    modv = mod.reshape(depth * MOD_ROWS * 6, 1, d)

    lbp = jax.nn.softmax(hgrn_lb.astype(F32), axis=0)
    lower = jnp.cumsum(lbp, axis=0) - lbp[0:1]
    norm_g4 = jnp.tile(hgrn_norm_g.astype(F32), (1, HGRN_HEADS))[:, None, :]

    tab_s, tab_l = _dft_table(s), _dft_table(l)
    c64, s64 = _channel_dft_consts()
    eye_g = jnp.eye(FOURIER_GROUPS, dtype=F32)

    tm_s, tm_c = _token_tile(s, 512), _token_tile(b * l, 512)
    tm_in_s, tm_in_c = _token_tile(s, 1024), _token_tile(b * l, 1024)
    xl = x.reshape(b * s, d)
    xc = ctx.reshape(b * l, d)
    for li in range(depth):
        with_ctx = li < depth - 1
        final = li == depth - 1
        base = li * MOD_ROWS
        w_in_b = w_in[li].astype(BF16)
        ul = _inproj(xl, norm1_g[li], modv, base, s, w_in_b, tm_in_s)
        uc = _inproj(xc, norm1_g[li], modv, base + b, None, w_in_b, tm_in_c)
        ul = [a.reshape(b, s, -1) for a in ul]
        uc = [a.reshape(b, l, -1) for a in uc]

        w_bd = (eye_g[:, None, :, None] * fourier_w[li][:, :, None, :]).reshape(MIX, MIX)
        f_lat = _fourier(ul[0], tab_s, c64, s64, w_bd)
        h_lat, h_ctx = _hgrn(ul[1:6], uc[1:6], lower[li], norm_g4[li], with_ctx)
        n_lat = _na_latent(ul[6], ul[7], ul[8], uc[7], uc[8], _na_bias_table(na_rpb[li]))

        wo, wg = w_out[li].astype(BF16), w_ffn_gate[li].astype(BF16)
        wu, wd = w_ffn_up[li].astype(BF16), w_ffn_down[li].astype(BF16)
        if with_ctx:
            f_ctx = _fourier(uc[0], tab_l, c64, s64, w_bd)
            n_ctx = _na_context(uc[6], uc[7], uc[8])
            xc = _ffn(xc, f_ctx.reshape(b * l, -1), h_ctx.reshape(b * l, -1),
                      n_ctx.reshape(b * l, -1), norm2_g[li], final_norm_g, modv, base + b, None,
                      wo, wg, wu, wd, tm_c, False)
        xl = _ffn(xl, f_lat.reshape(b * s, -1), h_lat.reshape(b * s, -1), n_lat.reshape(b * s, -1),
                  norm2_g[li], final_norm_g, modv, base, s, wo, wg, wu, wd, tm_s, final)
    return xl.reshape(b, s, d)
```

```python
import functools

import jax
import jax.numpy as jnp
import numpy as np
from jax import lax
from jax.experimental import pallas as pl
from jax.experimental.pallas import tpu as pltpu

F32 = jnp.float32
BF16 = jnp.bfloat16

EPS = 1e-6
NEG_INF = -1e30
GRID_W = 64
FOURIER_GROUPS = 4
FOURIER_GD = 64
HGRN_DK = 64
HGRN_HEADS = 4
HGRN_CHUNK = 32
NA_HEAD_DIM = 64
NA_HEADS = 8
NA_KH = 8
NA_KW = 16
MIX = 256
NA_WIDTH = NA_HEADS * NA_HEAD_DIM
MOD_ROWS = 16
FF_CHUNK = 256
VMEM_LIMIT = 56 * 1024 * 1024


def _cparams(n_axes):
    return pltpu.CompilerParams(dimension_semantics=("arbitrary",) * n_axes,
                                vmem_limit_bytes=VMEM_LIMIT)


def _split2(a):
    hi = a.astype(BF16)
    lo = (a - hi.astype(F32)).astype(BF16)
    return hi, lo


def _dot(a, b):
    return jnp.dot(a, b, preferred_element_type=F32)


def _dot_nt(a, b):
    return lax.dot_general(a, b, (((1,), (1,)), ((), ())), preferred_element_type=F32)


def _dot_hilo(a, b):
    ah, al = _split2(a)
    bh, bl = _split2(b)
    return _dot(ah, bh) + _dot(al, bh) + _dot(ah, bl)


def _sigmoid(z):
    return 1.0 / (1.0 + jnp.exp(-z))


def _const_spec(shape):
    nd = len(shape)
    return pl.BlockSpec(shape, lambda *_: (0,) * nd, pipeline_mode=pl.Buffered(1))


def _layer_spec(w, li):
    return pl.BlockSpec((None,) + w.shape[1:], lambda *_: (li,) + (0,) * (w.ndim - 1),
                        pipeline_mode=pl.Buffered(1))


def _mod_kernel(c_ref, w_ref, b_ref, o_ref):
    c = c_ref[...]
    s = c * _sigmoid(c)
    o_ref[0] = _dot_hilo(s, w_ref[0]) + b_ref[0]


def _modulation(cc, w_mod, b_mod):
    depth, d, n = w_mod.shape
    tn = 1024
    return pl.pallas_call(
        _mod_kernel,
        grid=(depth, n // tn),
        in_specs=[pl.BlockSpec((MOD_ROWS, d), lambda l, j: (0, 0)),
                  pl.BlockSpec((1, d, tn), lambda l, j: (l, 0, j)),
                  pl.BlockSpec((1, 1, tn), lambda l, j: (l, 0, j))],
        out_specs=pl.BlockSpec((1, MOD_ROWS, tn), lambda l, j: (l, 0, j)),
        out_shape=jax.ShapeDtypeStruct((depth, MOD_ROWS, n), F32),
        compiler_params=_cparams(2),
        name="modulation",
    )(cc, w_mod, b_mod.reshape(depth, 1, n))


def _inproj_kernel(x_ref, g_ref, sh_ref, sc_ref, w_ref,
                   uf_ref, hq_ref, zf_ref, zb_ref, hi_ref, hg_ref, q_ref, k_ref, v_ref):
    x = x_ref[...]
    y = x * lax.rsqrt(jnp.mean(x * x, axis=-1, keepdims=True) + EPS) * g_ref[...]
    a = (y * (1.0 + sc_ref[0]) + sh_ref[0]).astype(BF16)
    col = 0
    for ref in (uf_ref, hq_ref, zf_ref, zb_ref, hi_ref, hg_ref):
        ref[...] = _dot(a, w_ref[:, col:col + MIX]).astype(ref.dtype)
        col += MIX
    scale = NA_HEAD_DIM ** -0.5
    q_ref[...] = (_dot(a, w_ref[:, col:col + NA_WIDTH]) * scale).astype(BF16)
    col += NA_WIDTH
    k_ref[...] = _dot(a, w_ref[:, col:col + NA_WIDTH]).astype(BF16)
    col += NA_WIDTH
    v_ref[...] = _dot(a, w_ref[:, col:col + NA_WIDTH]).astype(BF16)


def _inproj(x2d, norm_g, modv, mod_base, rows_per_mod, w_in_b, li, tm):
    n, d = x2d.shape
    in_w = w_in_b.shape[2]

    def mod_idx(j):
        if rows_per_mod is None:
            return lambda i: (mod_base * 6 + j, 0, 0)
        tpm = rows_per_mod // tm
        return lambda i: ((mod_base + i // tpm) * 6 + j, 0, 0)

    def tok(w):
        return pl.BlockSpec((tm, w), lambda i: (i, 0))

    outs = [(MIX, BF16), (MIX, BF16), (MIX, F32), (MIX, F32), (MIX, BF16), (MIX, BF16),
            (NA_WIDTH, BF16), (NA_WIDTH, BF16), (NA_WIDTH, BF16)]
    return pl.pallas_call(
        _inproj_kernel,
        grid=(n // tm,),
        in_specs=[tok(d),
                  pl.BlockSpec((1, d), lambda i: (0, 0)),
                  pl.BlockSpec((1, 1, d), mod_idx(0)),
                  pl.BlockSpec((1, 1, d), mod_idx(1)),
                  _layer_spec(w_in_b, li)],
        out_specs=[tok(w) for w, _ in outs],
        out_shape=[jax.ShapeDtypeStruct((n, w), dt) for w, dt in outs],
        compiler_params=_cparams(1),
        name="inproj",
    )(x2d, norm_g.reshape(1, d), modv, modv, w_in_b)


def _dft_table(t_len):
    kb = min(64, t_len)
    assert t_len % kb == 0
    t = jnp.arange(t_len // 2, dtype=jnp.int32)[None, :]
    k_hi = jnp.arange(t_len // kb, dtype=jnp.int32)[:, None] * kb
    k_lo = jnp.arange(kb, dtype=jnp.int32)[:, None]
    w = 2.0 * np.pi / t_len
    ang_hi = ((k_hi * t) % t_len).astype(F32) * w
    ang_lo = ((k_lo * t) % t_len).astype(F32) * w
    ch, sh = jnp.cos(ang_hi)[:, None, :], jnp.sin(ang_hi)[:, None, :]
    cl, sl = jnp.cos(ang_lo)[None, :, :], jnp.sin(ang_lo)[None, :, :]
    cos = (ch * cl - sh * sl).reshape(t_len, t_len // 2)
    sin = (sh * cl + ch * sl).reshape(t_len, t_len // 2)
    return jnp.concatenate([cos, sin], axis=1).astype(BF16)


def _channel_dft_consts():
    cd = np.outer(np.arange(FOURIER_GD), np.arange(FOURIER_GD)) % FOURIER_GD
    ang = 2.0 * np.pi * cd / FOURIER_GD
    eye = np.eye(FOURIER_GROUPS)
    return (jnp.asarray(np.kron(eye, np.cos(ang)), F32),
            jnp.asarray(np.kron(eye, np.sin(ang)), F32))


def _fourier_kernel(u_ref, tab_ref, c64_ref, s64_ref, w_ref, o_ref, pq_ref, pm_ref, *, t_len, tm):
    half = t_len // 2
    bs = min(256, half)
    nb = half // bs
    bi = pl.program_id(1)

    @pl.when(pl.program_id(0) == 0)
    def _():
        norm = (FOURIER_GD * t_len) ** -0.5
        w = w_ref[...]
        a = (_dot_hilo(c64_ref[...], w) * norm).astype(BF16)
        b = (_dot_hilo(s64_ref[...], w) * (-norm)).astype(BF16)
        p_i = lax.broadcasted_iota(jnp.int32, (bs, bs), 0)
        q_i = lax.broadcasted_iota(jnp.int32, (bs, bs), 1)
        rev_shift = jnp.where(q_i == bs - p_i, 1.0, 0.0).astype(BF16)
        first_row = jnp.where(p_i + q_i == 0, 1.0, 0.0).astype(BF16)
        for i in range(nb):
            src = half + (nb - 1 - i) * bs
            r = _dot(rev_shift, u_ref[0, src:src + bs, :])
            if i > 0:
                r = r + _dot(first_row, u_ref[0, src + bs:src + 2 * bs, :])
            u = u_ref[0, i * bs:(i + 1) * bs, :].astype(F32)
            pq_ref[bi, i * bs:(i + 1) * bs, :] = _dot((u + r).astype(BF16), a).astype(BF16)
            pq_ref[bi, half + i * bs:half + (i + 1) * bs, :] = _dot((u - r).astype(BF16),
                                                                    b).astype(BF16)
        pm_ref[bi] = _dot(u_ref[0, half:half + 8, :], a)

    k = pl.program_id(0) * tm + lax.broadcasted_iota(jnp.int32, (tm, 1), 0)
    sign = (1 - 2 * (k & 1)).astype(F32)
    o_ref[0] = (_dot(tab_ref[...], pq_ref[bi]) + sign * pm_ref[bi, 0:1, :]).astype(BF16)


def _fourier(u, tab, c64, s64, w_bd):
    b, t_len, _ = u.shape
    assert t_len % 64 == 0
    tm = min(512, t_len)
    return pl.pallas_call(
        functools.partial(_fourier_kernel, t_len=t_len, tm=tm),
        grid=(t_len // tm, b),
        in_specs=[pl.BlockSpec((1, t_len, MIX), lambda j, i: (jnp.where(j == 0, i, b - 1), 0, 0)),
                  pl.BlockSpec((tm, t_len), lambda j, i: (j, 0)),
                  _const_spec((MIX, MIX)), _const_spec((MIX, MIX)), _const_spec((MIX, MIX))],
        out_specs=pl.BlockSpec((1, tm, MIX), lambda j, i: (i, j, 0)),
        out_shape=jax.ShapeDtypeStruct((b, t_len, MIX), BF16),
        scratch_shapes=[pltpu.VMEM((b, t_len, MIX), BF16), pltpu.VMEM((b, 8, MIX), F32)],
        compiler_params=_cparams(2),
        name="fourier",
    )(u, tab, c64, s64, w_bd)


def _hgrn_kernel(ql_ref, zfl_ref, zbl_ref, il_ref, gl_ref,
                 qc_ref, zfc_ref, zbc_ref, ic_ref, gc_ref, lb_ref, ng_ref,
                 *rest, n_lat, n_ctx, with_ctx):
    if with_ctx:
        ol_ref, oc_ref, st_ref, qd_ref, ks_ref, vt_ref, eal_ref, os_ref = rest
    else:
        ol_ref, st_ref, qd_ref, ks_ref, vt_ref, eal_ref, os_ref = rest
        oc_ref = None
    C = HGRN_CHUNK
    P2 = 2 * C
    W = MIX
    lane_head =lax.broadcasted_iota(jnp.int32, (1, W), 1) // HGRN_DK
    head_mask = [lane_head == h for h in range(HGRN_HEADS)]
    bd = (lax.broadcasted_iota(jnp.int32, (W, W), 0) // HGRN_DK
          == lax.broadcasted_iota(jnp.int32, (W, W), 1) // HGRN_DK)
    bd_ones = jnp.where(bd, 1.0, 0.0).astype(BF16)

    def stack_heads(x):
        return jnp.concatenate([jnp.where(m, x, jnp.zeros_like(x)) for m in head_mask], axis=0)

    def group_chunks(n_chunks):
        return 4 if n_chunks % 4 == 0 else 2

    def phase_a(q_ref, zf_ref, zb_ref, i_ref, n_chunks, need_o):
        gc = group_chunks(n_chunks)
        gt = gc * C
        t_i = lax.broadcasted_iota(jnp.int32, (gt, gt), 0)
        s_i = lax.broadcasted_iota(jnp.int32, (gt, gt), 1)
        same = (t_i // C) == (s_i // C)
        allow = (same & (s_i <= t_i), same & (s_i >= t_i))
        allow_b16 = tuple(jnp.where(m, 1.0, 0.0).astype(BF16) for m in allow)
        t4 = lax.broadcasted_iota(jnp.int32, (gt, HGRN_HEADS * gt), 0)
        s4 = lax.broadcasted_iota(jnp.int32, (gt, HGRN_HEADS * gt), 1) % gt
        same4 = (t4 // C) == (s4 // C)
        allow4 = (same4 & (s4 <= t4), same4 & (s4 >= t4))
        pair4 = (t4 // P2) == (s4 // P2)
        cross4 = (pair4 & (t4 // C == s4 // C + 1), pair4 & (t4 // C + 1 == s4 // C))
        ones_row = jnp.ones((C, W), F32)

        def body(g, carry):
            rows = pl.ds(pl.multiple_of(g * gt, gt), gt)
            v = i_ref[0, rows, :]
            vs = stack_heads(v) if need_o else None
            for j in range(gc // 2):
                vt_ref[g * (gc // 2) + j] = v[j * P2:(j + 1) * P2, :].T
            fs, parts = [], []
            for d, z_ref in ((0, zf_ref), (1, zb_ref)):
                lb = lb_ref[d:d + 1, :]
                f = lb + (1.0 - lb) * _sigmoid(z_ref[0, rows, :])
                lf = jnp.log(f)
                fs.append(f)
                parts.append(_split2(lf))
            a_s = [_dot(allow_b16[d], parts[d][0]) + _dot(allow_b16[d], parts[d][1])
                   for d in (0, 1)]
            scs = []
            for d in (0, 1):
                a = a_s[d]
                kd = (1.0 - fs[d]) * jnp.exp(-a)
                ends = [a[c * C + C - 1:c * C + C, :] if d == 0 else a[c * C:c * C + 1, :]
                        for c in range(gc)]
                eal = [jnp.exp(e) for e in ends]
                ebc = [jnp.broadcast_to(e, (C, W)) for e in eal]
                is_first = [(c % 2 == 0) == (d == 0) for c in range(gc)]
                k_mul = jnp.concatenate(
                    [jnp.broadcast_to(eal[c] * eal[c ^ 1], (C, W)) if is_first[c] else ebc[c]
                     for c in range(gc)], axis=0)
                ks_ref[d, rows, :] = (kd * k_mul).astype(BF16)
                n8 = (gc // 2) * 8
                eal_ref[d, pl.ds(pl.multiple_of(g * n8, n8), n8), :] = jnp.concatenate(
                    [jnp.broadcast_to(eal[2 * j] * eal[2 * j + 1], (8, W)) for j in range(gc // 2)],
                    axis=0)
                if need_o:
                    qd = q_ref[0, rows, :].astype(F32) * jnp.exp(a)
                    q_mul = jnp.concatenate([ones_row if is_first[c] else ebc[c ^ 1]
                                             for c in range(gc)], axis=0)
                    qd2 = (qd * q_mul).astype(BF16)
                    qd_ref[d, rows, :] = qd2
                    s2 = _dot_nt(jnp.concatenate([qd.astype(BF16), qd2], axis=0),
                                 stack_heads(kd.astype(BF16)))
                    scs.append(jnp.where(allow4[d], s2[:gt],
                                         jnp.where(cross4[d], s2[gt:], 0.0)).astype(BF16))
            if need_o:
                os_ref[rows, :] = _dot(scs[0], vs) + _dot(scs[1], vs)
            return carry

        lax.fori_loop(0, n_chunks // gc, body, 0, unroll=4)

    def phase_b(n_chunks, need_o):
        n_steps = n_chunks // 2
        gc = next(t for t in (4, 2, 1) if n_steps % t == 0)
        pw = 2 * HGRN_DK
        pairs = [slice(p * pw, (p + 1) * pw) for p in range(HGRN_HEADS // 2)]
        same_head = (lax.broadcasted_iota(jnp.int32, (pw, pw), 0) // HGRN_DK
                     == lax.broadcasted_iota(jnp.int32, (pw, pw), 1) // HGRN_DK)

        def step_rows(m):
            return pl.ds(pl.multiple_of(m * P2, P2), P2)

        def body(g, carry):
            idx = [[g * gc + c, n_steps - 1 - (g * gc + c)] for c in range(gc)]

            def updates(c):
                out = []
                for d, m in enumerate(idx[c]):
                    vt = vt_ref[m]
                    ks = ks_ref[d, step_rows(m), :]
                    out.append([jnp.where(same_head, _dot(vt[s, :], ks[:, s]), 0.0) for s in pairs])
                return out

            upd = {0: updates(0)}
            inters = []
            for c in range(gc):
                if c + 1 < gc:
                    upd[c + 1] = updates(c + 1)
                for d in (0, 1):
                    m = idx[c][d]
                    rows = step_rows(m)
                    eal = eal_ref[d, pl.ds(pl.multiple_of(m * 8, 8), 8), :][0:1, :]
                    sts = [st_ref[d, p] for p in range(len(pairs))]
                    if need_o:
                        qd = qd_ref[d, rows, :]
                        inter = [_dot_nt(qd[:, s], st.astype(BF16)) for s, st in zip(pairs, sts)]
                        inters.append((rows, jnp.concatenate(inter, axis=1)))
                    for p, s in enumerate(pairs):
                        st_ref[d, p] = sts[p] * eal[:, s] + upd[c][d][p]
            for rows, inter in inters:
                os_ref[rows, :] = os_ref[rows, :] + inter
            return carry

        lax.fori_loop(0, n_steps // gc, body, 0, unroll=2)

    def phase_c(g_ref, o_ref, n_rows):
        rt = next(t for t in (256, 128, 64) if n_rows % t == 0)

        def body(t, carry):
            rows = pl.ds(pl.multiple_of(t * rt, rt), rt)
            o = os_ref[rows, :]
            hi, lo = _split2(o * o)
            ms = (_dot(hi, bd_ones) + _dot(lo, bd_ones)) * (1.0 / HGRN_DK)
            y = o * lax.rsqrt(ms + EPS) * ng_ref[...]
            g = g_ref[0, rows, :].astype(F32)
            o_ref[0, rows, :] = (y * (g * _sigmoid(g))).astype(BF16)
            return carry

        lax.fori_loop(0, n_rows // rt, body, 0)

    st_ref[...] = jnp.zeros_like(st_ref)
    phase_a(qc_ref, zfc_ref, zbc_ref, ic_ref, n_ctx, with_ctx)
    phase_b(n_ctx, with_ctx)
    if with_ctx:
        phase_c(gc_ref, oc_ref, n_ctx * C)
    phase_a(ql_ref, zfl_ref, zbl_ref, il_ref, n_lat, True)
    phase_b(n_lat, True)
    phase_c(gl_ref, ol_ref, n_lat * C)


def _hgrn(lat, ctx, lb, norm_g4, with_ctx):
    b, s, _ = lat[0].shape
    l = ctx[0].shape[1]
    assert s % (2 * HGRN_CHUNK) == 0 and l % (2 * HGRN_CHUNK) == 0

    def seq(t):
        return pl.BlockSpec((1, t, MIX), lambda i: (i, 0, 0))

    out_specs = [pl.BlockSpec((1, s, MIX), lambda i: (i, 0, 0))]
    out_shape = [jax.ShapeDtypeStruct((b, s, MIX), BF16)]
    t_max = max(s, l)
    scratch = [pltpu.VMEM((2, HGRN_HEADS // 2, 2 * HGRN_DK, 2 * HGRN_DK), F32),
               pltpu.VMEM((2, t_max, MIX), BF16),
               pltpu.VMEM((2, t_max, MIX), BF16),
               pltpu.VMEM((t_max // (2 * HGRN_CHUNK), MIX, 2 * HGRN_CHUNK), BF16),
               pltpu.VMEM((2, t_max // (2 * HGRN_CHUNK) * 8, MIX), F32),
               pltpu.VMEM((t_max, MIX), F32)]
    if with_ctx:
        out_specs.append(pl.BlockSpec((1, l, MIX), lambda i: (i, 0, 0)))
        out_shape.append(jax.ShapeDtypeStruct((b, l, MIX), BF16))
    res = pl.pallas_call(
        functools.partial(_hgrn_kernel, n_lat=s // HGRN_CHUNK, n_ctx=l // HGRN_CHUNK,
                          with_ctx=with_ctx),
        grid=(b,),
        in_specs=[seq(s)] * 5 + [seq(l)] * 5 + [_const_spec((2, MIX)), _const_spec((1, MIX))],
        out_specs=out_specs,
        out_shape=out_shape,
        scratch_shapes=scratch,
        compiler_params=_cparams(1),
        name="hgrn",
    )(*lat, *ctx, lb, norm_g4)
    return (res[0], res[1]) if with_ctx else (res[0], None)


def _na_bias_table(rpb):
    n_h, nr, nc = rpb.shape
    w = GRID_W
    assert 2 * w == 128 and nc <= w
    padded = jnp.pad(rpb.astype(F32), ((0, 0), (0, 16 - nr), (0, 2 * w - nc)))
    return pl.pallas_call(
        _na_bias_kernel,
        grid=(n_h // 2,),
        in_specs=[pl.BlockSpec((2, 16, 2 * w), lambda i: (i, 0, 0))],
        out_specs=pl.BlockSpec((1, nr - 1, 2 * w, 2 * w), lambda i: (i, 0, 0, 0)),
        out_shape=jax.ShapeDtypeStruct((n_h // 2, nr - 1, 2 * w, 2 * w), F32),
        compiler_params=_cparams(1),
        name="na_bias",
    )(padded)


def _na_bias_kernel(rpb_ref, o_ref):
    w = GRID_W
    q = lax.broadcasted_iota(jnp.int32, (w, 2 * w), 0)
    lane = lax.broadcasted_iota(jnp.int32, (w, 2 * w), 1)
    kc = lane % w
    win = jnp.clip(q - NA_KW // 2, 0, w - NA_KW)
    valid = (kc >= win) & (kc < win + NA_KW)
    lower = lane < w
    shift_lo = 2 * w - (NA_KW - 1)
    shift_hi = shift_lo + w
    for h in range(2):
        for j in range(2 * NA_KH - 2):
            x0 = jnp.broadcast_to(rpb_ref[h, j:j + 1, :], (w, 2 * w))
            x1 = jnp.broadcast_to(rpb_ref[h, j + 1:j + 2, :], (w, 2 * w))
            t0 = pltpu.roll(x0, shift_lo, 1, stride=1, stride_axis=0)
            t1 = pltpu.roll(x1, shift_hi % (2 * w), 1, stride=1, stride_axis=0)
            o_ref[0, j, h * w:(h + 1) * w, :] = jnp.where(valid, jnp.where(lower, t0, t1), NEG_INF)


def _softmax_pv(s_list, v_list):
    m = functools.reduce(jnp.maximum, [jnp.max(s, axis=-1, keepdims=True) for s in s_list])
    p_list = [jnp.exp(s - m) for s in s_list]
    l = functools.reduce(jnp.add, [jnp.sum(p, axis=-1, keepdims=True) for p in p_list])
    o = functools.reduce(jnp.add, [_dot(p.astype(BF16), v) for p, v in zip(p_list, v_list)])
    return o * (1.0 / l)


NA_ROWS_PER_STEP = 16


def _na_kernel(q_ref, k_ref, v_ref, kc_ref, vc_ref, bias_ref, o_ref, *, rows):
    lane = lax.broadcasted_iota(jnp.int32, (1, 2 * NA_HEAD_DIM), 1)
    first = lane < NA_HEAD_DIM
    kc = kc_ref[0]
    vc = vc_ref[0]
    win = NA_KH * GRID_W
    nq = 2 * GRID_W

    def body(g, carry):
        qs, kws, vws, dis, qrows = [], [], [], [], []
        for j in range(NA_ROWS_PER_STEP):
            r = g * NA_ROWS_PER_STEP + j
            rs = jnp.clip(r - NA_KH // 2, 0, rows - NA_KH)
            dis.append(r - rs)
            qrows.append(pl.ds(pl.multiple_of(r * GRID_W, GRID_W), GRID_W))
            krows = pl.ds(pl.multiple_of(rs * GRID_W, GRID_W), win)
            q = q_ref[0, qrows[-1], :]
            qs += [jnp.where(first, q, jnp.zeros_like(q)), jnp.where(first, jnp.zeros_like(q), q)]
            kws.append(k_ref[0, krows, :])
            vws.append(v_ref[0, krows, :])
        qs = jnp.concatenate(qs, axis=0)
        sx = _dot_nt(qs, kc)
        pws, pxs, inv_ls = [], [], []
        for j in range(NA_ROWS_PER_STEP):
            ri0 = NA_KH - 1 - dis[j]
            bias = jnp.concatenate([bias_ref[0, ri0 + 2 * c] for c in range(NA_KH // 2)], axis=1)
            sw = _dot_nt(qs[j * nq:(j + 1) * nq], kws[j]) + bias
            sxj = sx[j * nq:(j + 1) * nq]
            m = jnp.maximum(jnp.max(sw, axis=-1, keepdims=True), jnp.max(sxj, axis=-1, keepdims=True))
            pw = jnp.exp(sw - m)
            px = jnp.exp(sxj - m)
            inv_ls.append(1.0 / (jnp.sum(pw, axis=-1, keepdims=True) + jnp.sum(px, axis=-1, keepdims=True)))
            pws.append(pw.astype(BF16))
            pxs.append(px.astype(BF16))
        ows = [_dot(pws[j], vws[j]) for j in range(NA_ROWS_PER_STEP)]
        oc = _dot(jnp.concatenate(pxs, axis=0), vc)
        for j in range(NA_ROWS_PER_STEP):
            o = (ows[j] + oc[j * nq:(j + 1) * nq]) * inv_ls[j]
            o_ref[0, qrows[j], :] = jnp.where(first, o[:GRID_W], o[GRID_W:]).astype(BF16)
        return carry

    lax.fori_loop(0, rows // NA_ROWS_PER_STEP, body, 0)


def _na_latent(q, k, v, kc, vc, bias):
    b, s, _ = q.shape
    l = kc.shape[1]
    rows = s // GRID_W
    assert s % GRID_W == 0 and rows >= NA_KH and rows % NA_ROWS_PER_STEP == 0
    hp = 2 * NA_HEAD_DIM

    def seq(t):
        return pl.BlockSpec((1, t, hp), lambda i, j: (i, 0, j))

    return pl.pallas_call(
        functools.partial(_na_kernel, rows=rows),
        grid=(b, NA_HEADS // 2),
        in_specs=[seq(s), seq(s), seq(s), seq(l), seq(l),
                  pl.BlockSpec((1, 2 * NA_KH - 2, 2 * GRID_W, 2 * GRID_W),
                               lambda i, j: (j, 0, 0, 0))],
        out_specs=seq(s),
        out_shape=jax.ShapeDtypeStruct((b, s, NA_WIDTH), BF16),
        compiler_params=_cparams(2),
        name="na_latent",
    )(q, k, v, kc, vc, bias)


def _na_ctx_kernel(q_ref, k_ref, v_ref, o_ref):
    lane = lax.broadcasted_iota(jnp.int32, (1, 2 * NA_HEAD_DIM), 1)
    first = lane < NA_HEAD_DIM
    for p in range(NA_HEADS // 2):
        cols = slice(p * 2 * NA_HEAD_DIM, (p + 1) * 2 * NA_HEAD_DIM)
        q, k, v = q_ref[0, :, cols], k_ref[0, :, cols], v_ref[0, :, cols]
        outs = []
        for h in range(2):
            qh = jnp.where(first if h == 0 else ~first, q, jnp.zeros_like(q))
            outs.append(_softmax_pv([_dot_nt(qh, k)], [v]))
        o_ref[0, :, cols] = jnp.where(first, outs[0], outs[1]).astype(BF16)


def _na_context(q, k, v):
    b, l, _ = q.shape
    spec = pl.BlockSpec((1, l, NA_WIDTH), lambda i: (i, 0, 0))
    return pl.pallas_call(
        _na_ctx_kernel,
        grid=(b,),
        in_specs=[spec, spec, spec],
        out_specs=spec,
        out_shape=jax.ShapeDtypeStruct((b, l, NA_WIDTH), BF16),
        compiler_params=_cparams(1),
        name="na_context",
    )(q, k, v)


def _ffn_kernel(x_ref, f_ref, h_ref, n_ref, g1_ref, sh_ref, sc_ref, g2_ref, ng_ref, fg_ref,
                wo_ref, wg_ref, wu_ref, wd_ref, o_ref, *, final):
    y = (_dot(f_ref[...], wo_ref[0:MIX, :]) + _dot(h_ref[...], wo_ref[MIX:2 * MIX, :])
         + _dot(n_ref[...], wo_ref[2 * MIX:, :]))
    x1 = x_ref[...] + g1_ref[0] * y
    hn = x1 * lax.rsqrt(jnp.mean(x1 * x1, axis=-1, keepdims=True) + EPS) * ng_ref[...]
    hm = (hn * (1.0 + sc_ref[0]) + sh_ref[0]).astype(BF16)
    d_ff = wg_ref.shape[1]
    acc = jnp.zeros(x1.shape, F32)
    for j in range(d_ff // FF_CHUNK):
        cols = slice(j * FF_CHUNK, (j + 1) * FF_CHUNK)
        g = _dot(hm, wg_ref[:, cols])
        u = _dot(hm, wu_ref[:, cols])
        act = (g * _sigmoid(g) * u).astype(BF16)
        acc = acc + _dot(act, wd_ref[cols, :])
    x2 = x1 + g2_ref[0] * acc
    if final:
        x2 = x2 * lax.rsqrt(jnp.mean(x2 * x2, axis=-1, keepdims=True) + EPS) * fg_ref[...]
    o_ref[...] = x2


def _ffn(x2d, f, h, n, norm_g, final_g, modv, mod_base, rows_per_mod, wo, wg, wu, wd, li, tm, final):
    nt, d = x2d.shape
    d_ff = wg.shape[2]
    assert d_ff % FF_CHUNK == 0

    def mod_idx(j):
        if rows_per_mod is None:
            return lambda i: (mod_base * 6 + j, 0, 0)
        tpm = rows_per_mod // tm
        return lambda i: ((mod_base + i // tpm) * 6 + j, 0, 0)

    def tok(w):
        return pl.BlockSpec((tm, w), lambda i: (i, 0))

    def vec():
        return pl.BlockSpec((1, d), lambda i: (0, 0))

    return pl.pallas_call(
        functools.partial(_ffn_kernel, final=final),
        grid=(nt // tm,),
        in_specs=[tok(d), tok(MIX), tok(MIX), tok(NA_WIDTH),
                  pl.BlockSpec((1, 1, d), mod_idx(2)), pl.BlockSpec((1, 1, d), mod_idx(3)),
                  pl.BlockSpec((1, 1, d), mod_idx(4)), pl.BlockSpec((1, 1, d), mod_idx(5)),
                  vec(), vec(),
                  _layer_spec(wo, li), _layer_spec(wg, li), _layer_spec(wu, li),
                  _layer_spec(wd, li)],
        out_specs=tok(d),
        out_shape=jax.ShapeDtypeStruct((nt, d), F32),
        compiler_params=_cparams(1),
        name="outproj_ffn",
    )(x2d, f, h, n, modv, modv, modv, modv, norm_g.reshape(1, d), final_g.reshape(1, d),
      wo, wg, wu, wd)


def _token_tile(n, largest):
    for tm in (1024, 512, 256, 128, 64, 32, 16, 8):
        if tm <= largest and n % tm == 0:
            return tm
    raise ValueError(f"token count {n} is not a multiple of 8")


def kernel(x, c, ctx, c_ctx, w_mod, b_mod, norm1_g, w_in, fourier_w, hgrn_lb, hgrn_norm_g, na_rpb,
           w_out, norm2_g, w_ffn_gate, w_ffn_up, w_ffn_down, final_norm_g):
    b, s, d = x.shape
    l = ctx.shape[1]
    depth = w_mod.shape[0]
    assert b < MOD_ROWS and d == 2 * MIX + NA_WIDTH

    cc = jnp.zeros((MOD_ROWS, d), F32).at[:b].set(c).at[b].set(c_ctx)
    mod = _modulation(cc, w_mod, b_mod)
    modv = mod.reshape(depth * MOD_ROWS * 6, 1, d)

    lbp = jax.nn.softmax(hgrn_lb.astype(F32), axis=0)
    lower = jnp.cumsum(lbp, axis=0) - lbp[0:1]
    norm_g4 = jnp.tile(hgrn_norm_g.astype(F32), (1, HGRN_HEADS))[:, None, :]

    tab_s, tab_l = _dft_table(s), _dft_table(l)
    c64, s64 = _channel_dft_consts()
    eye_g = jnp.eye(FOURIER_GROUPS, dtype=F32)

    tm_s, tm_c = _token_tile(s, 512), _token_tile(b * l, 512)
    tm_in_s, tm_in_c = _token_tile(s, 1024), _token_tile(b * l, 1024)
    xl = x.reshape(b * s, d)
    xc = ctx.reshape(b * l, d)
    w_in_b, wo, wg = w_in.astype(BF16), w_out.astype(BF16), w_ffn_gate.astype(BF16)
    wu, wd = w_ffn_up.astype(BF16), w_ffn_down.astype(BF16)
    for li in range(depth):
        with_ctx = li < depth - 1
        final = li == depth - 1
        base = li * MOD_ROWS
        ul = _inproj(xl, norm1_g[li], modv, base, s, w_in_b, li, tm_in_s)
        uc = _inproj(xc, norm1_g[li], modv, base + b, None, w_in_b, li, tm_in_c)
        ul = [a.reshape(b, s, -1) for a in ul]
        uc = [a.reshape(b, l, -1) for a in uc]

        w_bd = (eye_g[:, None, :, None] * fourier_w[li][:, :, None, :]).reshape(MIX, MIX)
        f_lat = _fourier(ul[0], tab_s, c64, s64, w_bd)
        h_lat, h_ctx = _hgrn(ul[1:6], uc[1:6], lower[li], norm_g4[li], with_ctx)
        n_lat = _na_latent(ul[6], ul[7], ul[8], uc[7], uc[8], _na_bias_table(na_rpb[li]))

        if with_ctx:
            f_ctx = _fourier(uc[0], tab_l, c64, s64, w_bd)
            n_ctx = _na_context(uc[6], uc[7], uc[8])
            xc = _ffn(xc, f_ctx.reshape(b * l, -1), h_ctx.reshape(b * l, -1),
                      n_ctx.reshape(b * l, -1), norm2_g[li], final_norm_g, modv, base + b, None,
                      wo, wg, wu, wd, li, tm_c, False)
        xl = _ffn(xl, f_lat.reshape(b * s, -1), h_lat.reshape(b * s, -1), n_lat.reshape(b * s, -1),
                  norm2_g[li], final_norm_g, modv, base, s, wo, wg, wu, wd, li, tm_s, final)
    return xl.reshape(b, s, d)
```

```python
import functools

import jax
import jax.numpy as jnp
import numpy as np
from jax import lax
from jax.experimental import pallas as pl
from jax.experimental.pallas import tpu as pltpu

F32 = jnp.float32
BF16 = jnp.bfloat16

EPS = 1e-6
NEG_INF = -1e30
GRID_W = 64
FOURIER_GROUPS = 4
FOURIER_GD = 64
HGRN_DK = 64
HGRN_HEADS = 4
HGRN_CHUNK = 32
NA_HEAD_DIM = 64
NA_HEADS = 8
NA_KH = 8
NA_KW = 16
MIX = 256
NA_WIDTH = NA_HEADS * NA_HEAD_DIM
MOD_ROWS = 16
MXU_DIM = 256
FF_CHUNK = MXU_DIM
TOKEN_TILE = 512
INPROJ_TOKEN_TILE = 1024
MOD_COL_TILE = 1024
VMEM_LIMIT = 56 * 1024 * 1024


def _cparams(n_axes):
    return pltpu.CompilerParams(dimension_semantics=("arbitrary",) * n_axes,
                                vmem_limit_bytes=VMEM_LIMIT)


def _split2(a):
    hi = a.astype(BF16)
    lo = (a - hi.astype(F32)).astype(BF16)
    return hi, lo


def _dot(a, b):
    return jnp.dot(a, b, preferred_element_type=F32)


def _dot_nt(a, b):
    return lax.dot_general(a, b, (((1,), (1,)), ((), ())), preferred_element_type=F32)


def _dot_hilo(a, b):
    ah, al = _split2(a)
    bh, bl = _split2(b)
    return _dot(ah, bh) + _dot(al, bh) + _dot(ah, bl)


def _sigmoid(z):
    return 1.0 / (1.0 + jnp.exp(-z))


def _const_spec(shape):
    nd = len(shape)
    return pl.BlockSpec(shape, lambda *_: (0,) * nd, pipeline_mode=pl.Buffered(1))


def _layer_spec(w, li):
    return pl.BlockSpec((None,) + w.shape[1:], lambda *_: (li,) + (0,) * (w.ndim - 1),
                        pipeline_mode=pl.Buffered(1))


def _mod_kernel(c_ref, w_ref, b_ref, o_ref):
    c = c_ref[...]
    s = c * _sigmoid(c)
    o_ref[0] = _dot_hilo(s, w_ref[0]) + b_ref[0]


def _modulation(cc, w_mod, b_mod):
    depth, d, n = w_mod.shape
    tn = MOD_COL_TILE
    return pl.pallas_call(
        _mod_kernel,
        grid=(depth, n // tn),
        in_specs=[pl.BlockSpec((MOD_ROWS, d), lambda l, j: (0, 0)),
                  pl.BlockSpec((1, d, tn), lambda l, j: (l, 0, j)),
                  pl.BlockSpec((1, 1, tn), lambda l, j: (l, 0, j))],
        out_specs=pl.BlockSpec((1, MOD_ROWS, tn), lambda l, j: (l, 0, j)),
        out_shape=jax.ShapeDtypeStruct((depth, MOD_ROWS, n), F32),
        compiler_params=_cparams(2),
        name="modulation",
    )(cc, w_mod, b_mod.reshape(depth, 1, n))


def _inproj_kernel(x_ref, g_ref, sh_ref, sc_ref, w_ref,
                   uf_ref, hq_ref, zf_ref, zb_ref, hi_ref, hg_ref, q_ref, k_ref, v_ref):
    x = x_ref[...]
    y = x * lax.rsqrt(jnp.mean(x * x, axis=-1, keepdims=True) + EPS) * g_ref[...]
    a = (y * (1.0 + sc_ref[0]) + sh_ref[0]).astype(BF16)
    col = 0
    for ref in (uf_ref, hq_ref, zf_ref, zb_ref, hi_ref, hg_ref):
        ref[...] = _dot(a, w_ref[:, col:col + MIX]).astype(ref.dtype)
        col += MIX
    scale = NA_HEAD_DIM ** -0.5
    q_ref[...] = (_dot(a, w_ref[:, col:col + NA_WIDTH]) * scale).astype(BF16)
    col += NA_WIDTH
    k_ref[...] = _dot(a, w_ref[:, col:col + NA_WIDTH]).astype(BF16)
    col += NA_WIDTH
    v_ref[...] = _dot(a, w_ref[:, col:col + NA_WIDTH]).astype(BF16)


def _inproj(x2d, norm_g, modv, mod_base, rows_per_mod, w_in_b, li, tm):
    n, d = x2d.shape
    in_w = w_in_b.shape[2]

    def mod_idx(j):
        if rows_per_mod is None:
            return lambda i: (mod_base * 6 + j, 0, 0)
        tpm = rows_per_mod // tm
        return lambda i: ((mod_base + i // tpm) * 6 + j, 0, 0)

    def tok(w):
        return pl.BlockSpec((tm, w), lambda i: (i, 0))

    outs = [(MIX, BF16), (MIX, BF16), (MIX, F32), (MIX, F32), (MIX, BF16), (MIX, BF16),
            (NA_WIDTH, BF16), (NA_WIDTH, BF16), (NA_WIDTH, BF16)]
    return pl.pallas_call(
        _inproj_kernel,
        grid=(n // tm,),
        in_specs=[tok(d),
                  pl.BlockSpec((1, d), lambda i: (0, 0)),
                  pl.BlockSpec((1, 1, d), mod_idx(0)),
                  pl.BlockSpec((1, 1, d), mod_idx(1)),
                  _layer_spec(w_in_b, li)],
        out_specs=[tok(w) for w, _ in outs],
        out_shape=[jax.ShapeDtypeStruct((n, w), dt) for w, dt in outs],
        compiler_params=_cparams(1),
        name="inproj",
    )(x2d, norm_g.reshape(1, d), modv, modv, w_in_b)


def _dft_table(t_len):
    kb = min(64, t_len)
    assert t_len % kb == 0
    t = jnp.arange(t_len // 2, dtype=jnp.int32)[None, :]
    k_hi = jnp.arange(t_len // kb, dtype=jnp.int32)[:, None] * kb
    k_lo = jnp.arange(kb, dtype=jnp.int32)[:, None]
    w = 2.0 * np.pi / t_len
    ang_hi = ((k_hi * t) % t_len).astype(F32) * w
    ang_lo = ((k_lo * t) % t_len).astype(F32) * w
    ch, sh = jnp.cos(ang_hi)[:, None, :], jnp.sin(ang_hi)[:, None, :]
    cl, sl = jnp.cos(ang_lo)[None, :, :], jnp.sin(ang_lo)[None, :, :]
    cos = (ch * cl - sh * sl).reshape(t_len, t_len // 2)
    sin = (sh * cl + ch * sl).reshape(t_len, t_len // 2)
    return cos.astype(BF16), sin.astype(BF16)


def _channel_dft_consts():
    cd = np.outer(np.arange(FOURIER_GD), np.arange(FOURIER_GD)) % FOURIER_GD
    ang = 2.0 * np.pi * cd / FOURIER_GD
    eye = np.eye(FOURIER_GROUPS)
    return (jnp.asarray(np.kron(eye, np.cos(ang)), F32),
            jnp.asarray(np.kron(eye, np.sin(ang)), F32))


def _fourier_kernel(u_ref, cos_ref, sin_ref, c64_ref, s64_ref, w_ref, o_ref, pq_ref, pm_ref,
                    *, t_len, tm):
    half = t_len // 2
    bs = min(MXU_DIM, half)
    nb = half // bs
    bi = pl.program_id(1)

    @pl.when(pl.program_id(0) == 0)
    def _():
        norm = (FOURIER_GD * t_len) ** -0.5
        w = w_ref[...]
        a = (_dot_hilo(c64_ref[...], w) * norm).astype(BF16)
        b = (_dot_hilo(s64_ref[...], w) * (-norm)).astype(BF16)
        p_i = lax.broadcasted_iota(jnp.int32, (bs, bs), 0)
        q_i = lax.broadcasted_iota(jnp.int32, (bs, bs), 1)
        rev_shift = jnp.where(q_i == bs - p_i, 1.0, 0.0).astype(BF16)
        first_row = jnp.where(p_i + q_i == 0, 1.0, 0.0).astype(BF16)
        for i in range(nb):
            src = half + (nb - 1 - i) * bs
            r = _dot(rev_shift, u_ref[0, src:src + bs, :])
            if i > 0:
                r = r + _dot(first_row, u_ref[0, src + bs:src + 2 * bs, :])
            u = u_ref[0, i * bs:(i + 1) * bs, :].astype(F32)
            pq_ref[bi, i * bs:(i + 1) * bs, :] = _dot((u + r).astype(BF16), a).astype(BF16)
            pq_ref[bi, half + i * bs:half + (i + 1) * bs, :] = _dot((u - r).astype(BF16),
                                                                    b).astype(BF16)
        pm_ref[bi] = _dot(u_ref[0, half:half + 8, :], a)

    k = pl.program_id(0) * tm + lax.broadcasted_iota(jnp.int32, (tm, 1), 0)
    sign = (1 - 2 * (k & 1)).astype(F32)
    o_ref[0] = (_dot(cos_ref[...], pq_ref[bi, 0:half, :]) + _dot(sin_ref[...], pq_ref[bi, half:, :])
                + sign * pm_ref[bi, 0:1, :]).astype(BF16)


def _fourier(u, tables, c64, s64, w_bd):
    b, t_len, _ = u.shape
    assert t_len % 64 == 0
    tm = min(TOKEN_TILE, t_len)
    half_spec = pl.BlockSpec((tm, t_len // 2), lambda j, i: (j, 0))
    return pl.pallas_call(
        functools.partial(_fourier_kernel, t_len=t_len, tm=tm),
        grid=(t_len // tm, b),
        in_specs=[pl.BlockSpec((1, t_len, MIX), lambda j, i: (jnp.where(j == 0, i, b - 1), 0, 0)),
                  half_spec, half_spec,
                  _const_spec((MIX, MIX)), _const_spec((MIX, MIX)), _const_spec((MIX, MIX))],
        out_specs=pl.BlockSpec((1, tm, MIX), lambda j, i: (i, j, 0)),
        out_shape=jax.ShapeDtypeStruct((b, t_len, MIX), BF16),
        scratch_shapes=[pltpu.VMEM((b, t_len, MIX), BF16), pltpu.VMEM((b, 8, MIX), F32)],
        compiler_params=_cparams(2),
        name="fourier",
    )(u, tables[0], tables[1], c64, s64, w_bd)


def _hgrn_kernel(ql_ref, zfl_ref, zbl_ref, il_ref, gl_ref,
                 qc_ref, zfc_ref, zbc_ref, ic_ref, gc_ref, lb_ref, ng_ref,
                 *rest, n_lat, n_ctx, with_ctx):
    if with_ctx:
        ol_ref, oc_ref, st_ref, qd_ref, ks_ref, vt_ref, eal_ref, os_ref = rest
    else:
        ol_ref, st_ref, qd_ref, ks_ref, vt_ref, eal_ref, os_ref = rest
        oc_ref = None
    C = HGRN_CHUNK
    P2 = 2 * C
    W = MIX
    lane_head =lax.broadcasted_iota(jnp.int32, (1, W), 1) // HGRN_DK
    head_mask = [lane_head == h for h in range(HGRN_HEADS)]
    bd = (lax.broadcasted_iota(jnp.int32, (W, W), 0) // HGRN_DK
          == lax.broadcasted_iota(jnp.int32, (W, W), 1) // HGRN_DK)
    bd_ones = jnp.where(bd, 1.0, 0.0).astype(BF16)

    def stack_heads(x):
        return jnp.concatenate([jnp.where(m, x, jnp.zeros_like(x)) for m in head_mask], axis=0)

    def group_chunks(n_chunks):
        return 4 if n_chunks % 4 == 0 else 2

    def phase_a(q_ref, zf_ref, zb_ref, i_ref, n_chunks, need_o):
        gc = group_chunks(n_chunks)
        gt = gc * C
        t_i = lax.broadcasted_iota(jnp.int32, (gt, gt), 0)
        s_i = lax.broadcasted_iota(jnp.int32, (gt, gt), 1)
        same = (t_i // C) == (s_i // C)
        allow = (same & (s_i <= t_i), same & (s_i >= t_i))
        allow_b16 = tuple(jnp.where(m, 1.0, 0.0).astype(BF16) for m in allow)
        t4 = lax.broadcasted_iota(jnp.int32, (gt, HGRN_HEADS * gt), 0)
        s4 = lax.broadcasted_iota(jnp.int32, (gt, HGRN_HEADS * gt), 1) % gt
        same4 = (t4 // C) == (s4 // C)
        allow4 = (same4 & (s4 <= t4), same4 & (s4 >= t4))
        pair4 = (t4 // P2) == (s4 // P2)
        cross4 = (pair4 & (t4 // C == s4 // C + 1), pair4 & (t4 // C + 1 == s4 // C))
        ones_row = jnp.ones((C, W), F32)

        def body(g, carry):
            rows = pl.ds(pl.multiple_of(g * gt, gt), gt)
            v = i_ref[0, rows, :]
            vs = stack_heads(v) if need_o else None
            for j in range(gc // 2):
                vt_ref[g * (gc // 2) + j] = v[j * P2:(j + 1) * P2, :].T
            fs, parts = [], []
            for d, z_ref in ((0, zf_ref), (1, zb_ref)):
                lb = lb_ref[d:d + 1, :]
                f = lb + (1.0 - lb) * _sigmoid(z_ref[0, rows, :])
                lf = jnp.log(f)
                fs.append(f)
                parts.append(_split2(lf))
            a_s = [_dot(allow_b16[d], parts[d][0]) + _dot(allow_b16[d], parts[d][1])
                   for d in (0, 1)]
            scs = []
            for d in (0, 1):
                a = a_s[d]
                kd = (1.0 - fs[d]) * jnp.exp(-a)
                ends = [a[c * C + C - 1:c * C + C, :] if d == 0 else a[c * C:c * C + 1, :]
                        for c in range(gc)]
                eal = [jnp.exp(e) for e in ends]
                ebc = [jnp.broadcast_to(e, (C, W)) for e in eal]
                is_first = [(c % 2 == 0) == (d == 0) for c in range(gc)]
                k_mul = jnp.concatenate(
                    [jnp.broadcast_to(eal[c] * eal[c ^ 1], (C, W)) if is_first[c] else ebc[c]
                     for c in range(gc)], axis=0)
                ks_ref[d, rows, :] = (kd * k_mul).astype(BF16)
                n8 = (gc // 2) * 8
                eal_ref[d, pl.ds(pl.multiple_of(g * n8, n8), n8), :] = jnp.concatenate(
                    [jnp.broadcast_to(eal[2 * j] * eal[2 * j + 1], (8, W)) for j in range(gc // 2)],
                    axis=0)
                if need_o:
                    qd = q_ref[0, rows, :].astype(F32) * jnp.exp(a)
                    q_mul = jnp.concatenate([ones_row if is_first[c] else ebc[c ^ 1]
                                             for c in range(gc)], axis=0)
                    qd2 = (qd * q_mul).astype(BF16)
                    qd_ref[d, rows, :] = qd2
                    s2 = _dot_nt(jnp.concatenate([qd.astype(BF16), qd2], axis=0),
                                 stack_heads(kd.astype(BF16)))
                    scs.append(jnp.where(allow4[d], s2[:gt],
                                         jnp.where(cross4[d], s2[gt:], 0.0)).astype(BF16))
            if need_o:
                os_ref[rows, :] = _dot(scs[0], vs) + _dot(scs[1], vs)
            return carry

        lax.fori_loop(0, n_chunks // gc, body, 0, unroll=4)

    def phase_b(n_chunks, need_o):
        n_steps = n_chunks // 2
        gc = next(t for t in (4, 2, 1) if n_steps % t == 0)
        pw = 2 * HGRN_DK
        pairs = [slice(p * pw, (p + 1) * pw) for p in range(HGRN_HEADS // 2)]
        same_head = (lax.broadcasted_iota(jnp.int32, (pw, pw), 0) // HGRN_DK
                     == lax.broadcasted_iota(jnp.int32, (pw, pw), 1) // HGRN_DK)

        def step_rows(m):
            return pl.ds(pl.multiple_of(m * P2, P2), P2)

        def body(g, carry):
            idx = [[g * gc + c, n_steps - 1 - (g * gc + c)] for c in range(gc)]

            def updates(c):
                out = []
                for d, m in enumerate(idx[c]):
                    vt = vt_ref[m]
                    ks = ks_ref[d, step_rows(m), :]
                    out.append([jnp.where(same_head, _dot(vt[s, :], ks[:, s]), 0.0) for s in pairs])
                return out

            upd = {0: updates(0)}
            inters = []
            for c in range(gc):
                if c + 1 < gc:
                    upd[c + 1] = updates(c + 1)
                for d in (0, 1):
                    m = idx[c][d]
                    rows = step_rows(m)
                    eal = eal_ref[d, pl.ds(pl.multiple_of(m * 8, 8), 8), :][0:1, :]
                    sts = [st_ref[d, p] for p in range(len(pairs))]
                    if need_o:
                        qd = qd_ref[d, rows, :]
                        inter = [_dot_nt(qd[:, s], st.astype(BF16)) for s, st in zip(pairs, sts)]
                        inters.append((rows, jnp.concatenate(inter, axis=1)))
                    for p, s in enumerate(pairs):
                        st_ref[d, p] = sts[p] * eal[:, s] + upd[c][d][p]
            for rows, inter in inters:
                os_ref[rows, :] = os_ref[rows, :] + inter
            return carry

        lax.fori_loop(0, n_steps // gc, body, 0, unroll=2)

    def phase_c(g_ref, o_ref, n_rows):
        rt = next(t for t in (256, 128, 64) if n_rows % t == 0)

        def body(t, carry):
            rows = pl.ds(pl.multiple_of(t * rt, rt), rt)
            o = os_ref[rows, :]
            hi, lo = _split2(o * o)
            ms = (_dot(hi, bd_ones) + _dot(lo, bd_ones)) * (1.0 / HGRN_DK)
            y = o * lax.rsqrt(ms + EPS) * ng_ref[...]
            g = g_ref[0, rows, :].astype(F32)
            o_ref[0, rows, :] = (y * (g * _sigmoid(g))).astype(BF16)
            return carry

        lax.fori_loop(0, n_rows // rt, body, 0)

    st_ref[...] = jnp.zeros_like(st_ref)
    phase_a(qc_ref, zfc_ref, zbc_ref, ic_ref, n_ctx, with_ctx)
    phase_b(n_ctx, with_ctx)
    if with_ctx:
        phase_c(gc_ref, oc_ref, n_ctx * C)
    phase_a(ql_ref, zfl_ref, zbl_ref, il_ref, n_lat, True)
    phase_b(n_lat, True)
    phase_c(gl_ref, ol_ref, n_lat * C)


def _hgrn(lat, ctx, lb, norm_g4, with_ctx):
    b, s, _ = lat[0].shape
    l = ctx[0].shape[1]
    assert s % (2 * HGRN_CHUNK) == 0 and l % (2 * HGRN_CHUNK) == 0

    def seq(t):
        return pl.BlockSpec((1, t, MIX), lambda i: (i, 0, 0))

    out_specs = [pl.BlockSpec((1, s, MIX), lambda i: (i, 0, 0))]
    out_shape = [jax.ShapeDtypeStruct((b, s, MIX), BF16)]
    t_max = max(s, l)
    scratch = [pltpu.VMEM((2, HGRN_HEADS // 2, 2 * HGRN_DK, 2 * HGRN_DK), F32),
               pltpu.VMEM((2, t_max, MIX), BF16),
               pltpu.VMEM((2, t_max, MIX), BF16),
               pltpu.VMEM((t_max // (2 * HGRN_CHUNK), MIX, 2 * HGRN_CHUNK), BF16),
               pltpu.VMEM((2, t_max // (2 * HGRN_CHUNK) * 8, MIX), F32),
               pltpu.VMEM((t_max, MIX), F32)]
    if with_ctx:
        out_specs.append(pl.BlockSpec((1, l, MIX), lambda i: (i, 0, 0)))
        out_shape.append(jax.ShapeDtypeStruct((b, l, MIX), BF16))
    res = pl.pallas_call(
        functools.partial(_hgrn_kernel, n_lat=s // HGRN_CHUNK, n_ctx=l // HGRN_CHUNK,
                          with_ctx=with_ctx),
        grid=(b,),
        in_specs=[seq(s)] * 5 + [seq(l)] * 5 + [_const_spec((2, MIX)), _const_spec((1, MIX))],
        out_specs=out_specs,
        out_shape=out_shape,
        scratch_shapes=scratch,
        compiler_params=_cparams(1),
        name="hgrn",
    )(*lat, *ctx, lb, norm_g4)
    return (res[0], res[1]) if with_ctx else (res[0], None)


def _na_bias_table(rpb):
    n_h, nr, nc = rpb.shape
    w = GRID_W
    assert 2 * w == 128 and nc <= w
    padded = jnp.pad(rpb.astype(F32), ((0, 0), (0, 16 - nr), (0, 2 * w - nc)))
    return pl.pallas_call(
        _na_bias_kernel,
        grid=(n_h // 2,),
        in_specs=[pl.BlockSpec((2, 16, 2 * w), lambda i: (i, 0, 0))],
        out_specs=pl.BlockSpec((1, nr - 1, 2 * w, 2 * w), lambda i: (i, 0, 0, 0)),
        out_shape=jax.ShapeDtypeStruct((n_h // 2, nr - 1, 2 * w, 2 * w), F32),
        compiler_params=_cparams(1),
        name="na_bias",
    )(padded)


def _na_bias_kernel(rpb_ref, o_ref):
    w = GRID_W
    q = lax.broadcasted_iota(jnp.int32, (w, 2 * w), 0)
    lane = lax.broadcasted_iota(jnp.int32, (w, 2 * w), 1)
    kc = lane % w
    win = jnp.clip(q - NA_KW // 2, 0, w - NA_KW)
    valid = (kc >= win) & (kc < win + NA_KW)
    lower = lane < w
    shift_lo = 2 * w - (NA_KW - 1)
    shift_hi = shift_lo + w
    for h in range(2):
        for j in range(2 * NA_KH - 2):
            x0 = jnp.broadcast_to(rpb_ref[h, j:j + 1, :], (w, 2 * w))
            x1 = jnp.broadcast_to(rpb_ref[h, j + 1:j + 2, :], (w, 2 * w))
            t0 = pltpu.roll(x0, shift_lo, 1, stride=1, stride_axis=0)
            t1 = pltpu.roll(x1, shift_hi % (2 * w), 1, stride=1, stride_axis=0)
            o_ref[0, j, h * w:(h + 1) * w, :] = jnp.where(valid, jnp.where(lower, t0, t1), NEG_INF)


def _softmax_pv(s_list, v_list):
    m = functools.reduce(jnp.maximum, [jnp.max(s, axis=-1, keepdims=True) for s in s_list])
    p_list = [jnp.exp(s - m) for s in s_list]
    l = functools.reduce(jnp.add, [jnp.sum(p, axis=-1, keepdims=True) for p in p_list])
    o = functools.reduce(jnp.add, [_dot(p.astype(BF16), v) for p, v in zip(p_list, v_list)])
    return o * (1.0 / l)


NA_ROWS_PER_STEP = 16


def _na_kernel(q_ref, k_ref, v_ref, kc_ref, vc_ref, bias_ref, o_ref, *, rows):
    lane = lax.broadcasted_iota(jnp.int32, (1, 2 * NA_HEAD_DIM), 1)
    first = lane < NA_HEAD_DIM
    kc = kc_ref[0]
    vc = vc_ref[0]
    win = NA_KH * GRID_W
    nq = 2 * GRID_W

    def body(g, carry):
        qs, kws, vws, dis, qrows = [], [], [], [], []
        for j in range(NA_ROWS_PER_STEP):
            r = g * NA_ROWS_PER_STEP + j
            rs = jnp.clip(r - NA_KH // 2, 0, rows - NA_KH)
            dis.append(r - rs)
            qrows.append(pl.ds(pl.multiple_of(r * GRID_W, GRID_W), GRID_W))
            krows = pl.ds(pl.multiple_of(rs * GRID_W, GRID_W), win)
            q = q_ref[0, qrows[-1], :]
            qs += [jnp.where(first, q, jnp.zeros_like(q)), jnp.where(first, jnp.zeros_like(q), q)]
            kws.append(k_ref[0, krows, :])
            vws.append(v_ref[0, krows, :])
        qs = jnp.concatenate(qs, axis=0)
        sx = _dot_nt(qs, kc)
        pws, pxs, inv_ls = [], [], []
        for j in range(NA_ROWS_PER_STEP):
            ri0 = NA_KH - 1 - dis[j]
            bias = jnp.concatenate([bias_ref[0, ri0 + 2 * c] for c in range(NA_KH // 2)], axis=1)
            sw = _dot_nt(qs[j * nq:(j + 1) * nq], kws[j]) + bias
            sxj = sx[j * nq:(j + 1) * nq]
            m = jnp.maximum(jnp.max(sw, axis=-1, keepdims=True), jnp.max(sxj, axis=-1, keepdims=True))
            pw = jnp.exp(sw - m)
            px = jnp.exp(sxj - m)
            inv_ls.append(1.0 / (jnp.sum(pw, axis=-1, keepdims=True) + jnp.sum(px, axis=-1, keepdims=True)))
            pws.append(pw.astype(BF16))
            pxs.append(px.astype(BF16))
        ows = [_dot(pws[j], vws[j]) for j in range(NA_ROWS_PER_STEP)]
        oc = _dot(jnp.concatenate(pxs, axis=0), vc)
        for j in range(NA_ROWS_PER_STEP):
            o = (ows[j] + oc[j * nq:(j + 1) * nq]) * inv_ls[j]
            o_ref[0, qrows[j], :] = jnp.where(first, o[:GRID_W], o[GRID_W:]).astype(BF16)
        return carry

    lax.fori_loop(0, rows // NA_ROWS_PER_STEP, body, 0)


def _na_latent(q, k, v, kc, vc, bias):
    b, s, _ = q.shape
    l = kc.shape[1]
    rows = s // GRID_W
    assert s % GRID_W == 0 and rows >= NA_KH and rows % NA_ROWS_PER_STEP == 0
    hp = 2 * NA_HEAD_DIM

    def seq(t):
        return pl.BlockSpec((1, t, hp), lambda i, j: (i, 0, j))

    return pl.pallas_call(
        functools.partial(_na_kernel, rows=rows),
        grid=(b, NA_HEADS // 2),
        in_specs=[seq(s), seq(s), seq(s), seq(l), seq(l),
                  pl.BlockSpec((1, 2 * NA_KH - 2, 2 * GRID_W, 2 * GRID_W),
                               lambda i, j: (j, 0, 0, 0))],
        out_specs=seq(s),
        out_shape=jax.ShapeDtypeStruct((b, s, NA_WIDTH), BF16),
        compiler_params=_cparams(2),
        name="na_latent",
    )(q, k, v, kc, vc, bias)


def _na_ctx_kernel(q_ref, k_ref, v_ref, o_ref):
    lane = lax.broadcasted_iota(jnp.int32, (1, 2 * NA_HEAD_DIM), 1)
    first = lane < NA_HEAD_DIM
    for p in range(NA_HEADS // 2):
        cols = slice(p * 2 * NA_HEAD_DIM, (p + 1) * 2 * NA_HEAD_DIM)
        q, k, v = q_ref[0, :, cols], k_ref[0, :, cols], v_ref[0, :, cols]
        outs = []
        for h in range(2):
            qh = jnp.where(first if h == 0 else ~first, q, jnp.zeros_like(q))
            outs.append(_softmax_pv([_dot_nt(qh, k)], [v]))
        o_ref[0, :, cols] = jnp.where(first, outs[0], outs[1]).astype(BF16)


def _na_context(q, k, v):
    b, l, _ = q.shape
    spec = pl.BlockSpec((1, l, NA_WIDTH), lambda i: (i, 0, 0))
    return pl.pallas_call(
        _na_ctx_kernel,
        grid=(b,),
        in_specs=[spec, spec, spec],
        out_specs=spec,
        out_shape=jax.ShapeDtypeStruct((b, l, NA_WIDTH), BF16),
        compiler_params=_cparams(1),
        name="na_context",
    )(q, k, v)


def _ffn_kernel(x_ref, f_ref, h_ref, n_ref, g1_ref, sh_ref, sc_ref, g2_ref, ng_ref, fg_ref,
                wo_ref, wg_ref, wu_ref, wd_ref, o_ref, *, final):
    y = (_dot(f_ref[...], wo_ref[0:MIX, :]) + _dot(h_ref[...], wo_ref[MIX:2 * MIX, :])
         + _dot(n_ref[...], wo_ref[2 * MIX:, :]))
    x1 = x_ref[...] + g1_ref[0] * y
    hn = x1 * lax.rsqrt(jnp.mean(x1 * x1, axis=-1, keepdims=True) + EPS) * ng_ref[...]
    hm = (hn * (1.0 + sc_ref[0]) + sh_ref[0]).astype(BF16)
    d_ff = wg_ref.shape[1]
    acc = jnp.zeros(x1.shape, F32)
    for j in range(d_ff // FF_CHUNK):
        cols = slice(j * FF_CHUNK, (j + 1) * FF_CHUNK)
        g = _dot(hm, wg_ref[:, cols])
        u = _dot(hm, wu_ref[:, cols])
        act = (g * _sigmoid(g) * u).astype(BF16)
        acc = acc + _dot(act, wd_ref[cols, :])
    x2 = x1 + g2_ref[0] * acc
    if final:
        x2 = x2 * lax.rsqrt(jnp.mean(x2 * x2, axis=-1, keepdims=True) + EPS) * fg_ref[...]
    o_ref[...] = x2


def _ffn(x2d, f, h, n, norm_g, final_g, modv, mod_base, rows_per_mod, wo, wg, wu, wd, li, tm, final):
    nt, d = x2d.shape
    d_ff = wg.shape[2]
    assert d_ff % FF_CHUNK == 0

    def mod_idx(j):
        if rows_per_mod is None:
            return lambda i: (mod_base * 6 + j, 0, 0)
        tpm = rows_per_mod // tm
        return lambda i: ((mod_base + i // tpm) * 6 + j, 0, 0)

    def tok(w):
        return pl.BlockSpec((tm, w), lambda i: (i, 0))

    def vec():
        return pl.BlockSpec((1, d), lambda i: (0, 0))

    return pl.pallas_call(
        functools.partial(_ffn_kernel, final=final),
        grid=(nt // tm,),
        in_specs=[tok(d), tok(MIX), tok(MIX), tok(NA_WIDTH),
                  pl.BlockSpec((1, 1, d), mod_idx(2)), pl.BlockSpec((1, 1, d), mod_idx(3)),
                  pl.BlockSpec((1, 1, d), mod_idx(4)), pl.BlockSpec((1, 1, d), mod_idx(5)),
                  vec(), vec(),
                  _layer_spec(wo, li), _layer_spec(wg, li), _layer_spec(wu, li),
                  _layer_spec(wd, li)],
        out_specs=tok(d),
        out_shape=jax.ShapeDtypeStruct((nt, d), F32),
        compiler_params=_cparams(1),
        name="outproj_ffn",
    )(x2d, f, h, n, modv, modv, modv, modv, norm_g.reshape(1, d), final_g.reshape(1, d),
      wo, wg, wu, wd)


def _token_tile(n, largest):
    for tm in (1024, 512, 256, 128, 64, 32, 16, 8):
        if tm <= largest and n % tm == 0:
            return tm
    raise ValueError(f"token count {n} is not a multiple of 8")


def kernel(x, c, ctx, c_ctx, w_mod, b_mod, norm1_g, w_in, fourier_w, hgrn_lb, hgrn_norm_g, na_rpb,
           w_out, norm2_g, w_ffn_gate, w_ffn_up, w_ffn_down, final_norm_g):
    b, s, d = x.shape
    l = ctx.shape[1]
    depth = w_mod.shape[0]
    assert b < MOD_ROWS and d == 2 * MIX + NA_WIDTH

    cc = jnp.zeros((MOD_ROWS, d), F32).at[:b].set(c).at[b].set(c_ctx)
    mod = _modulation(cc, w_mod, b_mod)
    modv = mod.reshape(depth * MOD_ROWS * 6, 1, d)

    lbp = jax.nn.softmax(hgrn_lb.astype(F32), axis=0)
    lower = jnp.cumsum(lbp, axis=0) - lbp[0:1]
    norm_g4 = jnp.tile(hgrn_norm_g.astype(F32), (1, HGRN_HEADS))[:, None, :]

    tab_s, tab_l = _dft_table(s), _dft_table(l)
    c64, s64 = _channel_dft_consts()
    eye_g = jnp.eye(FOURIER_GROUPS, dtype=F32)

    tm_s, tm_c = _token_tile(s, TOKEN_TILE), _token_tile(b * l, TOKEN_TILE)
    tm_in_s, tm_in_c = _token_tile(s, INPROJ_TOKEN_TILE), _token_tile(b * l, INPROJ_TOKEN_TILE)
    xl = x.reshape(b * s, d)
    xc = ctx.reshape(b * l, d)
    w_in_b, wo, wg = w_in.astype(BF16), w_out.astype(BF16), w_ffn_gate.astype(BF16)
    wu, wd = w_ffn_up.astype(BF16), w_ffn_down.astype(BF16)
    for li in range(depth):
        with_ctx = li < depth - 1
        final = li == depth - 1
        base = li * MOD_ROWS
        ul = _inproj(xl, norm1_g[li], modv, base, s, w_in_b, li, tm_in_s)
        uc = _inproj(xc, norm1_g[li], modv, base + b, None, w_in_b, li, tm_in_c)
        ul = [a.reshape(b, s, -1) for a in ul]
        uc = [a.reshape(b, l, -1) for a in uc]

        w_bd = (eye_g[:, None, :, None] * fourier_w[li][:, :, None, :]).reshape(MIX, MIX)
        f_lat = _fourier(ul[0], tab_s, c64, s64, w_bd)
        h_lat, h_ctx = _hgrn(ul[1:6], uc[1:6], lower[li], norm_g4[li], with_ctx)
        n_lat = _na_latent(ul[6], ul[7], ul[8], uc[7], uc[8], _na_bias_table(na_rpb[li]))

        if with_ctx:
            f_ctx = _fourier(uc[0], tab_l, c64, s64, w_bd)
            n_ctx = _na_context(uc[6], uc[7], uc[8])
            xc = _ffn(xc, f_ctx.reshape(b * l, -1), h_ctx.reshape(b * l, -1),
                      n_ctx.reshape(b * l, -1), norm2_g[li], final_norm_g, modv, base + b, None,
                      wo, wg, wu, wd, li, tm_c, False)
        xl = _ffn(xl, f_lat.reshape(b * s, -1), h_lat.reshape(b * s, -1), n_lat.reshape(b * s, -1),
                  norm2_g[li], final_norm_g, modv, base, s, wo, wg, wu, wd, li, tm_s, final)
    return xl.reshape(b, s, d)
```

```python
import functools

import jax
import jax.numpy as jnp
import numpy as np
from jax import lax
from jax.experimental import pallas as pl
from jax.experimental.pallas import tpu as pltpu

F32 = jnp.float32
BF16 = jnp.bfloat16

EPS = 1e-6
NEG_INF = -1e30
GRID_W = 64
FOURIER_GROUPS = 4
FOURIER_GD = 64
HGRN_DK = 64
HGRN_HEADS = 4
HGRN_CHUNK = 32
NA_HEAD_DIM = 64
NA_HEADS = 8
NA_KH = 8
NA_KW = 16
MIX = 256
NA_WIDTH = NA_HEADS * NA_HEAD_DIM
MOD_ROWS = 16
MXU_DIM = 256
FF_CHUNK = MXU_DIM
TOKEN_TILE = 512
INPROJ_TOKEN_TILE = 1024
MOD_COL_TILE = 1024
VMEM_LIMIT = 56 * 1024 * 1024


def _cparams(n_axes):
    return pltpu.CompilerParams(dimension_semantics=("arbitrary",) * n_axes,
                                vmem_limit_bytes=VMEM_LIMIT)


def _split2(a):
    hi = a.astype(BF16)
    lo = (a - hi.astype(F32)).astype(BF16)
    return hi, lo


def _dot(a, b):
    return jnp.dot(a, b, preferred_element_type=F32)


def _dot_nt(a, b):
    return lax.dot_general(a, b, (((1,), (1,)), ((), ())), preferred_element_type=F32)


def _dot_hilo(a, b):
    ah, al = _split2(a)
    bh, bl = _split2(b)
    return _dot(ah, bh) + _dot(al, bh) + _dot(ah, bl)


def _sigmoid(z):
    return 1.0 / (1.0 + jnp.exp(-z))


def _const_spec(shape):
    nd = len(shape)
    return pl.BlockSpec(shape, lambda *_: (0,) * nd, pipeline_mode=pl.Buffered(1))


def _layer_spec(w, li):
    return pl.BlockSpec((None,) + w.shape[1:], lambda *_: (li,) + (0,) * (w.ndim - 1),
                        pipeline_mode=pl.Buffered(1))


def _mod_kernel(c_ref, w_ref, b_ref, o_ref):
    c = c_ref[...]
    s = c * _sigmoid(c)
    o_ref[0] = _dot_hilo(s, w_ref[0]) + b_ref[0]


def _modulation(cc, w_mod, b_mod):
    depth, d, n = w_mod.shape
    tn = MOD_COL_TILE
    return pl.pallas_call(
        _mod_kernel,
        grid=(depth, n // tn),
        in_specs=[pl.BlockSpec((MOD_ROWS, d), lambda l, j: (0, 0)),
                  pl.BlockSpec((1, d, tn), lambda l, j: (l, 0, j)),
                  pl.BlockSpec((1, 1, tn), lambda l, j: (l, 0, j))],
        out_specs=pl.BlockSpec((1, MOD_ROWS, tn), lambda l, j: (l, 0, j)),
        out_shape=jax.ShapeDtypeStruct((depth, MOD_ROWS, n), F32),
        compiler_params=_cparams(2),
        name="modulation",
    )(cc, w_mod, b_mod.reshape(depth, 1, n))


def _inproj_kernel(x_ref, g_ref, sh_ref, sc_ref, w_ref,
                   uf_ref, hq_ref, zf_ref, zb_ref, hi_ref, hg_ref, q_ref, k_ref, v_ref):
    x = x_ref[...]
    y = x * lax.rsqrt(jnp.mean(x * x, axis=-1, keepdims=True) + EPS) * g_ref[...]
    a = (y * (1.0 + sc_ref[0]) + sh_ref[0]).astype(BF16)
    col = 0
    for ref in (uf_ref, hq_ref, zf_ref, zb_ref, hi_ref, hg_ref):
        ref[...] = _dot(a, w_ref[:, col:col + MIX]).astype(ref.dtype)
        col += MIX
    scale = NA_HEAD_DIM ** -0.5
    q_ref[...] = (_dot(a, w_ref[:, col:col + NA_WIDTH]) * scale).astype(BF16)
    col += NA_WIDTH
    k_ref[...] = _dot(a, w_ref[:, col:col + NA_WIDTH]).astype(BF16)
    col += NA_WIDTH
    v_ref[...] = _dot(a, w_ref[:, col:col + NA_WIDTH]).astype(BF16)


def _inproj(x2d, norm_g, modv, mod_base, rows_per_mod, w_in_b, li, tm):
    n, d = x2d.shape
    in_w = w_in_b.shape[2]

    def mod_idx(j):
        if rows_per_mod is None:
            return lambda i: (mod_base * 6 + j, 0, 0)
        tpm = rows_per_mod // tm
        return lambda i: ((mod_base + i // tpm) * 6 + j, 0, 0)

    def tok(w):
        return pl.BlockSpec((tm, w), lambda i: (i, 0))

    outs = [(MIX, BF16), (MIX, BF16), (MIX, F32), (MIX, F32), (MIX, BF16), (MIX, BF16),
            (NA_WIDTH, BF16), (NA_WIDTH, BF16), (NA_WIDTH, BF16)]
    return pl.pallas_call(
        _inproj_kernel,
        grid=(n // tm,),
        in_specs=[tok(d),
                  pl.BlockSpec((1, d), lambda i: (0, 0)),
                  pl.BlockSpec((1, 1, d), mod_idx(0)),
                  pl.BlockSpec((1, 1, d), mod_idx(1)),
                  _layer_spec(w_in_b, li)],
        out_specs=[tok(w) for w, _ in outs],
        out_shape=[jax.ShapeDtypeStruct((n, w), dt) for w, dt in outs],
        compiler_params=_cparams(1),
        name="inproj",
    )(x2d, norm_g.reshape(1, d), modv, modv, w_in_b)


def _dft_table(t_len):
    kb = min(64, t_len)
    assert t_len % kb == 0
    t = jnp.arange(t_len // 2, dtype=jnp.int32)[None, :]
    k_hi = jnp.arange(t_len // kb, dtype=jnp.int32)[:, None] * kb
    k_lo = jnp.arange(kb, dtype=jnp.int32)[:, None]
    w = 2.0 * np.pi / t_len
    ang_hi = ((k_hi * t) % t_len).astype(F32) * w
    ang_lo = ((k_lo * t) % t_len).astype(F32) * w
    ch, sh = jnp.cos(ang_hi)[:, None, :], jnp.sin(ang_hi)[:, None, :]
    cl, sl = jnp.cos(ang_lo)[None, :, :], jnp.sin(ang_lo)[None, :, :]
    cos = (ch * cl - sh * sl).reshape(t_len, t_len // 2)
    sin = (sh * cl + ch * sl).reshape(t_len, t_len // 2)
    return cos.astype(BF16), sin.astype(BF16)


def _channel_dft_consts():
    cd = np.outer(np.arange(FOURIER_GD), np.arange(FOURIER_GD)) % FOURIER_GD
    ang = 2.0 * np.pi * cd / FOURIER_GD
    eye = np.eye(FOURIER_GROUPS)
    return (jnp.asarray(np.kron(eye, np.cos(ang)), F32),
            jnp.asarray(np.kron(eye, np.sin(ang)), F32))


def _fourier_kernel(u_ref, cos_ref, sin_ref, c64_ref, s64_ref, w_ref, o_ref, pq_ref, pm_ref,
                    *, t_len, tm):
    half = t_len // 2
    bs = min(MXU_DIM, half)
    nb = half // bs
    bi = pl.program_id(1)

    @pl.when(pl.program_id(0) == 0)
    def _():
        norm = (FOURIER_GD * t_len) ** -0.5
        w = w_ref[...]
        a = (_dot_hilo(c64_ref[...], w) * norm).astype(BF16)
        b = (_dot_hilo(s64_ref[...], w) * (-norm)).astype(BF16)
        p_i = lax.broadcasted_iota(jnp.int32, (bs, bs), 0)
        q_i = lax.broadcasted_iota(jnp.int32, (bs, bs), 1)
        rev_shift = jnp.where(q_i == bs - p_i, 1.0, 0.0).astype(BF16)
        first_row = jnp.where(p_i + q_i == 0, 1.0, 0.0).astype(BF16)
        for i in range(nb):
            src = half + (nb - 1 - i) * bs
            r = _dot(rev_shift, u_ref[0, src:src + bs, :])
            if i > 0:
                r = r + _dot(first_row, u_ref[0, src + bs:src + 2 * bs, :])
            u = u_ref[0, i * bs:(i + 1) * bs, :].astype(F32)
            pq_ref[bi, i * bs:(i + 1) * bs, :] = _dot((u + r).astype(BF16), a).astype(BF16)
            pq_ref[bi, half + i * bs:half + (i + 1) * bs, :] = _dot((u - r).astype(BF16),
                                                                    b).astype(BF16)
        pm_ref[bi] = _dot(u_ref[0, half:half + 8, :], a)

    k = pl.program_id(0) * tm + lax.broadcasted_iota(jnp.int32, (tm, 1), 0)
    sign = (1 - 2 * (k & 1)).astype(F32)
    o_ref[0] = (_dot(cos_ref[...], pq_ref[bi, 0:half, :]) + _dot(sin_ref[...], pq_ref[bi, half:, :])
                + sign * pm_ref[bi, 0:1, :]).astype(BF16)


def _fourier(u, tables, c64, s64, w_bd):
    b, t_len, _ = u.shape
    assert t_len % 64 == 0
    tm = min(TOKEN_TILE, t_len)
    half_spec = pl.BlockSpec((tm, t_len // 2), lambda j, i: (j, 0))
    return pl.pallas_call(
        functools.partial(_fourier_kernel, t_len=t_len, tm=tm),
        grid=(t_len // tm, b),
        in_specs=[pl.BlockSpec((1, t_len, MIX), lambda j, i: (jnp.where(j == 0, i, b - 1), 0, 0)),
                  half_spec, half_spec,
                  _const_spec((MIX, MIX)), _const_spec((MIX, MIX)), _const_spec((MIX, MIX))],
        out_specs=pl.BlockSpec((1, tm, MIX), lambda j, i: (i, j, 0)),
        out_shape=jax.ShapeDtypeStruct((b, t_len, MIX), BF16),
        scratch_shapes=[pltpu.VMEM((b, t_len, MIX), BF16), pltpu.VMEM((b, 8, MIX), F32)],
        compiler_params=_cparams(2),
        name="fourier",
    )(u, tables[0], tables[1], c64, s64, w_bd)


def _hgrn_kernel(ql_ref, zfl_ref, zbl_ref, il_ref, gl_ref,
                 qc_ref, zfc_ref, zbc_ref, ic_ref, gc_ref, lb_ref, ng_ref,
                 *rest, n_lat, n_ctx, with_ctx):
    if with_ctx:
        ol_ref, oc_ref, st_ref, qd_ref, ks_ref, vt_ref, eal_ref, os_ref = rest
    else:
        ol_ref, st_ref, qd_ref, ks_ref, vt_ref, eal_ref, os_ref = rest
        oc_ref = None
    C = HGRN_CHUNK
    P2 = 2 * C
    W = MIX
    lane_head = lax.broadcasted_iota(jnp.int32, (1, W), 1) // HGRN_DK
    head_mask = [lane_head == h for h in range(HGRN_HEADS)]
    bd = (lax.broadcasted_iota(jnp.int32, (W, W), 0) // HGRN_DK
          == lax.broadcasted_iota(jnp.int32, (W, W), 1) // HGRN_DK)
    bd_ones = jnp.where(bd, 1.0, 0.0).astype(BF16)

    def stack_heads(x):
        return jnp.concatenate([jnp.where(m, x, jnp.zeros_like(x)) for m in head_mask], axis=0)

    def group_chunks(n_chunks):
        return 4 if n_chunks % 4 == 0 else 2

    def phase_a(q_ref, zf_ref, zb_ref, i_ref, n_chunks, need_o):
        gc = group_chunks(n_chunks)
        gt = gc * C
        t_i = lax.broadcasted_iota(jnp.int32, (gt, gt), 0)
        s_i = lax.broadcasted_iota(jnp.int32, (gt, gt), 1)
        same = (t_i // C) == (s_i // C)
        allow = (same & (s_i <= t_i), same & (s_i >= t_i))
        allow_b16 = tuple(jnp.where(m, 1.0, 0.0).astype(BF16) for m in allow)
        t4 = lax.broadcasted_iota(jnp.int32, (gt, HGRN_HEADS * gt), 0)
        s4 = lax.broadcasted_iota(jnp.int32, (gt, HGRN_HEADS * gt), 1) % gt
        same4 = (t4 // C) == (s4 // C)
        allow4 = (same4 & (s4 <= t4), same4 & (s4 >= t4))
        pair4 = (t4 // P2) == (s4 // P2)
        cross4 = (pair4 & (t4 // C == s4 // C + 1), pair4 & (t4 // C + 1 == s4 // C))
        ones_row = jnp.ones((C, W), F32)

        def body(g, carry):
            rows = pl.ds(pl.multiple_of(g * gt, gt), gt)
            v = i_ref[0, rows, :]
            vs = stack_heads(v) if need_o else None
            for j in range(gc // 2):
                vt_ref[g * (gc // 2) + j] = v[j * P2:(j + 1) * P2, :].T
            fs, parts = [], []
            for d, z_ref in ((0, zf_ref), (1, zb_ref)):
                lb = lb_ref[d:d + 1, :]
                f = lb + (1.0 - lb) * _sigmoid(z_ref[0, rows, :])
                lf = jnp.log(f)
                fs.append(f)
                parts.append(_split2(lf))
            a_s = [_dot(allow_b16[d], parts[d][0]) + _dot(allow_b16[d], parts[d][1])
                   for d in (0, 1)]
            scs = []
            for d in (0, 1):
                a = a_s[d]
                kd = (1.0 - fs[d]) * jnp.exp(-a)
                ends = [a[c * C + C - 1:c * C + C, :] if d == 0 else a[c * C:c * C + 1, :]
                        for c in range(gc)]
                eal = [jnp.exp(e) for e in ends]
                ebc = [jnp.broadcast_to(e, (C, W)) for e in eal]
                is_first = [(c % 2 == 0) == (d == 0) for c in range(gc)]
                k_mul = jnp.concatenate(
                    [jnp.broadcast_to(eal[c] * eal[c ^ 1], (C, W)) if is_first[c] else ebc[c]
                     for c in range(gc)], axis=0)
                ks_ref[d, rows, :] = (kd * k_mul).astype(BF16)
                n8 = (gc // 2) * 8
                eal_ref[d, pl.ds(pl.multiple_of(g * n8, n8), n8), :] = jnp.concatenate(
                    [jnp.broadcast_to(eal[2 * j] * eal[2 * j + 1], (8, W)) for j in range(gc // 2)],
                    axis=0)
                if need_o:
                    qd = q_ref[0, rows, :].astype(F32) * jnp.exp(a)
                    q_mul = jnp.concatenate([ones_row if is_first[c] else ebc[c ^ 1]
                                             for c in range(gc)], axis=0)
                    qd2 = (qd * q_mul).astype(BF16)
                    qd_ref[d, rows, :] = qd2
                    s2 = _dot_nt(jnp.concatenate([qd.astype(BF16), qd2], axis=0),
                                 stack_heads(kd.astype(BF16)))
                    scs.append(jnp.where(allow4[d], s2[:gt],
                                         jnp.where(cross4[d], s2[gt:], 0.0)).astype(BF16))
            if need_o:
                os_ref[rows, :] = _dot(scs[0], vs) + _dot(scs[1], vs)
            return carry

        lax.fori_loop(0, n_chunks // gc, body, 0, unroll=8)

    def phase_b(n_chunks, need_o):
        n_steps = n_chunks // 2
        gc = next(t for t in (4, 2, 1) if n_steps % t == 0)
        pw = 2 * HGRN_DK
        pairs = [slice(p * pw, (p + 1) * pw) for p in range(HGRN_HEADS // 2)]
        same_head = (lax.broadcasted_iota(jnp.int32, (pw, pw), 0) // HGRN_DK
                     == lax.broadcasted_iota(jnp.int32, (pw, pw), 1) // HGRN_DK)

        def step_rows(m):
            return pl.ds(pl.multiple_of(m * P2, P2), P2)

        def body(g, carry):
            idx = [[g * gc + c, n_steps - 1 - (g * gc + c)] for c in range(gc)]

            def updates(c):
                out = []
                for d, m in enumerate(idx[c]):
                    vt = vt_ref[m]
                    ks = ks_ref[d, step_rows(m), :]
                    out.append([jnp.where(same_head, _dot(vt[s, :], ks[:, s]), 0.0) for s in pairs])
                return out

            upd = {0: updates(0)}
            inters = []
            for c in range(gc):
                if c + 1 < gc:
                    upd[c + 1] = updates(c + 1)
                for d in (0, 1):
                    m = idx[c][d]
                    rows = step_rows(m)
                    eal = eal_ref[d, pl.ds(pl.multiple_of(m * 8, 8), 8), :][0:1, :]
                    sts = [st_ref[d, p] for p in range(len(pairs))]
                    if need_o:
                        qd = qd_ref[d, rows, :]
                        inter = [_dot_nt(qd[:, s], st.astype(BF16)) for s, st in zip(pairs, sts)]
                        inters.append((rows, jnp.concatenate(inter, axis=1)))
                    for p, s in enumerate(pairs):
                        st_ref[d, p] = sts[p] * eal[:, s] + upd[c][d][p]
            for rows, inter in inters:
                os_ref[rows, :] = os_ref[rows, :] + inter
            return carry

        lax.fori_loop(0, n_steps // gc, body, 0, unroll=4)

    def phase_c(g_ref, o_ref, n_rows):
        rt = next(t for t in (256, 128, 64) if n_rows % t == 0)

        def body(t, carry):
            rows = pl.ds(pl.multiple_of(t * rt, rt), rt)
            o = os_ref[rows, :]
            hi, lo = _split2(o * o)
            ms = (_dot(hi, bd_ones) + _dot(lo, bd_ones)) * (1.0 / HGRN_DK)
            y = o * lax.rsqrt(ms + EPS) * ng_ref[...]
            g = g_ref[0, rows, :].astype(F32)
            o_ref[0, rows, :] = (y * (g * _sigmoid(g))).astype(BF16)
            return carry

        lax.fori_loop(0, n_rows // rt, body, 0)

    st_ref[...] = jnp.zeros_like(st_ref)
    phase_a(qc_ref, zfc_ref, zbc_ref, ic_ref, n_ctx, with_ctx)
    phase_b(n_ctx, with_ctx)
    if with_ctx:
        phase_c(gc_ref, oc_ref, n_ctx * C)
    phase_a(ql_ref, zfl_ref, zbl_ref, il_ref, n_lat, True)
    phase_b(n_lat, True)
    phase_c(gl_ref, ol_ref, n_lat * C)


def _hgrn(lat, ctx, lb, norm_g4, with_ctx):
    b, s, _ = lat[0].shape
    l = ctx[0].shape[1]
    assert s % (2 * HGRN_CHUNK) == 0 and l % (2 * HGRN_CHUNK) == 0

    def seq(t):
        return pl.BlockSpec((1, t, MIX), lambda i: (i, 0, 0))

    out_specs = [pl.BlockSpec((1, s, MIX), lambda i: (i, 0, 0))]
    out_shape = [jax.ShapeDtypeStruct((b, s, MIX), BF16)]
    t_max = max(s, l)
    scratch = [pltpu.VMEM((2, HGRN_HEADS // 2, 2 * HGRN_DK, 2 * HGRN_DK), F32),
               pltpu.VMEM((2, t_max, MIX), BF16),
               pltpu.VMEM((2, t_max, MIX), BF16),
               pltpu.VMEM((t_max // (2 * HGRN_CHUNK), MIX, 2 * HGRN_CHUNK), BF16),
               pltpu.VMEM((2, t_max // (2 * HGRN_CHUNK) * 8, MIX), F32),
               pltpu.VMEM((t_max, MIX), F32)]
    if with_ctx:
        out_specs.append(pl.BlockSpec((1, l, MIX), lambda i: (i, 0, 0)))
        out_shape.append(jax.ShapeDtypeStruct((b, l, MIX), BF16))
    res = pl.pallas_call(
        functools.partial(_hgrn_kernel, n_lat=s // HGRN_CHUNK, n_ctx=l // HGRN_CHUNK,
                          with_ctx=with_ctx),
        grid=(b,),
        in_specs=[seq(s)] * 5 + [seq(l)] * 5 + [_const_spec((2, MIX)), _const_spec((1, MIX))],
        out_specs=out_specs,
        out_shape=out_shape,
        scratch_shapes=scratch,
        compiler_params=_cparams(1),
        name="hgrn",
    )(*lat, *ctx, lb, norm_g4)
    return (res[0], res[1]) if with_ctx else (res[0], None)


def _na_bias_table(rpb):
    n_h, nr, nc = rpb.shape
    w = GRID_W
    assert 2 * w == 128 and nc <= w
    padded = jnp.pad(rpb.astype(F32), ((0, 0), (0, 16 - nr), (0, 2 * w - nc)))
    return pl.pallas_call(
        _na_bias_kernel,
        grid=(n_h // 2,),
        in_specs=[pl.BlockSpec((2, 16, 2 * w), lambda i: (i, 0, 0))],
        out_specs=pl.BlockSpec((1, nr - 1, 2 * w, 2 * w), lambda i: (i, 0, 0, 0)),
        out_shape=jax.ShapeDtypeStruct((n_h // 2, nr - 1, 2 * w, 2 * w), F32),
        compiler_params=_cparams(1),
        name="na_bias",
    )(padded)


def _na_bias_kernel(rpb_ref, o_ref):
    w = GRID_W
    q = lax.broadcasted_iota(jnp.int32, (w, 2 * w), 0)
    lane = lax.broadcasted_iota(jnp.int32, (w, 2 * w), 1)
    kc = lane % w
    win = jnp.clip(q - NA_KW // 2, 0, w - NA_KW)
    valid = (kc >= win) & (kc < win + NA_KW)
    lower = lane < w
    shift_lo = 2 * w - (NA_KW - 1)
    shift_hi = shift_lo + w
    for h in range(2):
        for j in range(2 * NA_KH - 2):
            x0 = jnp.broadcast_to(rpb_ref[h, j:j + 1, :], (w, 2 * w))
            x1 = jnp.broadcast_to(rpb_ref[h, j + 1:j + 2, :], (w, 2 * w))
            t0 = pltpu.roll(x0, shift_lo, 1, stride=1, stride_axis=0)
            t1 = pltpu.roll(x1, shift_hi % (2 * w), 1, stride=1, stride_axis=0)
            o_ref[0, j, h * w:(h + 1) * w, :] = jnp.where(valid, jnp.where(lower, t0, t1), NEG_INF)


def _softmax_pv(s_list, v_list):
    m = functools.reduce(jnp.maximum, [jnp.max(s, axis=-1, keepdims=True) for s in s_list])
    p_list = [jnp.exp(s - m) for s in s_list]
    l = functools.reduce(jnp.add, [jnp.sum(p, axis=-1, keepdims=True) for p in p_list])
    o = functools.reduce(jnp.add, [_dot(p.astype(BF16), v) for p, v in zip(p_list, v_list)])
    return o * (1.0 / l)


NA_ROWS_PER_STEP = 32


def _na_kernel(q_ref, k_ref, v_ref, kc_ref, vc_ref, bias_ref, o_ref, *, rows):
    lane = lax.broadcasted_iota(jnp.int32, (1, 2 * NA_HEAD_DIM), 1)
    first = lane < NA_HEAD_DIM
    kc = kc_ref[0]
    vc = vc_ref[0]
    win = NA_KH * GRID_W
    nq = 2 * GRID_W

    def body(g, carry):
        qs, kws, vws, dis, qrows = [], [], [], [], []
        for j in range(NA_ROWS_PER_STEP):
            r = g * NA_ROWS_PER_STEP + j
            rs = jnp.clip(r - NA_KH // 2, 0, rows - NA_KH)
            dis.append(r - rs)
            qrows.append(pl.ds(pl.multiple_of(r * GRID_W, GRID_W), GRID_W))
            krows = pl.ds(pl.multiple_of(rs * GRID_W, GRID_W), win)
            q = q_ref[0, qrows[-1], :]
            qs += [jnp.where(first, q, jnp.zeros_like(q)), jnp.where(first, jnp.zeros_like(q), q)]
            kws.append(k_ref[0, krows, :])
            vws.append(v_ref[0, krows, :])
        qs = jnp.concatenate(qs, axis=0)
        sx = _dot_nt(qs, kc)
        pws, pxs, inv_ls = [], [], []
        for j in range(NA_ROWS_PER_STEP):
            ri0 = NA_KH - 1 - dis[j]
            bias = jnp.concatenate([bias_ref[0, ri0 + 2 * c] for c in range(NA_KH // 2)], axis=1)
            sw = _dot_nt(qs[j * nq:(j + 1) * nq], kws[j]) + bias
            sxj = sx[j * nq:(j + 1) * nq]
            m = jnp.maximum(jnp.max(sw, axis=-1, keepdims=True), jnp.max(sxj, axis=-1, keepdims=True))
            pw = jnp.exp(sw - m)
            px = jnp.exp(sxj - m)
            inv_ls.append(1.0 / (jnp.sum(pw, axis=-1, keepdims=True) + jnp.sum(px, axis=-1, keepdims=True)))
            pws.append(pw.astype(BF16))
            pxs.append(px.astype(BF16))
        ows = [_dot(pws[j], vws[j]) for j in range(NA_ROWS_PER_STEP)]
        oc = _dot(jnp.concatenate(pxs, axis=0), vc)
        for j in range(NA_ROWS_PER_STEP):
            o = (ows[j] + oc[j * nq:(j + 1) * nq]) * inv_ls[j]
            o_ref[0, qrows[j], :] = jnp.where(first, o[:GRID_W], o[GRID_W:]).astype(BF16)
        return carry

    lax.fori_loop(0, rows // NA_ROWS_PER_STEP, body, 0)


def _na_latent(q, k, v, kc, vc, bias):
    b, s, _ = q.shape
    l = kc.shape[1]
    rows = s // GRID_W
    assert s % GRID_W == 0 and rows >= NA_KH and rows % NA_ROWS_PER_STEP == 0
    hp = 2 * NA_HEAD_DIM

    def seq(t):
        return pl.BlockSpec((1, t, hp), lambda i, j: (i, 0, j))

    return pl.pallas_call(
        functools.partial(_na_kernel, rows=rows),
        grid=(b, NA_HEADS // 2),
        in_specs=[seq(s), seq(s), seq(s), seq(l), seq(l),
                  pl.BlockSpec((1, 2 * NA_KH - 2, 2 * GRID_W, 2 * GRID_W),
                               lambda i, j: (j, 0, 0, 0))],
        out_specs=seq(s),
        out_shape=jax.ShapeDtypeStruct((b, s, NA_WIDTH), BF16),
        compiler_params=_cparams(2),
        name="na_latent",
    )(q, k, v, kc, vc, bias)


def _na_ctx_kernel(q_ref, k_ref, v_ref, o_ref):
    lane = lax.broadcasted_iota(jnp.int32, (1, 2 * NA_HEAD_DIM), 1)
    first = lane < NA_HEAD_DIM
    for p in range(NA_HEADS // 2):
        cols = slice(p * 2 * NA_HEAD_DIM, (p + 1) * 2 * NA_HEAD_DIM)
        q, k, v = q_ref[0, :, cols], k_ref[0, :, cols], v_ref[0, :, cols]
        outs = []
        for h in range(2):
            qh = jnp.where(first if h == 0 else ~first, q, jnp.zeros_like(q))
            outs.append(_softmax_pv([_dot_nt(qh, k)], [v]))
        o_ref[0, :, cols] = jnp.where(first, outs[0], outs[1]).astype(BF16)


def _na_context(q, k, v):
    b, l, _ = q.shape
    spec = pl.BlockSpec((1, l, NA_WIDTH), lambda i: (i, 0, 0))
    return pl.pallas_call(
        _na_ctx_kernel,
        grid=(b,),
        in_specs=[spec, spec, spec],
        out_specs=spec,
        out_shape=jax.ShapeDtypeStruct((b, l, NA_WIDTH), BF16),
        compiler_params=_cparams(1),
        name="na_context",
    )(q, k, v)


def _ffn_kernel(x_ref, f_ref, h_ref, n_ref, g1_ref, sh_ref, sc_ref, g2_ref, ng_ref, fg_ref,
                wo_ref, wg_ref, wu_ref, wd_ref, o_ref, *, final):
    y = (_dot(f_ref[...], wo_ref[0:MIX, :]) + _dot(h_ref[...], wo_ref[MIX:2 * MIX, :])
         + _dot(n_ref[...], wo_ref[2 * MIX:, :]))
    x1 = x_ref[...] + g1_ref[0] * y
    hn = x1 * lax.rsqrt(jnp.mean(x1 * x1, axis=-1, keepdims=True) + EPS) * ng_ref[...]
    hm = (hn * (1.0 + sc_ref[0]) + sh_ref[0]).astype(BF16)
    d_ff = wg_ref.shape[1]
    acc = jnp.zeros(x1.shape, F32)
    for j in range(d_ff // FF_CHUNK):
        cols = slice(j * FF_CHUNK, (j + 1) * FF_CHUNK)
        g = _dot(hm, wg_ref[:, cols])
        u = _dot(hm, wu_ref[:, cols])
        act = (g * _sigmoid(g) * u).astype(BF16)
        acc = acc + _dot(act, wd_ref[cols, :])
    x2 = x1 + g2_ref[0] * acc
    if final:
        x2 = x2 * lax.rsqrt(jnp.mean(x2 * x2, axis=-1, keepdims=True) + EPS) * fg_ref[...]
    o_ref[...] = x2


def _ffn(x2d, f, h, n, norm_g, final_g, modv, mod_base, rows_per_mod, wo, wg, wu, wd, li, tm, final):
    nt, d = x2d.shape
    d_ff = wg.shape[2]
    assert d_ff % FF_CHUNK == 0

    def mod_idx(j):
        if rows_per_mod is None:
            return lambda i: (mod_base * 6 + j, 0, 0)
        tpm = rows_per_mod // tm
        return lambda i: ((mod_base + i // tpm) * 6 + j, 0, 0)

    def tok(w):
        return pl.BlockSpec((tm, w), lambda i: (i, 0))

    def vec():
        return pl.BlockSpec((1, d), lambda i: (0, 0))

    return pl.pallas_call(
        functools.partial(_ffn_kernel, final=final),
        grid=(nt // tm,),
        in_specs=[tok(d), tok(MIX), tok(MIX), tok(NA_WIDTH),
                  pl.BlockSpec((1, 1, d), mod_idx(2)), pl.BlockSpec((1, 1, d), mod_idx(3)),
                  pl.BlockSpec((1, 1, d), mod_idx(4)), pl.BlockSpec((1, 1, d), mod_idx(5)),
                  vec(), vec(),
                  _layer_spec(wo, li), _layer_spec(wg, li), _layer_spec(wu, li),
                  _layer_spec(wd, li)],
        out_specs=tok(d),
        out_shape=jax.ShapeDtypeStruct((nt, d), F32),
        compiler_params=_cparams(1),
        name="outproj_ffn",
    )(x2d, f, h, n, modv, modv, modv, modv, norm_g.reshape(1, d), final_g.reshape(1, d),
      wo, wg, wu, wd)


def _token_tile(n, largest):
    for tm in (1024, 512, 256, 128, 64, 32, 16, 8):
        if tm <= largest and n % tm == 0:
            return tm
    raise ValueError(f"token count {n} is not a multiple of 8")


def kernel(x, c, ctx, c_ctx, w_mod, b_mod, norm1_g, w_in, fourier_w, hgrn_lb, hgrn_norm_g, na_rpb,
           w_out, norm2_g, w_ffn_gate, w_ffn_up, w_ffn_down, final_norm_g):
    b, s, d = x.shape
    l = ctx.shape[1]
    depth = w_mod.shape[0]
    assert b < MOD_ROWS and d == 2 * MIX + NA_WIDTH

    cc = jnp.zeros((MOD_ROWS, d), F32).at[:b].set(c).at[b].set(c_ctx)
    mod = _modulation(cc, w_mod, b_mod)
    modv = mod.reshape(depth * MOD_ROWS * 6, 1, d)

    lbp = jax.nn.softmax(hgrn_lb.astype(F32), axis=0)
    lower = jnp.cumsum(lbp, axis=0) - lbp[0:1]
    norm_g4 = jnp.tile(hgrn_norm_g.astype(F32), (1, HGRN_HEADS))[:, None, :]

    tab_s, tab_l = _dft_table(s), _dft_table(l)
    c64, s64 = _channel_dft_consts()
    eye_g = jnp.eye(FOURIER_GROUPS, dtype=F32)

    tm_s, tm_c = _token_tile(s, TOKEN_TILE), _token_tile(b * l, TOKEN_TILE)
    tm_in_s, tm_in_c = _token_tile(s, INPROJ_TOKEN_TILE), _token_tile(b * l, INPROJ_TOKEN_TILE)
    xl = x.reshape(b * s, d)
    xc = ctx.reshape(b * l, d)
    w_in_b, wo, wg = w_in.astype(BF16), w_out.astype(BF16), w_ffn_gate.astype(BF16)
    wu, wd = w_ffn_up.astype(BF16), w_ffn_down.astype(BF16)
    for li in range(depth):
        with_ctx = li < depth - 1
        final = li == depth - 1
        base = li * MOD_ROWS
        ul = _inproj(xl, norm1_g[li], modv, base, s, w_in_b, li, tm_in_s)
        uc = _inproj(xc, norm1_g[li], modv, base + b, None, w_in_b, li, tm_in_c)
        ul = [a.reshape(b, s, -1) for a in ul]
        uc = [a.reshape(b, l, -1) for a in uc]

        w_bd = (eye_g[:, None, :, None] * fourier_w[li][:, :, None, :]).reshape(MIX, MIX)
        f_lat = _fourier(ul[0], tab_s, c64, s64, w_bd)
        h_lat, h_ctx = _hgrn(ul[1:6], uc[1:6], lower[li], norm_g4[li], with_ctx)
        n_lat = _na_latent(ul[6], ul[7], ul[8], uc[7], uc[8], _na_bias_table(na_rpb[li]))

        if with_ctx:
            f_ctx = _fourier(uc[0], tab_l, c64, s64, w_bd)
            n_ctx = _na_context(uc[6], uc[7], uc[8])
            xc = _ffn(xc, f_ctx.reshape(b * l, -1), h_ctx.reshape(b * l, -1),
                      n_ctx.reshape(b * l, -1), norm2_g[li], final_norm_g, modv, base + b, None,
                      wo, wg, wu, wd, li, tm_c, False)
        xl = _ffn(xl, f_lat.reshape(b * s, -1), h_lat.reshape(b * s, -1), n_lat.reshape(b * s, -1),
                  norm2_g[li], final_norm_g, modv, base, s, wo, wg, wu, wd, li, tm_s, final)
    return xl.reshape(b, s, d)
```

```python
import functools

import jax
import jax.numpy as jnp
import numpy as np
from jax import lax
from jax.experimental import pallas as pl
from jax.experimental.pallas import tpu as pltpu

F32 = jnp.float32
BF16 = jnp.bfloat16

EPS = 1e-6
NEG_INF = -1e30
GRID_W = 64
FOURIER_GROUPS = 4
FOURIER_GD = 64
HGRN_DK = 64
HGRN_HEADS = 4
HGRN_CHUNK = 32
NA_HEAD_DIM = 64
NA_HEADS = 8
NA_KH = 8
NA_KW = 16
MIX = 256
NA_WIDTH = NA_HEADS * NA_HEAD_DIM
MOD_ROWS = 16
MXU_DIM = 256
FF_CHUNK = MXU_DIM
TOKEN_TILE = 512
INPROJ_TOKEN_TILE = 1024
MOD_COL_TILE = 1024
VMEM_LIMIT = 56 * 1024 * 1024


def _cparams(n_axes):
    return pltpu.CompilerParams(dimension_semantics=("arbitrary",) * n_axes,
                                vmem_limit_bytes=VMEM_LIMIT)


def _split2(a):
    hi = a.astype(BF16)
    lo = (a - hi.astype(F32)).astype(BF16)
    return hi, lo


def _dot(a, b):
    return jnp.dot(a, b, preferred_element_type=F32)


def _dot_nt(a, b):
    return lax.dot_general(a, b, (((1,), (1,)), ((), ())), preferred_element_type=F32)


def _dot_hilo(a, b):
    ah, al = _split2(a)
    bh, bl = _split2(b)
    return _dot(ah, bh) + _dot(al, bh) + _dot(ah, bl)


def _sigmoid(z):
    return 1.0 / (1.0 + jnp.exp(-z))


def _const_spec(shape):
    nd = len(shape)
    return pl.BlockSpec(shape, lambda *_: (0,) * nd, pipeline_mode=pl.Buffered(1))


def _layer_spec(w, li):
    return pl.BlockSpec((None,) + w.shape[1:], lambda *_: (li,) + (0,) * (w.ndim - 1),
                        pipeline_mode=pl.Buffered(1))


def _mod_kernel(c_ref, w_ref, b_ref, o_ref):
    c = c_ref[...]
    s = c * _sigmoid(c)
    o_ref[0] = _dot_hilo(s, w_ref[0]) + b_ref[0]


def _modulation(cc, w_mod, b_mod):
    depth, d, n = w_mod.shape
    tn = MOD_COL_TILE
    return pl.pallas_call(
        _mod_kernel,
        grid=(depth, n // tn),
        in_specs=[pl.BlockSpec((MOD_ROWS, d), lambda l, j: (0, 0)),
                  pl.BlockSpec((1, d, tn), lambda l, j: (l, 0, j)),
                  pl.BlockSpec((1, 1, tn), lambda l, j: (l, 0, j))],
        out_specs=pl.BlockSpec((1, MOD_ROWS, tn), lambda l, j: (l, 0, j)),
        out_shape=jax.ShapeDtypeStruct((depth, MOD_ROWS, n), F32),
        compiler_params=_cparams(2),
        name="modulation",
    )(cc, w_mod, b_mod.reshape(depth, 1, n))


def _inproj_kernel(x_ref, g_ref, sh_ref, sc_ref, w_ref,
                   uf_ref, hq_ref, zf_ref, zb_ref, hi_ref, hg_ref, q_ref, k_ref, v_ref):
    x = x_ref[...]
    y = x * lax.rsqrt(jnp.mean(x * x, axis=-1, keepdims=True) + EPS) * g_ref[...]
    a = (y * (1.0 + sc_ref[0]) + sh_ref[0]).astype(BF16)
    col = 0
    for ref in (uf_ref, hq_ref, zf_ref, zb_ref, hi_ref, hg_ref):
        ref[...] = _dot(a, w_ref[:, col:col + MIX]).astype(ref.dtype)
        col += MIX
    scale = NA_HEAD_DIM ** -0.5
    q_ref[...] = (_dot(a, w_ref[:, col:col + NA_WIDTH]) * scale).astype(BF16)
    col += NA_WIDTH
    k_ref[...] = _dot(a, w_ref[:, col:col + NA_WIDTH]).astype(BF16)
    col += NA_WIDTH
    v_ref[...] = _dot(a, w_ref[:, col:col + NA_WIDTH]).astype(BF16)


def _inproj(x2d, norm_g, modv, mod_base, rows_per_mod, w_in_b, li, tm):
    n, d = x2d.shape
    in_w = w_in_b.shape[2]

    def mod_idx(j):
        if rows_per_mod is None:
            return lambda i: (mod_base * 6 + j, 0, 0)
        tpm = rows_per_mod // tm
        return lambda i: ((mod_base + i // tpm) * 6 + j, 0, 0)

    def tok(w):
        return pl.BlockSpec((tm, w), lambda i: (i, 0))

    outs = [(MIX, BF16), (MIX, BF16), (MIX, F32), (MIX, F32), (MIX, BF16), (MIX, BF16),
            (NA_WIDTH, BF16), (NA_WIDTH, BF16), (NA_WIDTH, BF16)]
    return pl.pallas_call(
        _inproj_kernel,
        grid=(n // tm,),
        in_specs=[tok(d),
                  pl.BlockSpec((1, d), lambda i: (0, 0)),
                  pl.BlockSpec((1, 1, d), mod_idx(0)),
                  pl.BlockSpec((1, 1, d), mod_idx(1)),
                  _layer_spec(w_in_b, li)],
        out_specs=[tok(w) for w, _ in outs],
        out_shape=[jax.ShapeDtypeStruct((n, w), dt) for w, dt in outs],
        compiler_params=_cparams(1),
        name="inproj",
    )(x2d, norm_g.reshape(1, d), modv, modv, w_in_b)


def _dft_table(t_len):
    kb = min(64, t_len)
    assert t_len % kb == 0
    t = jnp.arange(t_len // 2, dtype=jnp.int32)[None, :]
    k_hi = jnp.arange(t_len // kb, dtype=jnp.int32)[:, None] * kb
    k_lo = jnp.arange(kb, dtype=jnp.int32)[:, None]
    w = 2.0 * np.pi / t_len
    ang_hi = ((k_hi * t) % t_len).astype(F32) * w
    ang_lo = ((k_lo * t) % t_len).astype(F32) * w
    ch, sh = jnp.cos(ang_hi)[:, None, :], jnp.sin(ang_hi)[:, None, :]
    cl, sl = jnp.cos(ang_lo)[None, :, :], jnp.sin(ang_lo)[None, :, :]
    cos = (ch * cl - sh * sl).reshape(t_len, t_len // 2)
    sin = (sh * cl + ch * sl).reshape(t_len, t_len // 2)
    return cos.astype(BF16), sin.astype(BF16)


def _channel_dft_consts():
    cd = np.outer(np.arange(FOURIER_GD), np.arange(FOURIER_GD)) % FOURIER_GD
    ang = 2.0 * np.pi * cd / FOURIER_GD
    eye = np.eye(FOURIER_GROUPS)
    return (jnp.asarray(np.kron(eye, np.cos(ang)), F32),
            jnp.asarray(np.kron(eye, np.sin(ang)), F32))


def _fourier_kernel(u_ref, cos_ref, sin_ref, c64_ref, s64_ref, w_ref, o_ref, pq_ref, pm_ref,
                    *, t_len, tm):
    half = t_len // 2
    bs = min(MXU_DIM, half)
    nb = half // bs
    bi = pl.program_id(1)

    @pl.when(pl.program_id(0) == 0)
    def _():
        norm = (FOURIER_GD * t_len) ** -0.5
        w = w_ref[...]
        a = (_dot_hilo(c64_ref[...], w) * norm).astype(BF16)
        b = (_dot_hilo(s64_ref[...], w) * (-norm)).astype(BF16)
        p_i = lax.broadcasted_iota(jnp.int32, (bs, bs), 0)
        q_i = lax.broadcasted_iota(jnp.int32, (bs, bs), 1)
        rev_shift = jnp.where(q_i == bs - p_i, 1.0, 0.0).astype(BF16)
        first_row = jnp.where(p_i + q_i == 0, 1.0, 0.0).astype(BF16)
        for i in range(nb):
            src = half + (nb - 1 - i) * bs
            r = _dot(rev_shift, u_ref[0, src:src + bs, :])
            if i > 0:
                r = r + _dot(first_row, u_ref[0, src + bs:src + 2 * bs, :])
            u = u_ref[0, i * bs:(i + 1) * bs, :].astype(F32)
            pq_ref[bi, i * bs:(i + 1) * bs, :] = _dot((u + r).astype(BF16), a).astype(BF16)
            pq_ref[bi, half + i * bs:half + (i + 1) * bs, :] = _dot((u - r).astype(BF16),
                                                                    b).astype(BF16)
        pm_ref[bi] = _dot(u_ref[0, half:half + 8, :], a)

    k = pl.program_id(0) * tm + lax.broadcasted_iota(jnp.int32, (tm, 1), 0)
    sign = (1 - 2 * (k & 1)).astype(F32)
    o_ref[0] = (_dot(cos_ref[...], pq_ref[bi, 0:half, :]) + _dot(sin_ref[...], pq_ref[bi, half:, :])
                + sign * pm_ref[bi, 0:1, :]).astype(BF16)


def _fourier(u, tables, c64, s64, w_bd):
    b, t_len, _ = u.shape
    assert t_len % 64 == 0
    tm = min(TOKEN_TILE, t_len)
    half_spec = pl.BlockSpec((tm, t_len // 2), lambda j, i: (j, 0))
    return pl.pallas_call(
        functools.partial(_fourier_kernel, t_len=t_len, tm=tm),
        grid=(t_len // tm, b),
        in_specs=[pl.BlockSpec((1, t_len, MIX), lambda j, i: (jnp.where(j == 0, i, b - 1), 0, 0)),
                  half_spec, half_spec,
                  _const_spec((MIX, MIX)), _const_spec((MIX, MIX)), _const_spec((MIX, MIX))],
        out_specs=pl.BlockSpec((1, tm, MIX), lambda j, i: (i, j, 0)),
        out_shape=jax.ShapeDtypeStruct((b, t_len, MIX), BF16),
        scratch_shapes=[pltpu.VMEM((b, t_len, MIX), BF16), pltpu.VMEM((b, 8, MIX), F32)],
        compiler_params=_cparams(2),
        name="fourier",
    )(u, tables[0], tables[1], c64, s64, w_bd)


def _hgrn_kernel(ql_ref, zfl_ref, zbl_ref, il_ref, gl_ref,
                 qc_ref, zfc_ref, zbc_ref, ic_ref, gc_ref, lb_ref, ng_ref,
                 *rest, n_lat, n_ctx, with_ctx):
    if with_ctx:
        ol_ref, oc_ref, st_ref, qd_ref, ks_ref, vt_ref, eal_ref, os_ref = rest
    else:
        ol_ref, st_ref, qd_ref, ks_ref, vt_ref, eal_ref, os_ref = rest
        oc_ref = None
    C = HGRN_CHUNK
    P2 = 2 * C
    W = MIX
    lane_head = lax.broadcasted_iota(jnp.int32, (1, W), 1) // HGRN_DK
    head_mask = [lane_head == h for h in range(HGRN_HEADS)]
    bd = (lax.broadcasted_iota(jnp.int32, (W, W), 0) // HGRN_DK
          == lax.broadcasted_iota(jnp.int32, (W, W), 1) // HGRN_DK)
    bd_ones = jnp.where(bd, 1.0, 0.0).astype(BF16)

    def stack_heads(x):
        return jnp.concatenate([jnp.where(m, x, jnp.zeros_like(x)) for m in head_mask], axis=0)

    def group_chunks(n_chunks):
        return 4 if n_chunks % 4 == 0 else 2

    def phase_a(q_ref, zf_ref, zb_ref, i_ref, n_chunks, need_o):
        gc = group_chunks(n_chunks)
        gt = gc * C
        t_i = lax.broadcasted_iota(jnp.int32, (gt, gt), 0)
        s_i = lax.broadcasted_iota(jnp.int32, (gt, gt), 1)
        same = (t_i // C) == (s_i // C)
        allow = (same & (s_i <= t_i), same & (s_i >= t_i))
        allow_b16 = tuple(jnp.where(m, 1.0, 0.0).astype(BF16) for m in allow)
        t4 = lax.broadcasted_iota(jnp.int32, (gt, HGRN_HEADS * gt), 0)
        s4 = lax.broadcasted_iota(jnp.int32, (gt, HGRN_HEADS * gt), 1) % gt
        same4 = (t4 // C) == (s4 // C)
        allow4 = (same4 & (s4 <= t4), same4 & (s4 >= t4))
        pair4 = (t4 // P2) == (s4 // P2)
        cross4 = (pair4 & (t4 // C == s4 // C + 1), pair4 & (t4 // C + 1 == s4 // C))
        ones_row = jnp.ones((C, W), F32)

        def body(g, carry):
            rows = pl.ds(pl.multiple_of(g * gt, gt), gt)
            v = i_ref[0, rows, :]
            vs = stack_heads(v) if need_o else None
            for j in range(gc // 2):
                vt_ref[g * (gc // 2) + j] = v[j * P2:(j + 1) * P2, :].T
            fs, parts = [], []
            for d, z_ref in ((0, zf_ref), (1, zb_ref)):
                lb = lb_ref[d:d + 1, :]
                f = lb + (1.0 - lb) * _sigmoid(z_ref[0, rows, :])
                lf = jnp.log(f)
                fs.append(f)
                parts.append(_split2(lf))
            a_s = [_dot(allow_b16[d], parts[d][0]) + _dot(allow_b16[d], parts[d][1])
                   for d in (0, 1)]
            scs = []
            for d in (0, 1):
                a = a_s[d]
                kd = (1.0 - fs[d]) * jnp.exp(-a)
                ends = [a[c * C + C - 1:c * C + C, :] if d == 0 else a[c * C:c * C + 1, :]
                        for c in range(gc)]
                eal = [jnp.exp(e) for e in ends]
                ebc = [jnp.broadcast_to(e, (C, W)) for e in eal]
                is_first = [(c % 2 == 0) == (d == 0) for c in range(gc)]
                k_mul = jnp.concatenate(
                    [jnp.broadcast_to(eal[c] * eal[c ^ 1], (C, W)) if is_first[c] else ebc[c]
                     for c in range(gc)], axis=0)
                ks_ref[d, rows, :] = (kd * k_mul).astype(BF16)
                n8 = (gc // 2) * 8
                eal_ref[d, pl.ds(pl.multiple_of(g * n8, n8), n8), :] = jnp.concatenate(
                    [jnp.broadcast_to(eal[2 * j] * eal[2 * j + 1], (8, W)) for j in range(gc // 2)],
                    axis=0)
                if need_o:
                    qd = q_ref[0, rows, :].astype(F32) * jnp.exp(a)
                    q_mul = jnp.concatenate([ones_row if is_first[c] else ebc[c ^ 1]
                                             for c in range(gc)], axis=0)
                    qd2 = (qd * q_mul).astype(BF16)
                    qd_ref[d, rows, :] = qd2
                    s2 = _dot_nt(jnp.concatenate([qd.astype(BF16), qd2], axis=0),
                                 stack_heads(kd.astype(BF16)))
                    scs.append(jnp.where(allow4[d], s2[:gt],
                                         jnp.where(cross4[d], s2[gt:], 0.0)).astype(BF16))
            if need_o:
                os_ref[rows, :] = _dot(scs[0], vs) + _dot(scs[1], vs)
            return carry

        lax.fori_loop(0, n_chunks // gc, body, 0, unroll=8)

    def phase_b(n_chunks, need_o):
        n_steps = n_chunks // 2
        gc = next(t for t in (4, 2, 1) if n_steps % t == 0)
        pw = 2 * HGRN_DK
        pairs = [slice(p * pw, (p + 1) * pw) for p in range(HGRN_HEADS // 2)]
        same_head = (lax.broadcasted_iota(jnp.int32, (pw, pw), 0) // HGRN_DK
                     == lax.broadcasted_iota(jnp.int32, (pw, pw), 1) // HGRN_DK)

        def step_rows(m):
            return pl.ds(pl.multiple_of(m * P2, P2), P2)

        def body(g, carry):
            idx = [[g * gc + c, n_steps - 1 - (g * gc + c)] for c in range(gc)]

            def updates(c):
                out = []
                for d, m in enumerate(idx[c]):
                    vt = vt_ref[m]
                    ks = ks_ref[d, step_rows(m), :]
                    out.append([jnp.where(same_head, _dot(vt[s, :], ks[:, s]), 0.0) for s in pairs])
                return out

            upd = {0: updates(0)}
            inters = []
            for c in range(gc):
                if c + 1 < gc:
                    upd[c + 1] = updates(c + 1)
                for d in (0, 1):
                    m = idx[c][d]
                    rows = step_rows(m)
                    eal = eal_ref[d, pl.ds(pl.multiple_of(m * 8, 8), 8), :][0:1, :]
                    sts = [st_ref[d, p] for p in range(len(pairs))]
                    if need_o:
                        qd = qd_ref[d, rows, :]
                        inter = [_dot_nt(qd[:, s], st.astype(BF16)) for s, st in zip(pairs, sts)]
                        inters.append((rows, jnp.concatenate(inter, axis=1)))
                    for p, s in enumerate(pairs):
                        st_ref[d, p] = sts[p] * eal[:, s] + upd[c][d][p]
            for rows, inter in inters:
                os_ref[rows, :] = os_ref[rows, :] + inter
            return carry

        lax.fori_loop(0, n_steps // gc, body, 0, unroll=4)

    def phase_c(g_ref, o_ref, n_rows):
        rt = next(t for t in (256, 128, 64) if n_rows % t == 0)

        def body(t, carry):
            rows = pl.ds(pl.multiple_of(t * rt, rt), rt)
            o = os_ref[rows, :]
            hi, lo = _split2(o * o)
            ms = (_dot(hi, bd_ones) + _dot(lo, bd_ones)) * (1.0 / HGRN_DK)
            y = o * lax.rsqrt(ms + EPS) * ng_ref[...]
            g = g_ref[0, rows, :].astype(F32)
            o_ref[0, rows, :] = (y * (g * _sigmoid(g))).astype(BF16)
            return carry

        lax.fori_loop(0, n_rows // rt, body, 0)

    st_ref[...] = jnp.zeros_like(st_ref)
    phase_a(qc_ref, zfc_ref, zbc_ref, ic_ref, n_ctx, with_ctx)
    phase_b(n_ctx, with_ctx)
    if with_ctx:
        phase_c(gc_ref, oc_ref, n_ctx * C)
    phase_a(ql_ref, zfl_ref, zbl_ref, il_ref, n_lat, True)
    phase_b(n_lat, True)
    phase_c(gl_ref, ol_ref, n_lat * C)


def _hgrn(lat, ctx, lb, norm_g4, with_ctx):
    b, s, _ = lat[0].shape
    l = ctx[0].shape[1]
    assert s % (2 * HGRN_CHUNK) == 0 and l % (2 * HGRN_CHUNK) == 0

    def seq(t):
        return pl.BlockSpec((1, t, MIX), lambda i: (i, 0, 0))

    out_specs = [pl.BlockSpec((1, s, MIX), lambda i: (i, 0, 0))]
    out_shape = [jax.ShapeDtypeStruct((b, s, MIX), BF16)]
    t_max = max(s, l)
    scratch = [pltpu.VMEM((2, HGRN_HEADS // 2, 2 * HGRN_DK, 2 * HGRN_DK), F32),
               pltpu.VMEM((2, t_max, MIX), BF16),
               pltpu.VMEM((2, t_max, MIX), BF16),
               pltpu.VMEM((t_max // (2 * HGRN_CHUNK), MIX, 2 * HGRN_CHUNK), BF16),
               pltpu.VMEM((2, t_max // (2 * HGRN_CHUNK) * 8, MIX), F32),
               pltpu.VMEM((t_max, MIX), F32)]
    if with_ctx:
        out_specs.append(pl.BlockSpec((1, l, MIX), lambda i: (i, 0, 0)))
        out_shape.append(jax.ShapeDtypeStruct((b, l, MIX), BF16))
    res = pl.pallas_call(
        functools.partial(_hgrn_kernel, n_lat=s // HGRN_CHUNK, n_ctx=l // HGRN_CHUNK,
                          with_ctx=with_ctx),
        grid=(b,),
        in_specs=[seq(s)] * 5 + [seq(l)] * 5 + [_const_spec((2, MIX)), _const_spec((1, MIX))],
        out_specs=out_specs,
        out_shape=out_shape,
        scratch_shapes=scratch,
        compiler_params=_cparams(1),
        name="hgrn",
    )(*lat, *ctx, lb, norm_g4)
    return (res[0], res[1]) if with_ctx else (res[0], None)


def _na_bias_table(rpb):
    n_h, nr, nc = rpb.shape
    w = GRID_W
    assert 2 * w == 128 and nc <= w
    padded = jnp.pad(rpb.astype(F32), ((0, 0), (0, 16 - nr), (0, 2 * w - nc)))
    return pl.pallas_call(
        _na_bias_kernel,
        grid=(n_h // 2,),
        in_specs=[pl.BlockSpec((2, 16, 2 * w), lambda i: (i, 0, 0))],
        out_specs=pl.BlockSpec((1, nr - 1, 2 * w, 2 * w), lambda i: (i, 0, 0, 0)),
        out_shape=jax.ShapeDtypeStruct((n_h // 2, nr - 1, 2 * w, 2 * w), F32),
        compiler_params=_cparams(1),
        name="na_bias",
    )(padded)


def _na_bias_kernel(rpb_ref, o_ref):
    w = GRID_W
    q = lax.broadcasted_iota(jnp.int32, (w, 2 * w), 0)
    lane = lax.broadcasted_iota(jnp.int32, (w, 2 * w), 1)
    kc = lane % w
    win = jnp.clip(q - NA_KW // 2, 0, w - NA_KW)
    valid = (kc >= win) & (kc < win + NA_KW)
    lower = lane < w
    shift_lo = 2 * w - (NA_KW - 1)
    shift_hi = shift_lo + w
    for h in range(2):
        for j in range(2 * NA_KH - 2):
            x0 = jnp.broadcast_to(rpb_ref[h, j:j + 1, :], (w, 2 * w))
            x1 = jnp.broadcast_to(rpb_ref[h, j + 1:j + 2, :], (w, 2 * w))
            t0 = pltpu.roll(x0, shift_lo, 1, stride=1, stride_axis=0)
            t1 = pltpu.roll(x1, shift_hi % (2 * w), 1, stride=1, stride_axis=0)
            o_ref[0, j, h * w:(h + 1) * w, :] = jnp.where(valid, jnp.where(lower, t0, t1), NEG_INF)


def _softmax_pv(s_list, v_list):
    m = functools.reduce(jnp.maximum, [jnp.max(s, axis=-1, keepdims=True) for s in s_list])
    p_list = [jnp.exp(s - m) for s in s_list]
    l = functools.reduce(jnp.add, [jnp.sum(p, axis=-1, keepdims=True) for p in p_list])
    o = functools.reduce(jnp.add, [_dot(p.astype(BF16), v) for p, v in zip(p_list, v_list)])
    return o * (1.0 / l)


NA_ROWS_PER_STEP = 32


def _na_kernel(q_ref, k_ref, v_ref, kc_ref, vc_ref, bias_ref, o_ref, *, rows):
    lane = lax.broadcasted_iota(jnp.int32, (1, 2 * NA_HEAD_DIM), 1)
    first = lane < NA_HEAD_DIM
    kc = kc_ref[0]
    vc = vc_ref[0]
    win = NA_KH * GRID_W
    nq = 2 * GRID_W

    def body(g, carry):
        qs, kws, vws, dis, qrows = [], [], [], [], []
        for j in range(NA_ROWS_PER_STEP):
            r = g * NA_ROWS_PER_STEP + j
            rs = jnp.clip(r - NA_KH // 2, 0, rows - NA_KH)
            dis.append(r - rs)
            qrows.append(pl.ds(pl.multiple_of(r * GRID_W, GRID_W), GRID_W))
            krows = pl.ds(pl.multiple_of(rs * GRID_W, GRID_W), win)
            q = q_ref[0, qrows[-1], :]
            qs += [jnp.where(first, q, jnp.zeros_like(q)), jnp.where(first, jnp.zeros_like(q), q)]
            kws.append(k_ref[0, krows, :])
            vws.append(v_ref[0, krows, :])
        qs = jnp.concatenate(qs, axis=0)
        sx = _dot_nt(qs, kc)
        pws, pxs, inv_ls = [], [], []
        for j in range(NA_ROWS_PER_STEP):
            ri0 = NA_KH - 1 - dis[j]
            bias = jnp.concatenate([bias_ref[0, ri0 + 2 * c] for c in range(NA_KH // 2)], axis=1)
            sw = _dot_nt(qs[j * nq:(j + 1) * nq], kws[j]) + bias
            sxj = sx[j * nq:(j + 1) * nq]
            m = jnp.maximum(jnp.max(sw, axis=-1, keepdims=True), jnp.max(sxj, axis=-1, keepdims=True))
            pw = jnp.exp(sw - m)
            px = jnp.exp(sxj - m)
            inv_ls.append(1.0 / (jnp.sum(pw, axis=-1, keepdims=True) + jnp.sum(px, axis=-1, keepdims=True)))
            pws.append(pw.astype(BF16))
            pxs.append(px.astype(BF16))
        ows = [_dot(pws[j], vws[j]) for j in range(NA_ROWS_PER_STEP)]
        oc = _dot(jnp.concatenate(pxs, axis=0), vc)
        for j in range(NA_ROWS_PER_STEP):
            o = (ows[j] + oc[j * nq:(j + 1) * nq]) * inv_ls[j]
            o_ref[0, qrows[j], :] = jnp.where(first, o[:GRID_W], o[GRID_W:]).astype(BF16)
        return carry

    lax.fori_loop(0, rows // NA_ROWS_PER_STEP, body, 0)


def _na_latent(q, k, v, kc, vc, bias):
    b, s, _ = q.shape
    l = kc.shape[1]
    rows = s // GRID_W
    assert s % GRID_W == 0 and rows >= NA_KH and rows % NA_ROWS_PER_STEP == 0
    hp = 2 * NA_HEAD_DIM

    def seq(t):
        return pl.BlockSpec((1, t, hp), lambda i, j: (i, 0, j))

    return pl.pallas_call(
        functools.partial(_na_kernel, rows=rows),
        grid=(b, NA_HEADS // 2),
        in_specs=[seq(s), seq(s), seq(s), seq(l), seq(l),
                  pl.BlockSpec((1, 2 * NA_KH - 2, 2 * GRID_W, 2 * GRID_W),
                               lambda i, j: (j, 0, 0, 0))],
        out_specs=seq(s),
        out_shape=jax.ShapeDtypeStruct((b, s, NA_WIDTH), BF16),
        compiler_params=_cparams(2),
        name="na_latent",
    )(q, k, v, kc, vc, bias)


def _na_ctx_kernel(q_ref, k_ref, v_ref, o_ref):
    lane = lax.broadcasted_iota(jnp.int32, (1, 2 * NA_HEAD_DIM), 1)
    first = lane < NA_HEAD_DIM
    for p in range(NA_HEADS // 2):
        cols = slice(p * 2 * NA_HEAD_DIM, (p + 1) * 2 * NA_HEAD_DIM)
        q, k, v = q_ref[0, :, cols], k_ref[0, :, cols], v_ref[0, :, cols]
        outs = []
        for h in range(2):
            qh = jnp.where(first if h == 0 else ~first, q, jnp.zeros_like(q))
            outs.append(_softmax_pv([_dot_nt(qh, k)], [v]))
        o_ref[0, :, cols] = jnp.where(first, outs[0], outs[1]).astype(BF16)


def _na_context(q, k, v):
    b, l, _ = q.shape
    spec = pl.BlockSpec((1, l, NA_WIDTH), lambda i: (i, 0, 0))
    return pl.pallas_call(
        _na_ctx_kernel,
        grid=(b,),
        in_specs=[spec, spec, spec],
        out_specs=spec,
        out_shape=jax.ShapeDtypeStruct((b, l, NA_WIDTH), BF16),
        compiler_params=_cparams(1),
        name="na_context",
    )(q, k, v)


def _ffn_kernel(x_ref, f_ref, h_ref, n_ref, g1_ref, sh_ref, sc_ref, g2_ref, ng_ref, fg_ref,
                wo_ref, wg_ref, wu_ref, wd_ref, o_ref, act_ref, *, final):
    y = (_dot(f_ref[...], wo_ref[0:MIX, :]) + _dot(h_ref[...], wo_ref[MIX:2 * MIX, :])
         + _dot(n_ref[...], wo_ref[2 * MIX:, :]))
    x1 = x_ref[...] + g1_ref[0] * y
    hn = x1 * lax.rsqrt(jnp.mean(x1 * x1, axis=-1, keepdims=True) + EPS) * ng_ref[...]
    hm = (hn * (1.0 + sc_ref[0]) + sh_ref[0]).astype(BF16)
    d_ff = wg_ref.shape[1]
    for j in range(d_ff // FF_CHUNK):
        cols = slice(j * FF_CHUNK, (j + 1) * FF_CHUNK)
        g = _dot(hm, wg_ref[:, cols])
        u = _dot(hm, wu_ref[:, cols])
        act_ref[:, cols] = (g * _sigmoid(g) * u).astype(BF16)
    x2 = x1 + g2_ref[0] * _dot(act_ref[...], wd_ref[...])
    if final:
        x2 = x2 * lax.rsqrt(jnp.mean(x2 * x2, axis=-1, keepdims=True) + EPS) * fg_ref[...]
    o_ref[...] = x2


def _ffn(x2d, f, h, n, norm_g, final_g, modv, mod_base, rows_per_mod, wo, wg, wu, wd, li, tm, final):
    nt, d = x2d.shape
    d_ff = wg.shape[2]
    assert d_ff % FF_CHUNK == 0

    def mod_idx(j):
        if rows_per_mod is None:
            return lambda i: (mod_base * 6 + j, 0, 0)
        tpm = rows_per_mod // tm
        return lambda i: ((mod_base + i // tpm) * 6 + j, 0, 0)

    def tok(w):
        return pl.BlockSpec((tm, w), lambda i: (i, 0))

    def vec():
        return pl.BlockSpec((1, d), lambda i: (0, 0))

    return pl.pallas_call(
        functools.partial(_ffn_kernel, final=final),
        grid=(nt // tm,),
        in_specs=[tok(d), tok(MIX), tok(MIX), tok(NA_WIDTH),
                  pl.BlockSpec((1, 1, d), mod_idx(2)), pl.BlockSpec((1, 1, d), mod_idx(3)),
                  pl.BlockSpec((1, 1, d), mod_idx(4)), pl.BlockSpec((1, 1, d), mod_idx(5)),
                  vec(), vec(),
                  _layer_spec(wo, li), _layer_spec(wg, li), _layer_spec(wu, li),
                  _layer_spec(wd, li)],
        out_specs=tok(d),
        out_shape=jax.ShapeDtypeStruct((nt, d), F32),
        scratch_shapes=[pltpu.VMEM((tm, d_ff), BF16)],
        compiler_params=_cparams(1),
        name="outproj_ffn",
    )(x2d, f, h, n, modv, modv, modv, modv, norm_g.reshape(1, d), final_g.reshape(1, d),
      wo, wg, wu, wd)


def _token_tile(n, largest):
    for tm in (1024, 512, 256, 128, 64, 32, 16, 8):
        if tm <= largest and n % tm == 0:
            return tm
    raise ValueError(f"token count {n} is not a multiple of 8")


def kernel(x, c, ctx, c_ctx, w_mod, b_mod, norm1_g, w_in, fourier_w, hgrn_lb, hgrn_norm_g, na_rpb,
           w_out, norm2_g, w_ffn_gate, w_ffn_up, w_ffn_down, final_norm_g):
    b, s, d = x.shape
    l = ctx.shape[1]
    depth = w_mod.shape[0]
    assert b < MOD_ROWS and d == 2 * MIX + NA_WIDTH

    cc = jnp.zeros((MOD_ROWS, d), F32).at[:b].set(c).at[b].set(c_ctx)
    mod = _modulation(cc, w_mod, b_mod)
    modv = mod.reshape(depth * MOD_ROWS * 6, 1, d)

    lbp = jax.nn.softmax(hgrn_lb.astype(F32), axis=0)
    lower = jnp.cumsum(lbp, axis=0) - lbp[0:1]
    norm_g4 = jnp.tile(hgrn_norm_g.astype(F32), (1, HGRN_HEADS))[:, None, :]

    tab_s, tab_l = _dft_table(s), _dft_table(l)
    c64, s64 = _channel_dft_consts()
    eye_g = jnp.eye(FOURIER_GROUPS, dtype=F32)

    tm_s, tm_c = _token_tile(s, TOKEN_TILE), _token_tile(b * l, TOKEN_TILE)
    tm_in_s, tm_in_c = _token_tile(s, INPROJ_TOKEN_TILE), _token_tile(b * l, INPROJ_TOKEN_TILE)
    xl = x.reshape(b * s, d)
    xc = ctx.reshape(b * l, d)
    w_in_b, wo, wg = w_in.astype(BF16), w_out.astype(BF16), w_ffn_gate.astype(BF16)
    wu, wd = w_ffn_up.astype(BF16), w_ffn_down.astype(BF16)
    for li in range(depth):
        with_ctx = li < depth - 1
        final = li == depth - 1
        base = li * MOD_ROWS
        ul = _inproj(xl, norm1_g[li], modv, base, s, w_in_b, li, tm_in_s)
        uc = _inproj(xc, norm1_g[li], modv, base + b, None, w_in_b, li, tm_in_c)
        ul = [a.reshape(b, s, -1) for a in ul]
        uc = [a.reshape(b, l, -1) for a in uc]

        w_bd = (eye_g[:, None, :, None] * fourier_w[li][:, :, None, :]).reshape(MIX, MIX)
        f_lat = _fourier(ul[0], tab_s, c64, s64, w_bd)
        h_lat, h_ctx = _hgrn(ul[1:6], uc[1:6], lower[li], norm_g4[li], with_ctx)
        n_lat = _na_latent(ul[6], ul[7], ul[8], uc[7], uc[8], _na_bias_table(na_rpb[li]))

        if with_ctx:
            f_ctx = _fourier(uc[0], tab_l, c64, s64, w_bd)
            n_ctx = _na_context(uc[6], uc[7], uc[8])
            xc = _ffn(xc, f_ctx.reshape(b * l, -1), h_ctx.reshape(b * l, -1),
                      n_ctx.reshape(b * l, -1), norm2_g[li], final_norm_g, modv, base + b, None,
                      wo, wg, wu, wd, li, tm_c, False)
        xl = _ffn(xl, f_lat.reshape(b * s, -1), h_lat.reshape(b * s, -1), n_lat.reshape(b * s, -1),
                  norm2_g[li], final_norm_g, modv, base, s, wo, wg, wu, wd, li, tm_s, final)
    return xl.reshape(b, s, d)
```
